```python
import math
import jax, jax.numpy as jnp
from jax import lax
import numpy as np

D_MODEL = 2048
BATCH = 4
SEQ = 4096
DEPTH = 1

HEAD_DIM = 128
SB_HEADS = D_MODEL // (2 * HEAD_DIM)
RET_HEADS = D_MODEL // (2 * HEAD_DIM)
SB_WIDTH = SB_HEADS * HEAD_DIM
RET_WIDTH = RET_HEADS * HEAD_DIM
MIX_WIDTH = SB_WIDTH + RET_WIDTH
IN_COLS = 3 * SB_WIDTH + 4 * RET_WIDTH
Q_BLOCK = 128
RET_CHUNK = 128
ROPE_BASE = 10000.0
N_EXPERTS = 64
TOP_K = 8
N_GROUPS = 8
TOPK_GROUP = 4
EXPERT_DIM = D_MODEL // 4
SHARED_DIM = D_MODEL // 4
ROUTED_SCALE = 2.5
MOE_BLOCK = 128
EPS = 1e-6

kernel_name = 'hybrid_sbattn_retention_moe_adaln'


def rmsnorm(x, g):
    xf = x.astype(jnp.float32)
    y = xf * lax.rsqrt(jnp.mean(xf * xf, axis=-1, keepdims=True) + EPS)
    return (y * g.astype(jnp.float32)).astype(x.dtype)


def modulate(h, shift, scale):
    return h * (1.0 + scale[:, None, :]) + shift[:, None, :]


def rope_tables(S):
    pos = jnp.arange(S, dtype=jnp.float32)
    inv_freq = ROPE_BASE ** (-jnp.arange(0, HEAD_DIM, 2, dtype=jnp.float32) / HEAD_DIM)
    ang = pos[:, None] * inv_freq[None, :]
    return jnp.cos(ang)[:, None, :], jnp.sin(ang)[:, None, :]


def apply_rope(t, cos, sin):
    t1, t2 = t[..., :HEAD_DIM // 2], t[..., HEAD_DIM // 2:]
    return jnp.concatenate([t1 * cos - t2 * sin, t1 * sin + t2 * cos], axis=-1)


def stick_breaking_attention(q, k, v):
    B, S, H, dh = q.shape
    qf = jnp.transpose(q.astype(jnp.float32), (0, 2, 1, 3)) * (dh ** -0.5)
    kf = jnp.transpose(k.astype(jnp.float32), (0, 2, 1, 3))
    vf = jnp.transpose(v.astype(jnp.float32), (0, 2, 1, 3))
    nq = S // Q_BLOCK
    qb = jnp.transpose(qf.reshape(B, H, nq, Q_BLOCK, dh), (2, 0, 1, 3, 4))
    key_pos = jnp.arange(S, dtype=jnp.int32)

    def block(args):
        q_blk, t0 = args
        z = jnp.einsum('bhqd,bhkd->bhqk', q_blk, kf)
        qpos = t0 + jnp.arange(Q_BLOCK, dtype=jnp.int32)
        causal = key_pos[None, :] < qpos[:, None]
        log_surv = jnp.where(causal, jax.nn.log_sigmoid(-z), 0.0)
        after = lax.cumsum(log_surv, axis=3, reverse=True) - log_surv
        a = jnp.where(causal, jnp.exp(jax.nn.log_sigmoid(z) + after), 0.0)
        return jnp.einsum('bhqk,bhkd->bhqd', a, vf)

    ob = lax.map(block, (qb, jnp.arange(nq, dtype=jnp.int32) * Q_BLOCK))
    return jnp.transpose(ob, (1, 0, 3, 2, 4)).reshape(B, S, H * dh)


def chunkwise_retention(q, k, v):
    B, S, H, dh = q.shape
    C = RET_CHUNK
    N = S // C

    def to_chunks(t):
        return jnp.transpose(t.astype(jnp.float32), (0, 2, 1, 3)).reshape(B, H, N, C, dh)

    qc, kc, vc = to_chunks(q), to_chunks(k) * (dh ** -0.5), to_chunks(v)
    log_g = jnp.log(1.0 - jnp.exp2(-5.0 - jnp.arange(H, dtype=jnp.float32)))
    i = jnp.arange(C, dtype=jnp.float32)
    diff = i[:, None] - i[None, :]
    lower = diff >= 0
    dmat = jnp.where(lower[None], jnp.exp(jnp.where(lower, diff, 0.0)[None] * log_g[:, None, None]), 0.0)
    scores = jnp.einsum('bhnid,bhnjd->bhnij', qc, kc) * dmat[None, :, None]
    intra = jnp.einsum('bhnij,bhnje->bhnie', scores, vc)
    k_decay = jnp.exp((C - 1.0 - i)[None, :] * log_g[:, None])
    chunk_kv = jnp.einsum('bhncd,bhnce->bhnde', kc * k_decay[None, :, None, :, None], vc)
    chunk_decay = jnp.exp(C * log_g)[None, :, None, None]

    def step(state, kv):
        return state * chunk_decay + kv, state

    _, r_prev = lax.scan(step, jnp.zeros((B, H, dh, dh), jnp.float32), jnp.moveaxis(chunk_kv, 2, 0))
    r_prev = jnp.moveaxis(r_prev, 0, 2)
    q_decay = jnp.exp((i + 1.0)[None, :] * log_g[:, None])
    cross = jnp.einsum('bhncd,bhnde->bhnce', qc * q_decay[None, :, None, :, None], r_prev)
    out = (intra + cross).reshape(B, H, S, dh)
    return jnp.transpose(out, (0, 2, 1, 3))


def token_mixer(h, w_in, ret_gn_g, w_out, cos, sin):
    B, S, _ = h.shape
    u = h @ w_in
    cuts = [SB_WIDTH, 2 * SB_WIDTH, 3 * SB_WIDTH,
            3 * SB_WIDTH + RET_WIDTH, 3 * SB_WIDTH + 2 * RET_WIDTH, 3 * SB_WIDTH + 3 * RET_WIDTH]
    sbq, sbk, sbv, rq, rk, rv, rg = jnp.split(u, cuts, axis=-1)
    heads = lambda t, n: t.reshape(B, S, n, HEAD_DIM)
    sb = stick_breaking_attention(heads(sbq, SB_HEADS), heads(sbk, SB_HEADS), heads(sbv, SB_HEADS))
    ret = chunkwise_retention(apply_rope(heads(rq, RET_HEADS).astype(jnp.float32), cos, sin),
                              apply_rope(heads(rk, RET_HEADS).astype(jnp.float32), cos, sin),
                              heads(rv, RET_HEADS))
    ret = ret * lax.rsqrt(jnp.mean(ret * ret, axis=-1, keepdims=True) + EPS)
    ret = ret.reshape(B, S, RET_WIDTH) * ret_gn_g.astype(jnp.float32) * jax.nn.silu(rg.astype(jnp.float32))
    o = jnp.concatenate([sb.astype(h.dtype), ret.astype(h.dtype)], axis=-1)
    return o @ w_out


def moe_ffn(h2, w_router, router_bias, w_gate, w_up, w_down, ws_gate, ws_up, ws_down):
    T, D = h2.shape
    E = N_EXPERTS
    scores = jax.nn.sigmoid((h2 @ w_router).astype(jnp.float32))
    choice = scores + router_bias.astype(jnp.float32)
    grp = choice.reshape(T, N_GROUPS, E // N_GROUPS)
    grp_score = jnp.sum(lax.top_k(grp, 2)[0], axis=-1)
    _, gidx = lax.top_k(grp_score, TOPK_GROUP)
    gmask = jnp.any(gidx[..., None] == jnp.arange(N_GROUPS)[None, None, :], axis=1)
    emask = jnp.repeat(gmask, E // N_GROUPS, axis=1)
    _, eidx = lax.top_k(jnp.where(emask, choice, -jnp.inf), TOP_K)
    gsel = jnp.take_along_axis(scores, eidx, axis=1)
    gsel = gsel / jnp.sum(gsel, axis=-1, keepdims=True) * ROUTED_SCALE

    A = T * TOP_K
    flat_e = eidx.reshape(-1).astype(jnp.int32)
    flat_t = jnp.repeat(jnp.arange(T, dtype=jnp.int32), TOP_K)
    flat_w = gsel.reshape(-1)
    order = jnp.argsort(flat_e)
    se, st, sw = flat_e[order], flat_t[order], flat_w[order]
    counts = jax.ops.segment_sum(jnp.ones_like(flat_e), flat_e, num_segments=E)
    padded = ((counts + MOE_BLOCK - 1) // MOE_BLOCK) * MOE_BLOCK
    start = jnp.cumsum(counts) - counts
    pend = jnp.cumsum(padded)
    pstart = pend - padded
    pos = pstart[se] + (jnp.arange(A, dtype=jnp.int32) - start[se])
    P = A + E * MOE_BLOCK
    n_blk = P // MOE_BLOCK
    row_tok = jnp.full((P,), T, jnp.int32).at[pos].set(st)
    row_w = jnp.zeros((P,), jnp.float32).at[pos].set(sw)
    blk_expert = jnp.clip(jnp.searchsorted(pend, jnp.arange(n_blk, dtype=jnp.int32) * MOE_BLOCK, side='right'), 0, E - 1)
    h_pad = jnp.concatenate([h2, jnp.zeros((1, D), h2.dtype)], axis=0)

    def expert_block(y, xs):
        b, e = xs
        rows = lax.dynamic_slice(row_tok, (b * MOE_BLOCK,), (MOE_BLOCK,))
        gw = lax.dynamic_slice(row_w, (b * MOE_BLOCK,), (MOE_BLOCK,))
        xb = h_pad[rows]
        out = (jax.nn.silu(xb @ w_gate[e]) * (xb @ w_up[e])) @ w_down[e]
        return y.at[rows].add((out * gw[:, None]).astype(y.dtype)), None

    y, _ = lax.scan(expert_block, jnp.zeros((T + 1, D), h2.dtype),
                    (jnp.arange(n_blk, dtype=jnp.int32), blk_expert))
    shared = (jax.nn.silu(h2 @ ws_gate) * (h2 @ ws_up)) @ ws_down
    return y[:T] + shared


def setup_inputs(seed: int = 0) -> dict:
    key = jax.random.key(seed)
    ks = jax.random.split(key, 24)
    D, E, F, Fs = D_MODEL, N_EXPERTS, EXPERT_DIM, SHARED_DIM

    def nrm(k, shape, fan_in, scale=1.0):
        return jax.random.normal(k, shape, jnp.float32) * (scale * fan_in ** -0.5)

    def gain(k, shape):
        return 1.0 + 0.02 * jax.random.normal(k, shape, jnp.float32)

    return {
        'x': jax.random.normal(ks[0], (BATCH, SEQ, D), jnp.float32),
        'c': jax.random.normal(ks[1], (BATCH, D), jnp.float32),
        'w_ada': nrm(ks[2], (DEPTH, D, 6 * D), D, 0.5),
        'b_ada': 0.02 * jax.random.normal(ks[3], (DEPTH, 6 * D), jnp.float32),
        'norm1_g': gain(ks[4], (DEPTH, D)),
        'w_in': nrm(ks[5], (DEPTH, D, IN_COLS), D),
        'ret_gn_g': gain(ks[6], (DEPTH, RET_WIDTH)),
        'w_out': nrm(ks[7], (DEPTH, MIX_WIDTH, D), MIX_WIDTH),
        'norm2_g': gain(ks[8], (DEPTH, D)),
        'w_router': nrm(ks[9], (DEPTH, D, E), D),
        'router_bias': 0.01 * jax.random.normal(ks[10], (DEPTH, E), jnp.float32),
        'w_gate': nrm(ks[11], (DEPTH, E, D, F), D),
        'w_up': nrm(ks[12], (DEPTH, E, D, F), D),
        'w_down': nrm(ks[13], (DEPTH, E, F, D), F),
        'ws_gate': nrm(ks[14], (DEPTH, D, Fs), D),
        'ws_up': nrm(ks[15], (DEPTH, D, Fs), D),
        'ws_down': nrm(ks[16], (DEPTH, Fs, D), Fs),
        'w_ada_final': nrm(ks[17], (D, 2 * D), D, 0.5),
        'b_ada_final': 0.02 * jax.random.normal(ks[18], (2 * D,), jnp.float32),
        'norm_f_g': gain(ks[19], (D,)),
    }


def reference(x, c, w_ada, b_ada, norm1_g, w_in, ret_gn_g, w_out, norm2_g, w_router, router_bias,
              w_gate, w_up, w_down, ws_gate, ws_up, ws_down, w_ada_final, b_ada_final, norm_f_g):
    B, S, D = x.shape
    cos, sin = rope_tables(S)
    cs = jax.nn.silu(c)
    for l in range(DEPTH):
        mod = cs @ w_ada[l] + b_ada[l]
        sh1, sc1, g1, sh2, sc2, g2 = jnp.split(mod, 6, axis=-1)
        h = modulate(rmsnorm(x, norm1_g[l]), sh1, sc1)
        x = x + g1[:, None, :] * token_mixer(h, w_in[l], ret_gn_g[l], w_out[l], cos, sin)
        h = modulate(rmsnorm(x, norm2_g[l]), sh2, sc2)
        y = moe_ffn(h.reshape(B * S, D), w_router[l], router_bias[l], w_gate[l], w_up[l], w_down[l],
                    ws_gate[l], ws_up[l], ws_down[l])
        x = x + g2[:, None, :] * y.reshape(B, S, D)
    modf = cs @ w_ada_final + b_ada_final
    shf, scf = jnp.split(modf, 2, axis=-1)
    return modulate(rmsnorm(x, norm_f_g), shf, scf)
```

```python
import functools

import jax
import jax.numpy as jnp
from jax import lax
from jax.experimental import pallas as pl
from jax.experimental.pallas import tpu as pltpu

F32 = jnp.float32
BF16 = jnp.bfloat16
I32 = jnp.int32
U32 = jnp.uint32

HEAD_DIM = 128
N_EXPERTS = 64
TOP_K = 8
N_GROUPS = 8
TOPK_GROUP = 4
ROUTED_SCALE = 2.5
EPS = 1e-6
ROPE_BASE = 10000.0

LANES = 128
ROW_WORDS = 8
MOE_BLOCK = 256
V7X_VMEM_LIMIT = 56 * 1024 * 1024
SB_SKIP_BOUND = -115.0


def _cparams(sem, vmem=V7X_VMEM_LIMIT):
    return pltpu.CompilerParams(dimension_semantics=sem, vmem_limit_bytes=vmem)


def _silu(v):
    return v * jax.nn.sigmoid(v)


def _pack_pair(hi_f32, lo_f32):
    hw = pltpu.bitcast(hi_f32, U32) & jnp.uint32(0xFFFF0000)
    lw = pltpu.bitcast(lo_f32, U32) >> 16
    return hw | lw


def _unpack_pair(w):
    hi = pltpu.bitcast(w & jnp.uint32(0xFFFF0000), F32)
    lo = pltpu.bitcast(w << 16, F32)
    return hi, lo


def _ada_kernel(c_ref, w_ref, b_ref, o_ref):
    cs = _silu(c_ref[...]).astype(BF16)
    o_ref[...] = jnp.dot(cs, w_ref[...].astype(BF16), preferred_element_type=F32) + b_ref[...]


def _ada(c_pad, w, b):
    rows, d = c_pad.shape
    n = w.shape[1]
    tn = 1024
    return pl.pallas_call(
        _ada_kernel,
        grid=(n // tn,),
        in_specs=[
            pl.BlockSpec((rows, d), lambda j: (0, 0)),
            pl.BlockSpec((d, tn), lambda j: (0, j)),
            pl.BlockSpec((1, tn), lambda j: (0, j)),
        ],
        out_specs=pl.BlockSpec((rows, tn), lambda j: (0, j)),
        out_shape=jax.ShapeDtypeStruct((rows, n), F32),
        compiler_params=_cparams(("arbitrary",)),
        name="ada",
    )(c_pad, w, b.reshape(1, n))


def _norm_mod(x, g, sc, sh):
    ms = jnp.mean(x * x, axis=-1, keepdims=True)
    return (x * lax.rsqrt(ms + EPS) * g) * (1.0 + sc) + sh


def _inproj_kernel(x_ref, g_ref, sc_ref, sh_ref, w_ref, o_ref, h_ref, *, chunk):
    @pl.when(pl.program_id(1) == 0)
    def _():
        g = g_ref[...]
        sc = sc_ref[0]
        sh = sh_ref[0]

        def body(r, carry):
            rows = pl.ds(pl.multiple_of(r * chunk, chunk), chunk)
            h_ref[rows, :] = _norm_mod(x_ref[rows, :], g, sc, sh).astype(BF16)
            return carry

        lax.fori_loop(0, x_ref.shape[0] // chunk, body, 0)

    o_ref[...] = jnp.dot(h_ref[...], w_ref[...], preferred_element_type=F32).astype(BF16)


def _inproj(x2, g, sc, sh, w_bf, seq):
    t, d = x2.shape
    n = w_bf.shape[1]
    tm, tn = 1024, 1024
    per_batch = seq // tm
    return pl.pallas_call(
        functools.partial(_inproj_kernel, chunk=256),
        grid=(t // tm, n // tn),
        in_specs=[
            pl.BlockSpec((tm, d), lambda i, j: (i, 0)),
            pl.BlockSpec((1, d), lambda i, j: (0, 0)),
            pl.BlockSpec((1, 1, d), lambda i, j: (i // per_batch, 0, 0)),
            pl.BlockSpec((1, 1, d), lambda i, j: (i // per_batch, 0, 0)),
            pl.BlockSpec((d, tn), lambda i, j: (0, j)),
        ],
        out_specs=pl.BlockSpec((tm, tn), lambda i, j: (i, j)),
        out_shape=jax.ShapeDtypeStruct((t, n), BF16),
        scratch_shapes=[pltpu.VMEM((tm, d), BF16)],
        compiler_params=_cparams(("arbitrary", "arbitrary")),
        name="inproj",
    )(x2, g.reshape(1, d), sc, sh, w_bf)


def _sb_kernel(q_ref, k_ref, v_ref, o_ref):
    seq = q_ref.shape[0]
    blk = LANES
    scale = HEAD_DIM ** -0.5
    row = lax.broadcasted_iota(I32, (blk, blk), 0)
    col = lax.broadcasted_iota(I32, (blk, blk), 1)
    strict = col < row
    tr = lax.broadcasted_iota(I32, (2 * blk, 2 * blk), 0) % blk
    tc = lax.broadcasted_iota(I32, (2 * blk, 2 * blk), 1)
    tri = jnp.where((tc >= blk) | (tr > tc), 1.0, 0.0).astype(BF16)

    def tile(qb, j, carry, acc, diagonal):
        rows = pl.ds(pl.multiple_of(j * blk, blk), blk)
        kb = k_ref[rows, :]
        vb = v_ref[rows, :]
        z = lax.dot_general(qb, kb, (((1,), (1,)), ((), ())), preferred_element_type=F32) * scale
        sp = jnp.maximum(z, 0.0) + jnp.log(1.0 + jnp.exp(-jnp.abs(z)))
        log_surv = -sp
        log_beta = z - sp
        if diagonal:
            log_surv = jnp.where(strict, log_surv, 0.0)
        hi = log_surv.astype(BF16)
        lo = (log_surv - hi.astype(F32)).astype(BF16)
        r = jnp.dot(jnp.concatenate([hi, lo], axis=1), tri, preferred_element_type=F32)
        a = jnp.exp(log_beta + r[:, :blk] + carry)
        if diagonal:
            a = jnp.where(strict, a, 0.0)
        acc = acc + jnp.dot(a.astype(BF16), vb, preferred_element_type=F32)
        return carry + r[:, blk:], acc

    def q_block(i, c):
        rows = pl.ds(pl.multiple_of(i * blk, blk), blk)
        qb = q_ref[rows, :]
        zeros = jnp.zeros((blk, blk), F32)
        carry, acc = tile(qb, i, zeros, zeros, True)

        def cond(st):
            j, cr, _ = st
            return jnp.logical_and(j >= 0, jnp.max(cr) > SB_SKIP_BOUND)

        def body(st):
            j, cr, ac = st
            cr, ac = tile(qb, j, cr, ac, False)
            return j - 1, cr, ac

        _, _, acc = lax.while_loop(cond, body, (i - 1, carry, acc))
        o_ref[rows, :] = acc.astype(BF16)
        return c

    lax.fori_loop(0, seq // blk, q_block, 0)


def _sb_attention(u3, n_heads):
    b, s, _ = u3.shape
    spec = lambda off: pl.BlockSpec((None, s, HEAD_DIM), lambda bi, h: (bi, 0, off + h))
    return pl.pallas_call(
        _sb_kernel,
        grid=(b, n_heads),
        in_specs=[spec(0), spec(n_heads), spec(2 * n_heads)],
        out_specs=pl.BlockSpec((None, s, HEAD_DIM), lambda bi, h: (bi, 0, h)),
        out_shape=jax.ShapeDtypeStruct((b, s, n_heads * HEAD_DIM), BF16),
        compiler_params=_cparams(("arbitrary", "arbitrary")),
        name="sb_attn",
    )(u3, u3, u3)


def _ret_kernel(q_ref, k_ref, v_ref, g_ref, cos_ref, sin_ref, gn_ref, o_ref):
    seq = q_ref.shape[0]
    c = LANES
    head = pl.program_id(1)
    hv = jnp.full((c, c), head, I32).astype(F32)
    log_g = jnp.log(1.0 - jnp.exp2(-5.0 - hv))
    ri = lax.broadcasted_iota(I32, (c, c), 0).astype(F32)
    ci = lax.broadcasted_iota(I32, (c, c), 1).astype(F32)
    diff = ri - ci
    lower = diff >= 0.0
    dmat = jnp.where(lower, jnp.exp(jnp.where(lower, diff, 0.0) * log_g), 0.0)
    k_decay = jnp.exp((c - 1.0 - ri) * log_g)
    q_decay = jnp.exp((ri + 1.0) * log_g)
    chunk_decay = jnp.exp(c * log_g)
    gn = gn_ref[...]
    scale = HEAD_DIM ** -0.5

    def rope(t, cs, sn):
        return t * cs + pltpu.roll(t, HEAD_DIM // 2, 1) * sn

    def body(n, state):
        rows = pl.ds(pl.multiple_of(n * c, c), c)
        cs = cos_ref[rows, :]
        sn = sin_ref[rows, :]
        qc = rope(q_ref[rows, :].astype(F32), cs, sn)
        kc = rope(k_ref[rows, :].astype(F32), cs, sn) * scale
        vc = v_ref[rows, :]
        scores = lax.dot_general(qc.astype(BF16), kc.astype(BF16), (((1,), (1,)), ((), ())),
                                 preferred_element_type=F32) * dmat
        intra = jnp.dot(scores.astype(BF16), vc, preferred_element_type=F32)
        cross = jnp.dot((qc * q_decay).astype(BF16), state.astype(BF16), preferred_element_type=F32)
        out = intra + cross
        kd_t = jnp.transpose(kc * k_decay).astype(BF16)
        state = state * chunk_decay + jnp.dot(kd_t, vc, preferred_element_type=F32)
        ms = jnp.mean(out * out, axis=-1, keepdims=True)
        y = out * lax.rsqrt(ms + EPS) * gn * _silu(g_ref[rows, :].astype(F32))
        o_ref[rows, :] = y.astype(BF16)
        return state

    lax.fori_loop(0, seq // c, body, jnp.zeros((c, c), F32))


def _retention(u3, cos2, sin2, gn, n_heads, col0):
    b, s, _ = u3.shape
    spec = lambda off: pl.BlockSpec((None, s, HEAD_DIM), lambda bi, h: (bi, 0, col0 + off + h))
    return pl.pallas_call(
        _ret_kernel,
        grid=(b, n_heads),
        in_specs=[spec(0), spec(n_heads), spec(2 * n_heads), spec(3 * n_heads),
                  pl.BlockSpec((s, HEAD_DIM), lambda bi, h: (0, 0)),
                  pl.BlockSpec((s, HEAD_DIM), lambda bi, h: (0, 0)),
                  pl.BlockSpec((1, HEAD_DIM), lambda bi, h: (0, h))],
        out_specs=pl.BlockSpec((None, s, HEAD_DIM), lambda bi, h: (bi, 0, h)),
        out_shape=jax.ShapeDtypeStruct((b, s, n_heads * HEAD_DIM), BF16),
        compiler_params=_cparams(("arbitrary", "arbitrary")),
        name="retention",
    )(u3, u3, u3, u3, cos2, sin2, gn.reshape(1, -1))


def _outproj_kernel(sb_ref, rt_ref, wa_ref, wb_ref, x_ref, g1_ref, n2_ref, sc_ref, sh_ref, wr_ref,
                    x1_ref, h2_ref, hp_ref, lt_ref, *, chunk):
    d = x_ref.shape[1]
    half = d // 2
    g1 = g1_ref[0]
    n2 = n2_ref[...]
    sc = sc_ref[0]
    sh = sh_ref[0]
    for r in range(x_ref.shape[0] // chunk):
        rows = pl.ds(r * chunk, chunk)
        o = jnp.dot(sb_ref[rows, :], wa_ref[...], preferred_element_type=F32)
        o = o + jnp.dot(rt_ref[rows, :], wb_ref[...], preferred_element_type=F32)
        x1 = x_ref[rows, :] + g1 * o
        x1_ref[rows, :] = x1
        h2 = _norm_mod(x1, n2, sc, sh).astype(BF16)
        h2_ref[rows, :] = h2
        lt_ref[:, rows] = lax.dot_general(wr_ref[...], h2, (((1,), (1,)), ((), ())),
                                          preferred_element_type=F32)
        words = _pack_pair(h2[:, :half].astype(F32), h2[:, half:].astype(F32))
        for j in range(ROW_WORDS):
            hp_ref[pl.ds(r * chunk * ROW_WORDS + j, chunk, stride=ROW_WORDS), :] = (
                words[:, j * LANES:(j + 1) * LANES])


def _outproj(sb2, rt2, wa, wb, x2, g1, n2, sc2, sh2, wr_t, seq):
    t, d = x2.shape
    tm = 512
    per_batch = seq // tm
    mod = pl.BlockSpec((1, 1, d), lambda i: (i // per_batch, 0, 0))
    return pl.pallas_call(
        functools.partial(_outproj_kernel, chunk=256),
        grid=(t // tm,),
        in_specs=[
            pl.BlockSpec((tm, sb2.shape[1]), lambda i: (i, 0)),
            pl.BlockSpec((tm, rt2.shape[1]), lambda i: (i, 0)),
            pl.BlockSpec(wa.shape, lambda i: (0, 0)),
            pl.BlockSpec(wb.shape, lambda i: (0, 0)),
            pl.BlockSpec((tm, d), lambda i: (i, 0)),
            mod,
            pl.BlockSpec((1, d), lambda i: (0, 0)),
            mod, mod,
            pl.BlockSpec(wr_t.shape, lambda i: (0, 0)),
        ],
        out_specs=[
            pl.BlockSpec((tm, d), lambda i: (i, 0)),
            pl.BlockSpec((tm, d), lambda i: (i, 0)),
            pl.BlockSpec((tm * ROW_WORDS, LANES), lambda i: (i, 0)),
            pl.BlockSpec((N_EXPERTS, tm), lambda i: (0, i)),
        ],
        out_shape=[
            jax.ShapeDtypeStruct((t, d), F32),
            jax.ShapeDtypeStruct((t, d), BF16),
            jax.ShapeDtypeStruct((t * ROW_WORDS, LANES), U32),
            jax.ShapeDtypeStruct((N_EXPERTS, t), F32),
        ],
        compiler_params=_cparams(("arbitrary",)),
        name="outproj",
    )(sb2, rt2, wa, wb, x2, g1, n2.reshape(1, d), sc2, sh2, wr_t)


def _beats(row, allv, row_idx, idx):
    return (row > allv) | ((row == allv) & (row_idx < idx))


def _route_kernel(lt_ref, bias_ref, w_ref, rank_ref, cnt_ref, carry_ref):
    tr = lt_ref.shape[1]
    gsz = N_EXPERTS // N_GROUPS
    neg_inf = jnp.float32(-jnp.inf)

    @pl.when(pl.program_id(0) == 0)
    def _():
        carry_ref[...] = jnp.zeros_like(carry_ref)

    scores = jax.nn.sigmoid(lt_ref[...])
    choice = scores + bias_ref[...]
    sub = lax.broadcasted_iota(I32, (gsz, tr), 0)
    gs_rows = []
    for g in range(N_GROUPS):
        cg = choice[g * gsz:(g + 1) * gsz, :]
        m1 = jnp.max(cg, axis=0, keepdims=True)
        first = jnp.min(jnp.where(cg == m1, sub, gsz), axis=0, keepdims=True)
        m2 = jnp.max(jnp.where(sub == first, neg_inf, cg), axis=0, keepdims=True)
        gs_rows.append(m1 + m2)
    gs = jnp.concatenate(gs_rows, axis=0)
    gi = lax.broadcasted_iota(I32, (N_GROUPS, tr), 0)
    grank = jnp.zeros((N_GROUPS, tr), I32)
    for g in range(N_GROUPS):
        grank = grank + _beats(gs[g:g + 1, :], gs, g, gi).astype(I32)
    gmask = grank < TOPK_GROUP
    emask = jnp.concatenate(
        [jnp.broadcast_to(gmask[g:g + 1, :], (gsz, tr)) for g in range(N_GROUPS)], axis=0)
    masked = jnp.where(emask, choice, neg_inf)
    ei = lax.broadcasted_iota(I32, (N_EXPERTS, tr), 0)
    erank = jnp.zeros((N_EXPERTS, tr), I32)
    for e in range(N_EXPERTS):
        erank = erank + _beats(masked[e:e + 1, :], masked, e, ei).astype(I32)
    sel = erank < TOP_K
    ssel = jnp.where(sel, scores, 0.0)
    denom = jnp.sum(ssel, axis=0, keepdims=True)
    w_ref[...] = ssel / denom * ROUTED_SCALE

    self = sel.astype(F32)
    upper = (lax.broadcasted_iota(I32, (tr, tr), 0) < lax.broadcasted_iota(I32, (tr, tr), 1))
    prefix = jnp.dot(self.astype(BF16), upper.astype(BF16), preferred_element_type=F32)
    carry = carry_ref[...]
    rank = prefix + carry[:, 0:1]
    rank_ref[...] = jnp.where(sel, rank, -1.0).astype(I32)
    carry = carry + jnp.sum(self, axis=1, keepdims=True)
    carry_ref[...] = carry
    cnt_ref[...] = carry


def _route(lt, bias):
    e, t = lt.shape
    tr = 512
    return pl.pallas_call(
        _route_kernel,
        grid=(t // tr,),
        in_specs=[pl.BlockSpec((e, tr), lambda i: (0, i)),
                  pl.BlockSpec((e, 1), lambda i: (0, 0))],
        out_specs=[pl.BlockSpec((e, tr), lambda i: (0, i)),
                   pl.BlockSpec((e, tr), lambda i: (0, i)),
                   pl.BlockSpec((e, LANES), lambda i: (0, 0))],
        out_shape=[jax.ShapeDtypeStruct((e, t), F32),
                   jax.ShapeDtypeStruct((e, t), I32),
                   jax.ShapeDtypeStruct((e, LANES), F32)],
        scratch_shapes=[pltpu.VMEM((e, LANES), F32)],
        compiler_params=_cparams(("arbitrary",)),
        name="route",
    )(lt, bias.reshape(e, 1))


def _plan_kernel(w_ref, rank_ref, cnt_ref, pos_ref, wc_ref, bexp_ref, seg_ref):
    tr = w_ref.shape[1]
    nb_lanes = bexp_ref.shape[1]
    cnt = cnt_ref[...]
    nblk = jnp.floor((cnt + (MOE_BLOCK - 1.0)) * (1.0 / MOE_BLOCK))
    lower = (lax.broadcasted_iota(I32, (N_EXPERTS, N_EXPERTS), 1)
             < lax.broadcasted_iota(I32, (N_EXPERTS, N_EXPERTS), 0)).astype(BF16)
    bstart = jnp.dot(lower, nblk.astype(BF16), preferred_element_type=F32)
    bend = bstart + nblk

    @pl.when(pl.program_id(0) == 0)
    def _():
        bid = lax.broadcasted_iota(I32, (N_EXPERTS, nb_lanes), 1).astype(F32)
        owner = jnp.sum((bend[:, 0:1] <= bid).astype(I32), axis=0, keepdims=True)
        bexp_ref[...] = jnp.minimum(owner, N_EXPERTS - 1)
        seg_ref[...] = jnp.concatenate(
            [bstart[:, 0:1] * MOE_BLOCK, cnt[:, 0:1], nblk[:, 0:1] * MOE_BLOCK, bend[:, 0:1]],
            axis=1).astype(I32)

    rank = rank_ref[...]
    sel = rank >= 0
    pos = bstart[:, 0:1] * MOE_BLOCK + rank.astype(F32)
    slot = jnp.dot(lower, sel.astype(BF16), preferred_element_type=F32)
    w = w_ref[...]
    pos_rows, w_rows = [], []
    for k in range(TOP_K):
        m = sel & (slot == float(k))
        pos_rows.append(jnp.sum(jnp.where(m, pos, 0.0), axis=0, keepdims=True))
        w_rows.append(jnp.sum(jnp.where(m, w, 0.0), axis=0, keepdims=True))
    pos_ref[...] = jnp.concatenate(pos_rows, axis=0).astype(I32)
    wc_ref[...] = jnp.concatenate(w_rows, axis=0)


def _plan(w, rank, cnt, nb_lanes):
    e, t = w.shape
    tr = 1024
    return pl.pallas_call(
        _plan_kernel,
        grid=(t // tr,),
        in_specs=[pl.BlockSpec((e, tr), lambda i: (0, i)),
                  pl.BlockSpec((e, tr), lambda i: (0, i)),
                  pl.BlockSpec((e, LANES), lambda i: (0, 0))],
        out_specs=[pl.BlockSpec((TOP_K, tr), lambda i: (0, i)),
                   pl.BlockSpec((TOP_K, tr), lambda i: (0, i)),
                   pl.BlockSpec((1, nb_lanes), lambda i: (0, 0)),
                   pl.BlockSpec((e, 4), lambda i: (0, 0))],
        out_shape=[jax.ShapeDtypeStruct((TOP_K, t), I32),
                   jax.ShapeDtypeStruct((TOP_K, t), F32),
                   jax.ShapeDtypeStruct((1, nb_lanes), I32),
                   jax.ShapeDtypeStruct((e, 4), I32)],
        compiler_params=_cparams(("arbitrary",)),
        name="plan",
    )(w, rank, cnt)


def _row_copy(src, src_row, dst, dst_row, sem):
    return pltpu.make_async_copy(src.at[pl.ds(src_row * ROW_WORDS, ROW_WORDS), :],
                                 dst.at[pl.ds(dst_row * ROW_WORDS, ROW_WORDS), :], sem)


def _dispatch_kernel(pos_ref, seg_ref, hp_ref, xs_ref, zero_ref, sem, zsem):
    tq = pos_ref.shape[1]
    base = pl.program_id(0) * tq
    blk_words = MOE_BLOCK * ROW_WORDS
    n_blocks = xs_ref.shape[0] // blk_words

    @pl.when(pl.program_id(0) == 0)
    def _():
        zero_ref[...] = jnp.zeros_like(zero_ref)
        zrow = zero_ref.at[pl.ds(0, ROW_WORDS), :]

        def per_expert(e, n):
            start = seg_ref[e, 0] + seg_ref[e, 1]
            npad = seg_ref[e, 2] - seg_ref[e, 1]

            def fill(r, c):
                pltpu.make_async_copy(
                    zrow, xs_ref.at[pl.ds((start + r) * ROW_WORDS, ROW_WORDS), :], zsem).start()
                return c

            lax.fori_loop(0, npad, fill, 0)
            return n + npad

        n_pad_rows = lax.fori_loop(0, N_EXPERTS, per_expert, 0)

        def drain_row(r, c):
            pltpu.make_async_copy(zrow, xs_ref.at[pl.ds(0, ROW_WORDS), :], zsem).wait()
            return c

        lax.fori_loop(0, n_pad_rows, drain_row, 0)
        n_used = seg_ref[N_EXPERTS - 1, 3]

        def fill_block(b, c):
            pltpu.make_async_copy(
                zero_ref, xs_ref.at[pl.ds(b * blk_words, blk_words), :], zsem).start()
            return c

        lax.fori_loop(n_used, n_blocks, fill_block, 0)

        def drain_block(b, c):
            pltpu.make_async_copy(zero_ref, xs_ref.at[pl.ds(0, blk_words), :], zsem).wait()
            return c

        lax.fori_loop(n_used, n_blocks, drain_block, 0)

    def issue(t, c):
        for k in range(TOP_K):
            _row_copy(hp_ref, base + t, xs_ref, pos_ref[k, t], sem).start()
        return c

    lax.fori_loop(0, tq, issue, 0)

    def drain(t, c):
        for k in range(TOP_K):
            _row_copy(hp_ref, 0, xs_ref, 0, sem).wait()
        return c

    lax.fori_loop(0, tq, drain, 0)


def _dispatch(pos_c, seg, hp, n_rows):
    t = pos_c.shape[1]
    tq = 1024
    return pl.pallas_call(
        _dispatch_kernel,
        grid=(t // tq,),
        in_specs=[pl.BlockSpec((TOP_K, tq), lambda i: (0, i), memory_space=pltpu.SMEM),
                  pl.BlockSpec(seg.shape, lambda i: (0, 0), memory_space=pltpu.SMEM),
                  pl.BlockSpec(memory_space=pl.ANY)],
        out_specs=pl.BlockSpec(memory_space=pl.ANY),
        out_shape=jax.ShapeDtypeStruct((n_rows * ROW_WORDS, LANES), U32),
        scratch_shapes=[pltpu.VMEM((MOE_BLOCK * ROW_WORDS, LANES), U32),
                        pltpu.SemaphoreType.DMA(()), pltpu.SemaphoreType.DMA(())],
        compiler_params=_cparams(("arbitrary",)),
        name="dispatch",
    )(pos_c, seg, hp)


def _expert_kernel(bexp_ref, nused_ref, xs_ref, wg_ref, wu_ref, wd_ref, ys_ref, wgb, wub, wdb):
    b = pl.program_id(0)
    rows = MOE_BLOCK
    prev = bexp_ref[jnp.maximum(b - 1, 0)]
    fresh = jnp.logical_or(b == 0, bexp_ref[b] != prev)
    used = b < nused_ref[0]

    @pl.when(jnp.logical_and(used, fresh))
    def _():
        wgb[...] = wg_ref[...].astype(BF16)
        wub[...] = wu_ref[...].astype(BF16)
        wdb[...] = wd_ref[...].astype(BF16)

    @pl.when(used)
    def _():
        his, los = [], []
        for j in range(ROW_WORDS):
            hi, lo = _unpack_pair(xs_ref[pl.ds(j, rows, stride=ROW_WORDS), :])
            his.append(hi.astype(BF16))
            los.append(lo.astype(BF16))
        xb = jnp.concatenate(his + los, axis=1)
        a = jnp.dot(xb, wgb[...], preferred_element_type=F32)
        u = jnp.dot(xb, wub[...], preferred_element_type=F32)
        hid = (_silu(a) * u).astype(BF16)
        out = jnp.dot(hid, wdb[...], preferred_element_type=F32).astype(BF16).astype(F32)
        half = out.shape[1] // 2
        words = _pack_pair(out[:, :half], out[:, half:])
        for j in range(ROW_WORDS):
            ys_ref[pl.ds(j, rows, stride=ROW_WORDS), :] = words[:, j * LANES:(j + 1) * LANES]

    @pl.when(jnp.logical_not(used))
    def _():
        ys_ref[...] = jnp.zeros_like(ys_ref)


def _experts(bexp, nused, xs, w_gate, w_up, w_down):
    n_blocks = xs.shape[0] // (MOE_BLOCK * ROW_WORDS)
    _, d, f = w_gate.shape

    def xmap(b, be, nu):
        return (jnp.minimum(b, jnp.maximum(nu[0] - 1, 0)), 0)

    grid_spec = pltpu.PrefetchScalarGridSpec(
        num_scalar_prefetch=2,
        grid=(n_blocks,),
        in_specs=[
            pl.BlockSpec((MOE_BLOCK * ROW_WORDS, LANES), xmap),
            pl.BlockSpec((None, d, f), lambda b, be, nu: (be[b], 0, 0)),
            pl.BlockSpec((None, d, f), lambda b, be, nu: (be[b], 0, 0)),
            pl.BlockSpec((None, f, d), lambda b, be, nu: (be[b], 0, 0)),
        ],
        out_specs=pl.BlockSpec((MOE_BLOCK * ROW_WORDS, LANES), lambda b, be, nu: (b, 0)),
        scratch_shapes=[pltpu.VMEM((d, f), BF16), pltpu.VMEM((d, f), BF16), pltpu.VMEM((f, d), BF16)],
    )
    return pl.pallas_call(
        _expert_kernel,
        grid_spec=grid_spec,
        out_shape=jax.ShapeDtypeStruct(xs.shape, U32),
        compiler_params=_cparams(("arbitrary",)),
        name="experts",
    )(bexp, nused, xs, w_gate, w_up, w_down)


def _combine_kernel(pos_ref, ys_ref, wc_ref, h2_ref, x1_ref, sg_ref, su_ref, sd_ref,
                    g2_ref, nf_ref, scf_ref, shf_ref, o_ref, buf, sem):
    tq = wc_ref.shape[1]
    d = x1_ref.shape[1]
    half = d // 2

    def issue(t, c):
        for k in range(TOP_K):
            pltpu.make_async_copy(
                ys_ref.at[pl.ds(pos_ref[k, t] * ROW_WORDS, ROW_WORDS), :],
                buf.at[k, pl.ds(t * ROW_WORDS, ROW_WORDS), :], sem).start()
        return c

    lax.fori_loop(0, tq, issue, 0)

    h2 = h2_ref[...]
    a = jnp.dot(h2, sg_ref[...], preferred_element_type=F32)
    u = jnp.dot(h2, su_ref[...], preferred_element_type=F32)
    shared = jnp.dot((_silu(a) * u).astype(BF16), sd_ref[...], preferred_element_type=F32)

    def drain(t, c):
        for k in range(TOP_K):
            pltpu.make_async_copy(ys_ref.at[pl.ds(0, ROW_WORDS), :],
                                  buf.at[0, pl.ds(0, ROW_WORDS), :], sem).wait()
        return c

    lax.fori_loop(0, tq, drain, 0)

    wt = jnp.transpose(wc_ref[...])
    his, los = [], []
    for j in range(ROW_WORDS):
        acc_hi = jnp.zeros((tq, LANES), F32)
        acc_lo = jnp.zeros((tq, LANES), F32)
        for k in range(TOP_K):
            hi, lo = _unpack_pair(buf.at[k][pl.ds(j, tq, stride=ROW_WORDS), :])
            wk = wt[:, k:k + 1]
            acc_hi = acc_hi + wk * hi
            acc_lo = acc_lo + wk * lo
        his.append(acc_hi)
        los.append(acc_lo)
    y = jnp.concatenate(his + los, axis=1) + shared
    x2 = x1_ref[...] + g2_ref[0] * y
    o_ref[...] = _norm_mod(x2, nf_ref[...], scf_ref[0], shf_ref[0])


def _combine(pos_c, ys, w_c, h2, x1, sg, su, sd, g2, nf, scf, shf, seq):
    t, d = x1.shape
    tq = 128
    per_batch = seq // tq
    mod = pl.BlockSpec((1, 1, d), lambda i: (i // per_batch, 0, 0))
    full = lambda a: pl.BlockSpec(a.shape, lambda i: (0, 0))
    return pl.pallas_call(
        _combine_kernel,
        grid=(t // tq,),
        in_specs=[
            pl.BlockSpec((TOP_K, tq), lambda i: (0, i), memory_space=pltpu.SMEM),
            pl.BlockSpec(memory_space=pl.ANY),
            pl.BlockSpec((TOP_K, tq), lambda i: (0, i)),
            pl.BlockSpec((tq, d), lambda i: (i, 0)),
            pl.BlockSpec((tq, d), lambda i: (i, 0)),
            full(sg), full(su), full(sd),
            mod,
            pl.BlockSpec((1, d), lambda i: (0, 0)),
            mod, mod,
        ],
        out_specs=pl.BlockSpec((tq, d), lambda i: (i, 0)),
        out_shape=jax.ShapeDtypeStruct((t, d), F32),
        scratch_shapes=[pltpu.VMEM((TOP_K, tq * ROW_WORDS, LANES), U32),
                        pltpu.SemaphoreType.DMA(())],
        compiler_params=_cparams(("arbitrary",)),
        name="combine",
    )(pos_c, ys, w_c, h2, x1, sg, su, sd, g2, nf.reshape(1, d), scf, shf)


def _rope_tables(seq):
    pos = jnp.arange(seq, dtype=F32)
    inv_freq = ROPE_BASE ** (-jnp.arange(0, HEAD_DIM, 2, dtype=F32) / HEAD_DIM)
    ang = pos[:, None] * inv_freq[None, :]
    cos, sin = jnp.cos(ang), jnp.sin(ang)
    return jnp.concatenate([cos, cos], axis=1), jnp.concatenate([-sin, sin], axis=1)


def kernel(x, c, w_ada, b_ada, norm1_g, w_in, ret_gn_g, w_out, norm2_g, w_router, router_bias,
           w_gate, w_up, w_down, ws_gate, ws_up, ws_down, w_ada_final, b_ada_final, norm_f_g):
    bsz, seq, d = x.shape
    t = bsz * seq
    depth = w_ada.shape[0]
    n_heads = d // (2 * HEAD_DIM)
    sb_w = n_heads * HEAD_DIM
    cos2, sin2 = _rope_tables(seq)
    c_pad = jnp.pad(c, ((0, 8 - bsz % 8), (0, 0)))
    n_rows = t * TOP_K + N_EXPERTS * MOE_BLOCK
    n_blocks = n_rows // MOE_BLOCK
    nb_lanes = -(-n_blocks // LANES) * LANES
    as_mod = lambda v: v.reshape(bsz, 1, d)

    x2 = x.reshape(t, d)
    for l in range(depth):
        mod = _ada(c_pad, w_ada[l], b_ada[l])[:bsz]
        sh1, sc1, g1, sh2, sc2, g2 = [as_mod(m) for m in jnp.split(mod, 6, axis=-1)]
        u = _inproj(x2, norm1_g[l], sc1, sh1, w_in[l].astype(BF16), seq)
        u3 = u.reshape(bsz, seq, -1)
        sb = _sb_attention(u3, n_heads)
        rt = _retention(u3, cos2, sin2, ret_gn_g[l], n_heads, 3 * n_heads)
        wo = w_out[l].astype(BF16)
        x1, h2, hp, lt = _outproj(sb.reshape(t, sb_w), rt.reshape(t, -1), wo[:sb_w], wo[sb_w:], x2,
                                  g1, norm2_g[l], sc2, sh2, jnp.transpose(w_router[l]).astype(BF16), seq)
        w_e, rank_e, cnt = _route(lt, router_bias[l])
        pos_c, w_c, bexp, seg = _plan(w_e, rank_e, cnt, nb_lanes)
        xs = _dispatch(pos_c, seg, hp, n_rows)
        ys = _experts(bexp.reshape(-1), seg[N_EXPERTS - 1, 3].reshape(1), xs, w_gate[l], w_up[l], w_down[l])
        if l + 1 < depth:
            raise NotImplementedError("only the final layer fuses the output norm")
        modf = _ada(c_pad, w_ada_final, b_ada_final)[:bsz]
        shf, scf = [as_mod(m) for m in jnp.split(modf, 2, axis=-1)]
        x2 = _combine(pos_c, ys, w_c, h2, x1, ws_gate[l].astype(BF16), ws_up[l].astype(BF16),
                      ws_down[l].astype(BF16), g2, norm_f_g, scf, shf, seq)
    return x2.reshape(bsz, seq, d)
```

```python
import functools

import jax
import jax.numpy as jnp
from jax import lax
from jax.experimental import pallas as pl
from jax.experimental.pallas import tpu as pltpu

F32 = jnp.float32
BF16 = jnp.bfloat16
I32 = jnp.int32
U32 = jnp.uint32

HEAD_DIM = 128
N_EXPERTS = 64
TOP_K = 8
N_GROUPS = 8
TOPK_GROUP = 4
ROUTED_SCALE = 2.5
EPS = 1e-6
ROPE_BASE = 10000.0

LANES = 128
ROW_WORDS = 8
MOE_BLOCK = 256
V7X_VMEM_LIMIT = 56 * 1024 * 1024
SB_SKIP_BOUND = -115.0


def _cparams(sem, vmem=V7X_VMEM_LIMIT):
    return pltpu.CompilerParams(dimension_semantics=sem, vmem_limit_bytes=vmem)


def _silu(v):
    return v * jax.nn.sigmoid(v)


def _pack_pair(hi_f32, lo_f32):
    hw = pltpu.bitcast(hi_f32, U32) & jnp.uint32(0xFFFF0000)
    lw = pltpu.bitcast(lo_f32, U32) >> 16
    return hw | lw


def _unpack_pair(w):
    hi = pltpu.bitcast(w & jnp.uint32(0xFFFF0000), F32)
    lo = pltpu.bitcast(w << 16, F32)
    return hi, lo


def _ada_kernel(c_ref, w_ref, b_ref, o_ref):
    cs = _silu(c_ref[...]).astype(BF16)
    o_ref[...] = jnp.dot(cs, w_ref[...].astype(BF16), preferred_element_type=F32) + b_ref[...]


def _ada(c_pad, w, b):
    rows, d = c_pad.shape
    n = w.shape[1]
    tn = 1024
    return pl.pallas_call(
        _ada_kernel,
        grid=(n // tn,),
        in_specs=[
            pl.BlockSpec((rows, d), lambda j: (0, 0)),
            pl.BlockSpec((d, tn), lambda j: (0, j)),
            pl.BlockSpec((1, tn), lambda j: (0, j)),
        ],
        out_specs=pl.BlockSpec((rows, tn), lambda j: (0, j)),
        out_shape=jax.ShapeDtypeStruct((rows, n), F32),
        compiler_params=_cparams(("arbitrary",)),
        name="ada",
    )(c_pad, w, b.reshape(1, n))


def _norm_mod(x, g, sc, sh):
    ms = jnp.mean(x * x, axis=-1, keepdims=True)
    return (x * lax.rsqrt(ms + EPS) * g) * (1.0 + sc) + sh


def _inproj_kernel(x_ref, g_ref, sc_ref, sh_ref, w_ref, o_ref, h_ref, *, chunk):
    @pl.when(pl.program_id(1) == 0)
    def _():
        g = g_ref[...]
        sc = sc_ref[0]
        sh = sh_ref[0]

        def body(r, carry):
            rows = pl.ds(pl.multiple_of(r * chunk, chunk), chunk)
            h_ref[rows, :] = _norm_mod(x_ref[rows, :], g, sc, sh).astype(BF16)
            return carry

        lax.fori_loop(0, x_ref.shape[0] // chunk, body, 0)

    o_ref[...] = jnp.dot(h_ref[...], w_ref[...], preferred_element_type=F32).astype(BF16)


def _inproj(x2, g, sc, sh, w_bf, seq):
    t, d = x2.shape
    n = w_bf.shape[1]
    tm, tn = 1024, 1024
    per_batch = seq // tm
    return pl.pallas_call(
        functools.partial(_inproj_kernel, chunk=256),
        grid=(t // tm, n // tn),
        in_specs=[
            pl.BlockSpec((tm, d), lambda i, j: (i, 0)),
            pl.BlockSpec((1, d), lambda i, j: (0, 0)),
            pl.BlockSpec((1, 1, d), lambda i, j: (i // per_batch, 0, 0)),
            pl.BlockSpec((1, 1, d), lambda i, j: (i // per_batch, 0, 0)),
            pl.BlockSpec((d, tn), lambda i, j: (0, j)),
        ],
        out_specs=pl.BlockSpec((tm, tn), lambda i, j: (i, j)),
        out_shape=jax.ShapeDtypeStruct((t, n), BF16),
        scratch_shapes=[pltpu.VMEM((tm, d), BF16)],
        compiler_params=_cparams(("arbitrary", "arbitrary")),
        name="inproj",
    )(x2, g.reshape(1, d), sc, sh, w_bf)


def _sb_kernel(q_ref, k_ref, v_ref, o_ref, *, group):
    seq = q_ref.shape[0]
    blk = LANES
    scale = HEAD_DIM ** -0.5
    row = lax.broadcasted_iota(I32, (blk, blk), 0)
    col = lax.broadcasted_iota(I32, (blk, blk), 1)
    strict = col < row
    tr = lax.broadcasted_iota(I32, (2 * blk, 2 * blk), 0) % blk
    tc = lax.broadcasted_iota(I32, (2 * blk, 2 * blk), 1)
    tri = jnp.where((tc >= blk) | (tr > tc), 1.0, 0.0).astype(BF16)

    def key_rows(j):
        return pl.ds(pl.multiple_of(j * blk, blk), blk)

    def logits(qb, kb):
        z = lax.dot_general(qb, kb, (((1,), (1,)), ((), ())), preferred_element_type=F32) * scale
        sp = jnp.maximum(z, 0.0) + jnp.log(1.0 + jnp.exp(-jnp.abs(z)))
        return -sp, z - sp

    def suffix_sums(log_surv):
        hi = log_surv.astype(BF16)
        lo = (log_surv - hi.astype(F32)).astype(BF16)
        r = jnp.dot(jnp.concatenate([hi, lo], axis=1), tri, preferred_element_type=F32)
        return r[:, :blk], r[:, blk:]

    def tile(qb, j, g, carry, acc):
        log_surv, log_beta = logits(qb, k_ref[key_rows(j), g * blk:(g + 1) * blk])
        after, total = suffix_sums(log_surv)
        a = jnp.exp(log_beta + after + carry)
        acc = acc + jnp.dot(a.astype(BF16), v_ref[key_rows(j), g * blk:(g + 1) * blk],
                            preferred_element_type=F32)
        return carry + total, acc

    def head_start(qb, i, g):
        cols = slice(g * blk, (g + 1) * blk)
        js = [i, jnp.maximum(i - 1, 0), jnp.maximum(i - 2, 0)]
        ivec = jnp.full((blk, blk), i, I32)
        masks = [strict, ivec >= 1, ivec >= 2]
        kcat = jnp.concatenate([k_ref[key_rows(j), cols] for j in js], axis=0)
        log_surv, log_beta = logits(qb, kcat)
        carry = jnp.zeros((blk, blk), F32)
        weights = []
        for m in range(3):
            blk_cols = slice(m * blk, (m + 1) * blk)
            after, total = suffix_sums(jnp.where(masks[m], log_surv[:, blk_cols], 0.0))
            a = jnp.exp(log_beta[:, blk_cols] + after + carry)
            weights.append(jnp.where(masks[m], a, 0.0).astype(BF16))
            carry = carry + total
        vcat = jnp.concatenate([v_ref[key_rows(j), cols] for j in js], axis=0)
        acc = jnp.dot(jnp.concatenate(weights, axis=1), vcat, preferred_element_type=F32)
        return carry, acc

    def q_block(i, c):
        rows = pl.ds(pl.multiple_of(i * blk, blk), blk)
        qbs = [q_ref[rows, g * blk:(g + 1) * blk] for g in range(group)]
        first = [head_start(qbs[g], i, g) for g in range(group)]
        carries = tuple(f[0] for f in first)
        accs = tuple(f[1] for f in first)

        def cond(st):
            j, crs, _ = st
            top = functools.reduce(jnp.maximum, crs)
            return jnp.logical_and(j >= 0, jnp.max(top) > SB_SKIP_BOUND)

        def body(st):
            j, crs, acs = st
            nxt = [tile(qbs[g], j, g, crs[g], acs[g]) for g in range(group)]
            return j - 1, tuple(n[0] for n in nxt), tuple(n[1] for n in nxt)

        _, _, accs = lax.while_loop(cond, body, (i - 3, carries, accs))
        for g in range(group):
            o_ref[rows, g * blk:(g + 1) * blk] = accs[g].astype(BF16)
        return c

    lax.fori_loop(0, seq // blk, q_block, 0)


SB_HEAD_GROUP = 4


def _sb_attention(u3, n_heads):
    b, s, _ = u3.shape
    gw = SB_HEAD_GROUP * HEAD_DIM
    n_groups = n_heads // SB_HEAD_GROUP
    spec = lambda off: pl.BlockSpec((None, s, gw), lambda bi, h: (bi, 0, off + h))
    return pl.pallas_call(
        functools.partial(_sb_kernel, group=SB_HEAD_GROUP),
        grid=(b, n_groups),
        in_specs=[spec(0), spec(n_groups), spec(2 * n_groups)],
        out_specs=pl.BlockSpec((None, s, gw), lambda bi, h: (bi, 0, h)),
        out_shape=jax.ShapeDtypeStruct((b, s, n_heads * HEAD_DIM), BF16),
        compiler_params=_cparams(("arbitrary", "arbitrary")),
        name="sb_attn",
    )(u3, u3, u3)


def _ret_kernel(q_ref, k_ref, v_ref, g_ref, cos_ref, sin_ref, gn_ref, o_ref):
    seq = q_ref.shape[0]
    c = LANES
    head = pl.program_id(1)
    hv = jnp.full((c, c), head, I32).astype(F32)
    log_g = jnp.log(1.0 - jnp.exp2(-5.0 - hv))
    ri = lax.broadcasted_iota(I32, (c, c), 0).astype(F32)
    ci = lax.broadcasted_iota(I32, (c, c), 1).astype(F32)
    diff = ri - ci
    lower = diff >= 0.0
    dmat = jnp.where(lower, jnp.exp(jnp.where(lower, diff, 0.0) * log_g), 0.0)
    k_decay = jnp.exp((c - 1.0 - ri) * log_g)
    q_decay = jnp.exp((ri + 1.0) * log_g)
    chunk_decay = jnp.exp(c * log_g)
    gn = gn_ref[...]
    scale = HEAD_DIM ** -0.5

    def rope(t, cs, sn):
        return t * cs + pltpu.roll(t, HEAD_DIM // 2, 1) * sn

    def body(n, state):
        rows = pl.ds(pl.multiple_of(n * c, c), c)
        cs = cos_ref[rows, :]
        sn = sin_ref[rows, :]
        qc = rope(q_ref[rows, :].astype(F32), cs, sn)
        kc = rope(k_ref[rows, :].astype(F32), cs, sn) * scale
        vc = v_ref[rows, :]
        scores = lax.dot_general(qc.astype(BF16), kc.astype(BF16), (((1,), (1,)), ((), ())),
                                 preferred_element_type=F32) * dmat
        intra = jnp.dot(scores.astype(BF16), vc, preferred_element_type=F32)
        cross = jnp.dot((qc * q_decay).astype(BF16), state.astype(BF16), preferred_element_type=F32)
        out = intra + cross
        kd_t = jnp.transpose(kc * k_decay).astype(BF16)
        state = state * chunk_decay + jnp.dot(kd_t, vc, preferred_element_type=F32)
        ms = jnp.mean(out * out, axis=-1, keepdims=True)
        y = out * lax.rsqrt(ms + EPS) * gn * _silu(g_ref[rows, :].astype(F32))
        o_ref[rows, :] = y.astype(BF16)
        return state

    lax.fori_loop(0, seq // c, body, jnp.zeros((c, c), F32), unroll=4)


def _retention(u3, cos2, sin2, gn, n_heads, col0):
    b, s, _ = u3.shape
    spec = lambda off: pl.BlockSpec((None, s, HEAD_DIM), lambda bi, h: (bi, 0, col0 + off + h))
    return pl.pallas_call(
        _ret_kernel,
        grid=(b, n_heads),
        in_specs=[spec(0), spec(n_heads), spec(2 * n_heads), spec(3 * n_heads),
                  pl.BlockSpec((s, HEAD_DIM), lambda bi, h: (0, 0)),
                  pl.BlockSpec((s, HEAD_DIM), lambda bi, h: (0, 0)),
                  pl.BlockSpec((1, HEAD_DIM), lambda bi, h: (0, h))],
        out_specs=pl.BlockSpec((None, s, HEAD_DIM), lambda bi, h: (bi, 0, h)),
        out_shape=jax.ShapeDtypeStruct((b, s, n_heads * HEAD_DIM), BF16),
        compiler_params=_cparams(("arbitrary", "arbitrary")),
        name="retention",
    )(u3, u3, u3, u3, cos2, sin2, gn.reshape(1, -1))


def _outproj_kernel(sb_ref, rt_ref, wa_ref, wb_ref, x_ref, g1_ref, n2_ref, sc_ref, sh_ref, wr_ref,
                    x1_ref, h2_ref, hp_ref, lt_ref, *, chunk):
    d = x_ref.shape[1]
    half = d // 2
    g1 = g1_ref[0]
    n2 = n2_ref[...]
    sc = sc_ref[0]
    sh = sh_ref[0]
    for r in range(x_ref.shape[0] // chunk):
        rows = pl.ds(r * chunk, chunk)
        o = jnp.dot(sb_ref[rows, :], wa_ref[...], preferred_element_type=F32)
        o = o + jnp.dot(rt_ref[rows, :], wb_ref[...], preferred_element_type=F32)
        x1 = x_ref[rows, :] + g1 * o
        x1_ref[rows, :] = x1
        h2 = _norm_mod(x1, n2, sc, sh).astype(BF16)
        h2_ref[rows, :] = h2
        lt_ref[:, rows] = lax.dot_general(wr_ref[...], h2, (((1,), (1,)), ((), ())),
                                          preferred_element_type=F32)
        words = _pack_pair(h2[:, :half].astype(F32), h2[:, half:].astype(F32))
        for j in range(ROW_WORDS):
            hp_ref[pl.ds(r * chunk * ROW_WORDS + j, chunk, stride=ROW_WORDS), :] = (
                words[:, j * LANES:(j + 1) * LANES])


def _outproj(sb2, rt2, wa, wb, x2, g1, n2, sc2, sh2, wr_t, seq):
    t, d = x2.shape
    tm = 512
    per_batch = seq // tm
    mod = pl.BlockSpec((1, 1, d), lambda i: (i // per_batch, 0, 0))
    return pl.pallas_call(
        functools.partial(_outproj_kernel, chunk=256),
        grid=(t // tm,),
        in_specs=[
            pl.BlockSpec((tm, sb2.shape[1]), lambda i: (i, 0)),
            pl.BlockSpec((tm, rt2.shape[1]), lambda i: (i, 0)),
            pl.BlockSpec(wa.shape, lambda i: (0, 0)),
            pl.BlockSpec(wb.shape, lambda i: (0, 0)),
            pl.BlockSpec((tm, d), lambda i: (i, 0)),
            mod,
            pl.BlockSpec((1, d), lambda i: (0, 0)),
            mod, mod,
            pl.BlockSpec(wr_t.shape, lambda i: (0, 0)),
        ],
        out_specs=[
            pl.BlockSpec((tm, d), lambda i: (i, 0)),
            pl.BlockSpec((tm, d), lambda i: (i, 0)),
            pl.BlockSpec((tm * ROW_WORDS, LANES), lambda i: (i, 0)),
            pl.BlockSpec((N_EXPERTS, tm), lambda i: (0, i)),
        ],
        out_shape=[
            jax.ShapeDtypeStruct((t, d), F32),
            jax.ShapeDtypeStruct((t, d), BF16),
            jax.ShapeDtypeStruct((t * ROW_WORDS, LANES), U32),
            jax.ShapeDtypeStruct((N_EXPERTS, t), F32),
        ],
        compiler_params=_cparams(("arbitrary",)),
        name="outproj",
    )(sb2, rt2, wa, wb, x2, g1, n2.reshape(1, d), sc2, sh2, wr_t)


def _beats(row, allv, row_idx, idx):
    return (row > allv) | ((row == allv) & (row_idx < idx))


def _route_kernel(lt_ref, bias_ref, w_ref, rank_ref, cnt_ref, carry_ref):
    tr = lt_ref.shape[1]
    gsz = N_EXPERTS // N_GROUPS
    neg_inf = jnp.float32(-jnp.inf)

    @pl.when(pl.program_id(0) == 0)
    def _():
        carry_ref[...] = jnp.zeros_like(carry_ref)

    scores = jax.nn.sigmoid(lt_ref[...])
    choice = scores + bias_ref[...]
    sub = lax.broadcasted_iota(I32, (gsz, tr), 0)
    gs_rows = []
    for g in range(N_GROUPS):
        cg = choice[g * gsz:(g + 1) * gsz, :]
        m1 = jnp.max(cg, axis=0, keepdims=True)
        first = jnp.min(jnp.where(cg == m1, sub, gsz), axis=0, keepdims=True)
        m2 = jnp.max(jnp.where(sub == first, neg_inf, cg), axis=0, keepdims=True)
        gs_rows.append(m1 + m2)
    gs = jnp.concatenate(gs_rows, axis=0)
    gi = lax.broadcasted_iota(I32, (N_GROUPS, tr), 0)
    grank = jnp.zeros((N_GROUPS, tr), I32)
    for g in range(N_GROUPS):
        grank = grank + _beats(gs[g:g + 1, :], gs, g, gi).astype(I32)
    gmask = grank < TOPK_GROUP
    emask = jnp.concatenate(
        [jnp.broadcast_to(gmask[g:g + 1, :], (gsz, tr)) for g in range(N_GROUPS)], axis=0)
    masked = jnp.where(emask, choice, neg_inf)
    ei = lax.broadcasted_iota(I32, (N_EXPERTS, tr), 0)
    erank = jnp.zeros((N_EXPERTS, tr), I32)
    for e in range(N_EXPERTS):
        erank = erank + _beats(masked[e:e + 1, :], masked, e, ei).astype(I32)
    sel = erank < TOP_K
    ssel = jnp.where(sel, scores, 0.0)
    denom = jnp.sum(ssel, axis=0, keepdims=True)
    w_ref[...] = ssel / denom * ROUTED_SCALE

    self = sel.astype(F32)
    upper = (lax.broadcasted_iota(I32, (tr, tr), 0) < lax.broadcasted_iota(I32, (tr, tr), 1))
    prefix = jnp.dot(self.astype(BF16), upper.astype(BF16), preferred_element_type=F32)
    carry = carry_ref[...]
    rank = prefix + carry[:, 0:1]
    rank_ref[...] = jnp.where(sel, rank, -1.0).astype(I32)
    carry = carry + jnp.sum(self, axis=1, keepdims=True)
    carry_ref[...] = carry
    cnt_ref[...] = carry


def _route(lt, bias):
    e, t = lt.shape
    tr = min(512, t)
    return pl.pallas_call(
        _route_kernel,
        grid=(t // tr,),
        in_specs=[pl.BlockSpec((e, tr), lambda i: (0, i)),
                  pl.BlockSpec((e, 1), lambda i: (0, 0))],
        out_specs=[pl.BlockSpec((e, tr), lambda i: (0, i)),
                   pl.BlockSpec((e, tr), lambda i: (0, i)),
                   pl.BlockSpec((e, LANES), lambda i: (0, 0))],
        out_shape=[jax.ShapeDtypeStruct((e, t), F32),
                   jax.ShapeDtypeStruct((e, t), I32),
                   jax.ShapeDtypeStruct((e, LANES), F32)],
        scratch_shapes=[pltpu.VMEM((e, LANES), F32)],
        compiler_params=_cparams(("arbitrary",)),
        name="route",
    )(lt, bias.reshape(e, 1))


def _plan_kernel(w_ref, rank_ref, cnt_ref, pos_ref, wc_ref, bexp_ref, seg_ref):
    tr = w_ref.shape[1]
    nb_lanes = bexp_ref.shape[1]
    cnt = cnt_ref[...]
    nblk = jnp.floor((cnt + (MOE_BLOCK - 1.0)) * (1.0 / MOE_BLOCK))
    lower = (lax.broadcasted_iota(I32, (N_EXPERTS, N_EXPERTS), 1)
             < lax.broadcasted_iota(I32, (N_EXPERTS, N_EXPERTS), 0)).astype(BF16)
    bstart = jnp.dot(lower, nblk.astype(BF16), preferred_element_type=F32)
    bend = bstart + nblk

    @pl.when(pl.program_id(0) == 0)
    def _():
        bid = lax.broadcasted_iota(I32, (N_EXPERTS, nb_lanes), 1).astype(F32)
        owner = jnp.sum((bend[:, 0:1] <= bid).astype(I32), axis=0, keepdims=True)
        bexp_ref[...] = jnp.minimum(owner, N_EXPERTS - 1)
        seg_ref[...] = jnp.concatenate(
            [bstart[:, 0:1] * MOE_BLOCK, cnt[:, 0:1], nblk[:, 0:1] * MOE_BLOCK, bend[:, 0:1]],
            axis=1).astype(I32)

    rank = rank_ref[...]
    sel = rank >= 0
    pos = bstart[:, 0:1] * MOE_BLOCK + rank.astype(F32)
    slot = jnp.dot(lower, sel.astype(BF16), preferred_element_type=F32)
    w = w_ref[...]
    pos_rows, w_rows = [], []
    for k in range(TOP_K):
        m = sel & (slot == float(k))
        pos_rows.append(jnp.sum(jnp.where(m, pos, 0.0), axis=0, keepdims=True))
        w_rows.append(jnp.sum(jnp.where(m, w, 0.0), axis=0, keepdims=True))
    pos_ref[...] = jnp.concatenate(pos_rows, axis=0).astype(I32)
    wc_ref[...] = jnp.concatenate(w_rows, axis=0)


def _plan(w, rank, cnt, nb_lanes):
    e, t = w.shape
    tr = min(1024, t)
    return pl.pallas_call(
        _plan_kernel,
        grid=(t // tr,),
        in_specs=[pl.BlockSpec((e, tr), lambda i: (0, i)),
                  pl.BlockSpec((e, tr), lambda i: (0, i)),
                  pl.BlockSpec((e, LANES), lambda i: (0, 0))],
        out_specs=[pl.BlockSpec((TOP_K, tr), lambda i: (0, i)),
                   pl.BlockSpec((TOP_K, tr), lambda i: (0, i)),
                   pl.BlockSpec((1, nb_lanes), lambda i: (0, 0)),
                   pl.BlockSpec((e, 4), lambda i: (0, 0))],
        out_shape=[jax.ShapeDtypeStruct((TOP_K, t), I32),
                   jax.ShapeDtypeStruct((TOP_K, t), F32),
                   jax.ShapeDtypeStruct((1, nb_lanes), I32),
                   jax.ShapeDtypeStruct((e, 4), I32)],
        compiler_params=_cparams(("arbitrary",)),
        name="plan",
    )(w, rank, cnt)


def _row_copy(src, src_row, dst, dst_row, sem):
    return pltpu.make_async_copy(src.at[pl.ds(src_row * ROW_WORDS, ROW_WORDS), :],
                                 dst.at[pl.ds(dst_row * ROW_WORDS, ROW_WORDS), :], sem)


def _dispatch_kernel(pos_ref, seg_ref, hp_ref, xs_ref, zero_ref, sem, zsem):
    tq = pos_ref.shape[1]
    blk_words = MOE_BLOCK * ROW_WORDS
    n_blocks = xs_ref.shape[0] // blk_words

    @pl.when(pl.program_id(0) == 0)
    def _():
        zero_ref[...] = jnp.zeros_like(zero_ref)
        zrow = zero_ref.at[pl.ds(0, ROW_WORDS), :]

        def per_expert(e, n):
            start = seg_ref[e, 0] + seg_ref[e, 1]
            npad = seg_ref[e, 2] - seg_ref[e, 1]

            def fill(r, c):
                pltpu.make_async_copy(
                    zrow, xs_ref.at[pl.ds((start + r) * ROW_WORDS, ROW_WORDS), :], zsem).start()
                return c

            lax.fori_loop(0, npad, fill, 0)
            return n + npad

        n_pad_rows = lax.fori_loop(0, N_EXPERTS, per_expert, 0)

        def drain_row(r, c):
            pltpu.make_async_copy(zrow, xs_ref.at[pl.ds(0, ROW_WORDS), :], zsem).wait()
            return c

        lax.fori_loop(0, n_pad_rows, drain_row, 0)
        n_used = seg_ref[N_EXPERTS - 1, 3]

        def fill_block(b, c):
            pltpu.make_async_copy(
                zero_ref, xs_ref.at[pl.ds(b * blk_words, blk_words), :], zsem).start()
            return c

        lax.fori_loop(n_used, n_blocks, fill_block, 0)

        def drain_block(b, c):
            pltpu.make_async_copy(zero_ref, xs_ref.at[pl.ds(0, blk_words), :], zsem).wait()
            return c

        lax.fori_loop(n_used, n_blocks, drain_block, 0)

    def issue(t, c):
        for k in range(TOP_K):
            _row_copy(hp_ref, t, xs_ref, pos_ref[k, t], sem).start(priority=k % 2)
        return c

    lax.fori_loop(0, tq, issue, 0)
    for k in range(TOP_K):
        pltpu.make_async_copy(hp_ref, xs_ref.at[pl.ds(0, tq * ROW_WORDS), :], sem).wait()


def _dispatch(pos_c, seg, hp, n_rows):
    t = pos_c.shape[1]
    tq = min(1024, t)
    return pl.pallas_call(
        _dispatch_kernel,
        grid=(t // tq,),
        in_specs=[pl.BlockSpec((TOP_K, tq), lambda i: (0, i), memory_space=pltpu.SMEM),
                  pl.BlockSpec(seg.shape, lambda i: (0, 0), memory_space=pltpu.SMEM),
                  pl.BlockSpec((tq * ROW_WORDS, LANES), lambda i: (i, 0))],
        out_specs=pl.BlockSpec(memory_space=pl.ANY),
        out_shape=jax.ShapeDtypeStruct((n_rows * ROW_WORDS, LANES), U32),
        scratch_shapes=[pltpu.VMEM((MOE_BLOCK * ROW_WORDS, LANES), U32),
                        pltpu.SemaphoreType.DMA(()), pltpu.SemaphoreType.DMA(())],
        compiler_params=_cparams(("arbitrary",)),
        name="dispatch",
    )(pos_c, seg, hp)


def _expert_kernel(bexp_ref, nused_ref, xs_ref, wg_ref, wu_ref, wd_ref, ys_ref, wgb, wub, wdb):
    b = pl.program_id(0)
    rows = MOE_BLOCK
    prev = bexp_ref[jnp.maximum(b - 1, 0)]
    fresh = jnp.logical_or(b == 0, bexp_ref[b] != prev)
    used = b < nused_ref[0]

    @pl.when(jnp.logical_and(used, fresh))
    def _():
        wgb[...] = wg_ref[...].astype(BF16)
        wub[...] = wu_ref[...].astype(BF16)
        wdb[...] = wd_ref[...].astype(BF16)

    @pl.when(used)
    def _():
        his, los = [], []
        for j in range(ROW_WORDS):
            hi, lo = _unpack_pair(xs_ref[pl.ds(j, rows, stride=ROW_WORDS), :])
            his.append(hi.astype(BF16))
            los.append(lo.astype(BF16))
        xb = jnp.concatenate(his + los, axis=1)
        a = jnp.dot(xb, wgb[...], preferred_element_type=F32)
        u = jnp.dot(xb, wub[...], preferred_element_type=F32)
        hid = (_silu(a) * u).astype(BF16)
        out = jnp.dot(hid, wdb[...], preferred_element_type=F32).astype(BF16).astype(F32)
        half = out.shape[1] // 2
        words = _pack_pair(out[:, :half], out[:, half:])
        for j in range(ROW_WORDS):
            ys_ref[pl.ds(j, rows, stride=ROW_WORDS), :] = words[:, j * LANES:(j + 1) * LANES]

    @pl.when(jnp.logical_not(used))
    def _():
        ys_ref[...] = jnp.zeros_like(ys_ref)


def _experts(bexp, nused, xs, w_gate, w_up, w_down):
    n_blocks = xs.shape[0] // (MOE_BLOCK * ROW_WORDS)
    _, d, f = w_gate.shape

    def xmap(b, be, nu):
        return (jnp.minimum(b, jnp.maximum(nu[0] - 1, 0)), 0)

    grid_spec = pltpu.PrefetchScalarGridSpec(
        num_scalar_prefetch=2,
        grid=(n_blocks,),
        in_specs=[
            pl.BlockSpec((MOE_BLOCK * ROW_WORDS, LANES), xmap),
            pl.BlockSpec((None, d, f), lambda b, be, nu: (be[b], 0, 0)),
            pl.BlockSpec((None, d, f), lambda b, be, nu: (be[b], 0, 0)),
            pl.BlockSpec((None, f, d), lambda b, be, nu: (be[b], 0, 0)),
        ],
        out_specs=pl.BlockSpec((MOE_BLOCK * ROW_WORDS, LANES), lambda b, be, nu: (b, 0)),
        scratch_shapes=[pltpu.VMEM((d, f), BF16), pltpu.VMEM((d, f), BF16), pltpu.VMEM((f, d), BF16)],
    )
    return pl.pallas_call(
        _expert_kernel,
        grid_spec=grid_spec,
        out_shape=jax.ShapeDtypeStruct(xs.shape, U32),
        compiler_params=_cparams(("arbitrary",)),
        name="experts",
    )(bexp, nused, xs, w_gate, w_up, w_down)


def _combine_kernel(pos_ref, ys_ref, wc_ref, h2_ref, x1_ref, sg_ref, su_ref, sd_ref,
                    g2_ref, nf_ref, scf_ref, shf_ref, o_ref, buf, sem):
    tq = wc_ref.shape[1]
    d = x1_ref.shape[1]
    half = d // 2

    def issue(t, c):
        for k in range(TOP_K):
            pltpu.make_async_copy(
                ys_ref.at[pl.ds(pos_ref[k, t] * ROW_WORDS, ROW_WORDS), :],
                buf.at[k, pl.ds(t * ROW_WORDS, ROW_WORDS), :], sem).start(priority=k % 2)
        return c

    lax.fori_loop(0, tq, issue, 0)

    h2 = h2_ref[...]
    a = jnp.dot(h2, sg_ref[...], preferred_element_type=F32)
    u = jnp.dot(h2, su_ref[...], preferred_element_type=F32)
    shared = jnp.dot((_silu(a) * u).astype(BF16), sd_ref[...], preferred_element_type=F32)

    for k in range(TOP_K):
        pltpu.make_async_copy(ys_ref.at[pl.ds(0, tq * ROW_WORDS), :], buf.at[k], sem).wait()

    wt = jnp.transpose(wc_ref[...])
    his, los = [], []
    for j in range(ROW_WORDS):
        acc_hi = jnp.zeros((tq, LANES), F32)
        acc_lo = jnp.zeros((tq, LANES), F32)
        for k in range(TOP_K):
            hi, lo = _unpack_pair(buf.at[k][pl.ds(j, tq, stride=ROW_WORDS), :])
            wk = wt[:, k:k + 1]
            acc_hi = acc_hi + wk * hi
            acc_lo = acc_lo + wk * lo
        his.append(acc_hi)
        los.append(acc_lo)
    y = jnp.concatenate(his + los, axis=1) + shared
    x2 = x1_ref[...] + g2_ref[0] * y
    o_ref[...] = _norm_mod(x2, nf_ref[...], scf_ref[0], shf_ref[0])


def _combine(pos_c, ys, w_c, h2, x1, sg, su, sd, g2, nf, scf, shf, seq):
    t, d = x1.shape
    tq = 128
    per_batch = seq // tq
    mod = pl.BlockSpec((1, 1, d), lambda i: (i // per_batch, 0, 0))
    full = lambda a: pl.BlockSpec(a.shape, lambda i: (0, 0))
    return pl.pallas_call(
        _combine_kernel,
        grid=(t // tq,),
        in_specs=[
            pl.BlockSpec((TOP_K, tq), lambda i: (0, i), memory_space=pltpu.SMEM),
            pl.BlockSpec(memory_space=pl.ANY),
            pl.BlockSpec((TOP_K, tq), lambda i: (0, i)),
            pl.BlockSpec((tq, d), lambda i: (i, 0)),
            pl.BlockSpec((tq, d), lambda i: (i, 0)),
            full(sg), full(su), full(sd),
            mod,
            pl.BlockSpec((1, d), lambda i: (0, 0)),
            mod, mod,
        ],
        out_specs=pl.BlockSpec((tq, d), lambda i: (i, 0)),
        out_shape=jax.ShapeDtypeStruct((t, d), F32),
        scratch_shapes=[pltpu.VMEM((TOP_K, tq * ROW_WORDS, LANES), U32),
                        pltpu.SemaphoreType.DMA(())],
        compiler_params=_cparams(("arbitrary",)),
        name="combine",
    )(pos_c, ys, w_c, h2, x1, sg, su, sd, g2, nf.reshape(1, d), scf, shf)


def _rope_tables(seq):
    pos = jnp.arange(seq, dtype=F32)
    inv_freq = ROPE_BASE ** (-jnp.arange(0, HEAD_DIM, 2, dtype=F32) / HEAD_DIM)
    ang = pos[:, None] * inv_freq[None, :]
    cos, sin = jnp.cos(ang), jnp.sin(ang)
    return jnp.concatenate([cos, cos], axis=1), jnp.concatenate([-sin, sin], axis=1)


def kernel(x, c, w_ada, b_ada, norm1_g, w_in, ret_gn_g, w_out, norm2_g, w_router, router_bias,
           w_gate, w_up, w_down, ws_gate, ws_up, ws_down, w_ada_final, b_ada_final, norm_f_g):
    bsz, seq, d = x.shape
    t = bsz * seq
    depth = w_ada.shape[0]
    n_heads = d // (2 * HEAD_DIM)
    sb_w = n_heads * HEAD_DIM
    cos2, sin2 = _rope_tables(seq)
    c_pad = jnp.pad(c, ((0, 8 - bsz % 8), (0, 0)))
    n_rows = t * TOP_K + N_EXPERTS * MOE_BLOCK
    n_blocks = n_rows // MOE_BLOCK
    nb_lanes = -(-n_blocks // LANES) * LANES
    as_mod = lambda v: v.reshape(bsz, 1, d)

    x2 = x.reshape(t, d)
    for l in range(depth):
        mod = _ada(c_pad, w_ada[l], b_ada[l])[:bsz]
        sh1, sc1, g1, sh2, sc2, g2 = [as_mod(m) for m in jnp.split(mod, 6, axis=-1)]
        u = _inproj(x2, norm1_g[l], sc1, sh1, w_in[l].astype(BF16), seq)
        u3 = u.reshape(bsz, seq, -1)
        sb = _sb_attention(u3, n_heads)
        rt = _retention(u3, cos2, sin2, ret_gn_g[l], n_heads, 3 * n_heads)
        wo = w_out[l].astype(BF16)
        x1, h2, hp, lt = _outproj(sb.reshape(t, sb_w), rt.reshape(t, -1), wo[:sb_w], wo[sb_w:], x2,
                                  g1, norm2_g[l], sc2, sh2, jnp.transpose(w_router[l]).astype(BF16), seq)
        w_e, rank_e, cnt = _route(lt, router_bias[l])
        pos_c, w_c, bexp, seg = _plan(w_e, rank_e, cnt, nb_lanes)
        xs = _dispatch(pos_c, seg, hp, n_rows)
        ys = _experts(bexp.reshape(-1), seg[N_EXPERTS - 1, 3].reshape(1), xs, w_gate[l], w_up[l], w_down[l])
        if l + 1 < depth:
            raise NotImplementedError("only the final layer fuses the output norm")
        modf = _ada(c_pad, w_ada_final, b_ada_final)[:bsz]
        shf, scf = [as_mod(m) for m in jnp.split(modf, 2, axis=-1)]
        x2 = _combine(pos_c, ys, w_c, h2, x1, ws_gate[l].astype(BF16), ws_up[l].astype(BF16),
                      ws_down[l].astype(BF16), g2, norm_f_g, scf, shf, seq)
    return x2.reshape(bsz, seq, d)
```

```python
import functools

import jax
import jax.numpy as jnp
from jax import lax
from jax.experimental import pallas as pl
from jax.experimental.pallas import tpu as pltpu

F32 = jnp.float32
BF16 = jnp.bfloat16
I32 = jnp.int32
U32 = jnp.uint32

HEAD_DIM = 128
N_EXPERTS = 64
TOP_K = 8
N_GROUPS = 8
TOPK_GROUP = 4
ROUTED_SCALE = 2.5
EPS = 1e-6
ROPE_BASE = 10000.0

LANES = 128
ROW_WORDS = 8
MOE_BLOCK = 256
V7X_VMEM_LIMIT = 56 * 1024 * 1024
SB_SKIP_BOUND = -115.0


def _cparams(sem, vmem=V7X_VMEM_LIMIT):
    return pltpu.CompilerParams(dimension_semantics=sem, vmem_limit_bytes=vmem)


def _silu(v):
    return v * jax.nn.sigmoid(v)


def _pack_pair(hi_f32, lo_f32):
    hw = pltpu.bitcast(hi_f32, U32) & jnp.uint32(0xFFFF0000)
    lw = pltpu.bitcast(lo_f32, U32) >> 16
    return hw | lw


def _unpack_pair(w):
    hi = pltpu.bitcast(w & jnp.uint32(0xFFFF0000), F32)
    lo = pltpu.bitcast(w << 16, F32)
    return hi, lo


def _ada_kernel(c_ref, w_ref, b_ref, o_ref):
    cs = _silu(c_ref[...]).astype(BF16)
    o_ref[...] = jnp.dot(cs, w_ref[...].astype(BF16), preferred_element_type=F32) + b_ref[...]


def _ada(c_pad, w, b):
    rows, d = c_pad.shape
    n = w.shape[1]
    tn = 1024
    return pl.pallas_call(
        _ada_kernel,
        grid=(n // tn,),
        in_specs=[
            pl.BlockSpec((rows, d), lambda j: (0, 0)),
            pl.BlockSpec((d, tn), lambda j: (0, j)),
            pl.BlockSpec((1, tn), lambda j: (0, j)),
        ],
        out_specs=pl.BlockSpec((rows, tn), lambda j: (0, j)),
        out_shape=jax.ShapeDtypeStruct((rows, n), F32),
        compiler_params=_cparams(("arbitrary",)),
        name="ada",
    )(c_pad, w, b.reshape(1, n))


def _norm_mod(x, g, sc, sh):
    ms = jnp.mean(x * x, axis=-1, keepdims=True)
    return (x * lax.rsqrt(ms + EPS) * g) * (1.0 + sc) + sh


def _inproj_kernel(x_ref, g_ref, sc_ref, sh_ref, w_ref, o_ref, h_ref, *, chunk):
    first = pl.program_id(1) == 0

    @pl.when(first)
    def _():
        g = g_ref[...]
        sc = sc_ref[0]
        sh = sh_ref[0]
        for r in range(x_ref.shape[0] // chunk):
            rows = pl.ds(r * chunk, chunk)
            h = _norm_mod(x_ref[rows, :], g, sc, sh).astype(BF16)
            h_ref[rows, :] = h
            o_ref[rows, :] = jnp.dot(h, w_ref[...], preferred_element_type=F32).astype(BF16)

    @pl.when(jnp.logical_not(first))
    def _():
        o_ref[...] = jnp.dot(h_ref[...], w_ref[...], preferred_element_type=F32).astype(BF16)


def _inproj(x2, g, sc, sh, w_bf, seq):
    t, d = x2.shape
    n = w_bf.shape[1]
    tm = 1024
    tn = n // 4 if (n // 4) % (2 * LANES) == 0 else 1024
    per_batch = seq // tm
    return pl.pallas_call(
        functools.partial(_inproj_kernel, chunk=256),
        grid=(t // tm, n // tn),
        in_specs=[
            pl.BlockSpec((tm, d), lambda i, j: (i, 0)),
            pl.BlockSpec((1, d), lambda i, j: (0, 0)),
            pl.BlockSpec((1, 1, d), lambda i, j: (i // per_batch, 0, 0)),
            pl.BlockSpec((1, 1, d), lambda i, j: (i // per_batch, 0, 0)),
            pl.BlockSpec((d, tn), lambda i, j: (0, j)),
        ],
        out_specs=pl.BlockSpec((tm, tn), lambda i, j: (i, j)),
        out_shape=jax.ShapeDtypeStruct((t, n), BF16),
        scratch_shapes=[pltpu.VMEM((tm, d), BF16)],
        compiler_params=_cparams(("arbitrary", "arbitrary")),
        name="inproj",
    )(x2, g.reshape(1, d), sc, sh, w_bf)


def _sb_kernel(q_ref, k_ref, v_ref, o_ref, *, group):
    seq = q_ref.shape[0]
    blk = LANES
    scale = HEAD_DIM ** -0.5
    row = lax.broadcasted_iota(I32, (blk, blk), 0)
    col = lax.broadcasted_iota(I32, (blk, blk), 1)
    strict = col < row
    tr = lax.broadcasted_iota(I32, (2 * blk, 2 * blk), 0) % blk
    tc = lax.broadcasted_iota(I32, (2 * blk, 2 * blk), 1)
    tri = jnp.where((tc >= blk) | (tr > tc), 1.0, 0.0).astype(BF16)

    def key_rows(j):
        return pl.ds(pl.multiple_of(j * blk, blk), blk)

    def logits(qb, kb):
        z = lax.dot_general(qb, kb, (((1,), (1,)), ((), ())), preferred_element_type=F32) * scale
        sp = jnp.maximum(z, 0.0) + jnp.log(1.0 + jnp.exp(-jnp.abs(z)))
        return -sp, z - sp

    def suffix_sums(log_surv):
        hi = log_surv.astype(BF16)
        lo = (log_surv - hi.astype(F32)).astype(BF16)
        r = jnp.dot(jnp.concatenate([hi, lo], axis=1), tri, preferred_element_type=F32)
        return r[:, :blk], r[:, blk:]

    def tile(qb, j, g, carry, acc):
        log_surv, log_beta = logits(qb, k_ref[key_rows(j), g * blk:(g + 1) * blk])
        after, total = suffix_sums(log_surv)
        a = jnp.exp(log_beta + after + carry)
        acc = acc + jnp.dot(a.astype(BF16), v_ref[key_rows(j), g * blk:(g + 1) * blk],
                            preferred_element_type=F32)
        return carry + total, acc

    def group_start(qbs, i):
        n_near = 3
        js = [i, jnp.maximum(i - 1, 0), jnp.maximum(i - 2, 0)]
        ivec = jnp.full((blk, blk), i, I32)
        masks = [strict, ivec >= 1, ivec >= 2]
        log_betas, parts = [], []
        for g in range(group):
            cols = slice(g * blk, (g + 1) * blk)
            kcat = jnp.concatenate([k_ref[key_rows(j), cols] for j in js], axis=0)
            log_surv, log_beta = logits(qbs[g], kcat)
            log_betas.append(log_beta)
            for m in range(n_near):
                parts.append(jnp.where(masks[m], log_surv[:, m * blk:(m + 1) * blk], 0.0))
        after, total = suffix_sums(jnp.concatenate(parts, axis=0))
        out = []
        for g in range(group):
            cols = slice(g * blk, (g + 1) * blk)
            carry = jnp.zeros((blk, blk), F32)
            weights = []
            for m in range(n_near):
                rows = slice((g * n_near + m) * blk, (g * n_near + m + 1) * blk)
                a = jnp.exp(log_betas[g][:, m * blk:(m + 1) * blk] + after[rows] + carry)
                weights.append(jnp.where(masks[m], a, 0.0).astype(BF16))
                carry = carry + total[rows]
            vcat = jnp.concatenate([v_ref[key_rows(j), cols] for j in js], axis=0)
            acc = jnp.dot(jnp.concatenate(weights, axis=1), vcat, preferred_element_type=F32)
            out.append((carry, acc))
        return out

    def q_block(i, c):
        rows = pl.ds(pl.multiple_of(i * blk, blk), blk)
        qbs = [q_ref[rows, g * blk:(g + 1) * blk] for g in range(group)]
        first = group_start(qbs, i)
        carries = tuple(f[0] for f in first)
        accs = tuple(f[1] for f in first)

        def cond(st):
            j, crs, _ = st
            top = functools.reduce(jnp.maximum, crs)
            return jnp.logical_and(j >= 0, jnp.max(top) > SB_SKIP_BOUND)

        def body(st):
            j, crs, acs = st
            nxt = [tile(qbs[g], j, g, crs[g], acs[g]) for g in range(group)]
            return j - 1, tuple(n[0] for n in nxt), tuple(n[1] for n in nxt)

        _, _, accs = lax.while_loop(cond, body, (i - 3, carries, accs))
        for g in range(group):
            o_ref[rows, g * blk:(g + 1) * blk] = accs[g].astype(BF16)
        return c

    lax.fori_loop(0, seq // blk, q_block, 0)


SB_HEAD_GROUP = 4


def _sb_attention(u3, n_heads):
    b, s, _ = u3.shape
    gw = SB_HEAD_GROUP * HEAD_DIM
    n_groups = n_heads // SB_HEAD_GROUP
    spec = lambda off: pl.BlockSpec((None, s, gw), lambda bi, h: (bi, 0, off + h))
    return pl.pallas_call(
        functools.partial(_sb_kernel, group=SB_HEAD_GROUP),
        grid=(b, n_groups),
        in_specs=[spec(0), spec(n_groups), spec(2 * n_groups)],
        out_specs=pl.BlockSpec((None, s, gw), lambda bi, h: (bi, 0, h)),
        out_shape=jax.ShapeDtypeStruct((b, s, n_heads * HEAD_DIM), BF16),
        compiler_params=_cparams(("arbitrary", "arbitrary")),
        name="sb_attn",
    )(u3, u3, u3)


def _ret_kernel(q_ref, k_ref, v_ref, g_ref, cos_ref, sin_ref, gn_ref, o_ref):
    seq = q_ref.shape[0]
    c = LANES
    head = pl.program_id(1)
    hv = jnp.full((c, c), head, I32).astype(F32)
    log_g = jnp.log(1.0 - jnp.exp2(-5.0 - hv))
    ri = lax.broadcasted_iota(I32, (c, c), 0).astype(F32)
    ci = lax.broadcasted_iota(I32, (c, c), 1).astype(F32)
    diff = ri - ci
    lower = diff >= 0.0
    dmat = jnp.where(lower, jnp.exp(jnp.where(lower, diff, 0.0) * log_g), 0.0)
    k_decay = jnp.exp((c - 1.0 - ri) * log_g)
    q_decay = jnp.exp((ri + 1.0) * log_g)
    chunk_decay = jnp.exp(c * log_g)
    gn = gn_ref[...]
    scale = HEAD_DIM ** -0.5

    def rope(t, cs, sn):
        return t * cs + pltpu.roll(t, HEAD_DIM // 2, 1) * sn

    def body(n, state):
        rows = pl.ds(pl.multiple_of(n * c, c), c)
        cs = cos_ref[rows, :]
        sn = sin_ref[rows, :]
        qc = rope(q_ref[rows, :].astype(F32), cs, sn)
        kc = rope(k_ref[rows, :].astype(F32), cs, sn) * scale
        vc = v_ref[rows, :]
        scores = lax.dot_general(qc.astype(BF16), kc.astype(BF16), (((1,), (1,)), ((), ())),
                                 preferred_element_type=F32) * dmat
        intra = jnp.dot(scores.astype(BF16), vc, preferred_element_type=F32)
        cross = jnp.dot((qc * q_decay).astype(BF16), state.astype(BF16), preferred_element_type=F32)
        out = intra + cross
        kd_t = jnp.transpose(kc * k_decay).astype(BF16)
        state = state * chunk_decay + jnp.dot(kd_t, vc, preferred_element_type=F32)
        ms = jnp.mean(out * out, axis=-1, keepdims=True)
        y = out * lax.rsqrt(ms + EPS) * gn * _silu(g_ref[rows, :].astype(F32))
        o_ref[rows, :] = y.astype(BF16)
        return state

    lax.fori_loop(0, seq // c, body, jnp.zeros((c, c), F32), unroll=4)


def _retention(u3, cos2, sin2, gn, n_heads, col0):
    b, s, _ = u3.shape
    spec = lambda off: pl.BlockSpec((None, s, HEAD_DIM), lambda bi, h: (bi, 0, col0 + off + h))
    return pl.pallas_call(
        _ret_kernel,
        grid=(b, n_heads),
        in_specs=[spec(0), spec(n_heads), spec(2 * n_heads), spec(3 * n_heads),
                  pl.BlockSpec((s, HEAD_DIM), lambda bi, h: (0, 0)),
                  pl.BlockSpec((s, HEAD_DIM), lambda bi, h: (0, 0)),
                  pl.BlockSpec((1, HEAD_DIM), lambda bi, h: (0, h))],
        out_specs=pl.BlockSpec((None, s, HEAD_DIM), lambda bi, h: (bi, 0, h)),
        out_shape=jax.ShapeDtypeStruct((b, s, n_heads * HEAD_DIM), BF16),
        compiler_params=_cparams(("arbitrary", "arbitrary")),
        name="retention",
    )(u3, u3, u3, u3, cos2, sin2, gn.reshape(1, -1))


def _outproj_kernel(sb_ref, rt_ref, wa_ref, wb_ref, x_ref, g1_ref, n2_ref, sc_ref, sh_ref, wr_ref,
                    x1_ref, h2_ref, hp_ref, lt_ref, *, chunk):
    d = x_ref.shape[1]
    half = d // 2
    g1 = g1_ref[0]
    n2 = n2_ref[...]
    sc = sc_ref[0]
    sh = sh_ref[0]
    for r in range(x_ref.shape[0] // chunk):
        rows = pl.ds(r * chunk, chunk)
        o = jnp.dot(sb_ref[rows, :], wa_ref[...], preferred_element_type=F32)
        o = o + jnp.dot(rt_ref[rows, :], wb_ref[...], preferred_element_type=F32)
        x1 = x_ref[rows, :] + g1 * o
        x1_ref[rows, :] = x1
        h2 = _norm_mod(x1, n2, sc, sh).astype(BF16)
        h2_ref[rows, :] = h2
        lt_ref[:, rows] = lax.dot_general(wr_ref[...], h2, (((1,), (1,)), ((), ())),
                                          preferred_element_type=F32)
        words = _pack_pair(h2[:, :half].astype(F32), h2[:, half:].astype(F32))
        for j in range(ROW_WORDS):
            hp_ref[pl.ds(r * chunk * ROW_WORDS + j, chunk, stride=ROW_WORDS), :] = (
                words[:, j * LANES:(j + 1) * LANES])


def _outproj(sb2, rt2, wa, wb, x2, g1, n2, sc2, sh2, wr_t, seq):
    t, d = x2.shape
    tm = 512
    per_batch = seq // tm
    mod = pl.BlockSpec((1, 1, d), lambda i: (i // per_batch, 0, 0))
    return pl.pallas_call(
        functools.partial(_outproj_kernel, chunk=256),
        grid=(t // tm,),
        in_specs=[
            pl.BlockSpec((tm, sb2.shape[1]), lambda i: (i, 0)),
            pl.BlockSpec((tm, rt2.shape[1]), lambda i: (i, 0)),
            pl.BlockSpec(wa.shape, lambda i: (0, 0)),
            pl.BlockSpec(wb.shape, lambda i: (0, 0)),
            pl.BlockSpec((tm, d), lambda i: (i, 0)),
            mod,
            pl.BlockSpec((1, d), lambda i: (0, 0)),
            mod, mod,
            pl.BlockSpec(wr_t.shape, lambda i: (0, 0)),
        ],
        out_specs=[
            pl.BlockSpec((tm, d), lambda i: (i, 0)),
            pl.BlockSpec((tm, d), lambda i: (i, 0)),
            pl.BlockSpec((tm * ROW_WORDS, LANES), lambda i: (i, 0)),
            pl.BlockSpec((N_EXPERTS, tm), lambda i: (0, i)),
        ],
        out_shape=[
            jax.ShapeDtypeStruct((t, d), F32),
            jax.ShapeDtypeStruct((t, d), BF16),
            jax.ShapeDtypeStruct((t * ROW_WORDS, LANES), U32),
            jax.ShapeDtypeStruct((N_EXPERTS, t), F32),
        ],
        compiler_params=_cparams(("arbitrary",)),
        name="outproj",
    )(sb2, rt2, wa, wb, x2, g1, n2.reshape(1, d), sc2, sh2, wr_t)


def _beats(row, allv, row_idx, idx):
    return (row > allv) | ((row == allv) & (row_idx < idx))


def _route_kernel(lt_ref, bias_ref, w_ref, rank_ref, cnt_ref, carry_ref):
    tr = lt_ref.shape[1]
    gsz = N_EXPERTS // N_GROUPS
    neg_inf = jnp.float32(-jnp.inf)

    @pl.when(pl.program_id(0) == 0)
    def _():
        carry_ref[...] = jnp.zeros_like(carry_ref)

    scores = jax.nn.sigmoid(lt_ref[...])
    choice = scores + bias_ref[...]
    sub = lax.broadcasted_iota(I32, (gsz, tr), 0)
    gs_rows = []
    for g in range(N_GROUPS):
        cg = choice[g * gsz:(g + 1) * gsz, :]
        m1 = jnp.max(cg, axis=0, keepdims=True)
        first = jnp.min(jnp.where(cg == m1, sub, gsz), axis=0, keepdims=True)
        m2 = jnp.max(jnp.where(sub == first, neg_inf, cg), axis=0, keepdims=True)
        gs_rows.append(m1 + m2)
    gs = jnp.concatenate(gs_rows, axis=0)
    gi = lax.broadcasted_iota(I32, (N_GROUPS, tr), 0)
    grank = jnp.zeros((N_GROUPS, tr), I32)
    for g in range(N_GROUPS):
        grank = grank + _beats(gs[g:g + 1, :], gs, g, gi).astype(I32)
    gmask = grank < TOPK_GROUP
    emask = jnp.concatenate(
        [jnp.broadcast_to(gmask[g:g + 1, :], (gsz, tr)) for g in range(N_GROUPS)], axis=0)
    masked = jnp.where(emask, choice, neg_inf)
    ei = lax.broadcasted_iota(I32, (N_EXPERTS, tr), 0)
    erank = jnp.zeros((N_EXPERTS, tr), I32)
    for e in range(N_EXPERTS):
        erank = erank + _beats(masked[e:e + 1, :], masked, e, ei).astype(I32)
    sel = erank < TOP_K
    ssel = jnp.where(sel, scores, 0.0)
    denom = jnp.sum(ssel, axis=0, keepdims=True)
    w_ref[...] = ssel / denom * ROUTED_SCALE

    self = sel.astype(F32)
    upper = (lax.broadcasted_iota(I32, (tr, tr), 0) < lax.broadcasted_iota(I32, (tr, tr), 1))
    prefix = jnp.dot(self.astype(BF16), upper.astype(BF16), preferred_element_type=F32)
    carry = carry_ref[...]
    rank = prefix + carry[:, 0:1]
    rank_ref[...] = jnp.where(sel, rank, -1.0).astype(I32)
    carry = carry + jnp.sum(self, axis=1, keepdims=True)
    carry_ref[...] = carry
    cnt_ref[...] = carry


def _route(lt, bias):
    e, t = lt.shape
    tr = min(512, t)
    return pl.pallas_call(
        _route_kernel,
        grid=(t // tr,),
        in_specs=[pl.BlockSpec((e, tr), lambda i: (0, i)),
                  pl.BlockSpec((e, 1), lambda i: (0, 0))],
        out_specs=[pl.BlockSpec((e, tr), lambda i: (0, i)),
                   pl.BlockSpec((e, tr), lambda i: (0, i)),
                   pl.BlockSpec((e, LANES), lambda i: (0, 0))],
        out_shape=[jax.ShapeDtypeStruct((e, t), F32),
                   jax.ShapeDtypeStruct((e, t), I32),
                   jax.ShapeDtypeStruct((e, LANES), F32)],
        scratch_shapes=[pltpu.VMEM((e, LANES), F32)],
        compiler_params=_cparams(("arbitrary",)),
        name="route",
    )(lt, bias.reshape(e, 1))


def _plan_kernel(w_ref, rank_ref, cnt_ref, pos_ref, wc_ref, seg_ref):
    cnt = cnt_ref[...]
    nblk = jnp.floor((cnt + (MOE_BLOCK - 1.0)) * (1.0 / MOE_BLOCK))
    lower = (lax.broadcasted_iota(I32, (N_EXPERTS, N_EXPERTS), 1)
             < lax.broadcasted_iota(I32, (N_EXPERTS, N_EXPERTS), 0)).astype(BF16)
    bstart = jnp.dot(lower, nblk.astype(BF16), preferred_element_type=F32)
    bend = bstart + nblk

    @pl.when(pl.program_id(0) == 0)
    def _():
        seg_ref[...] = jnp.concatenate(
            [bstart[:, 0:1] * MOE_BLOCK, cnt[:, 0:1], nblk[:, 0:1] * MOE_BLOCK, bend[:, 0:1]],
            axis=1).astype(I32)

    rank = rank_ref[...]
    sel = rank >= 0
    pos = bstart[:, 0:1] * MOE_BLOCK + rank.astype(F32)
    slot = jnp.dot(lower, sel.astype(BF16), preferred_element_type=F32)
    w = w_ref[...]
    pos_rows, w_rows = [], []
    for k in range(TOP_K):
        m = sel & (slot == float(k))
        pos_rows.append(jnp.sum(jnp.where(m, pos, 0.0), axis=0, keepdims=True))
        w_rows.append(jnp.sum(jnp.where(m, w, 0.0), axis=0, keepdims=True))
    pos_ref[...] = jnp.concatenate(pos_rows, axis=0).astype(I32)
    wc_ref[...] = jnp.concatenate(w_rows, axis=0)


def _plan(w, rank, cnt):
    e, t = w.shape
    tr = min(1024, t)
    return pl.pallas_call(
        _plan_kernel,
        grid=(t // tr,),
        in_specs=[pl.BlockSpec((e, tr), lambda i: (0, i)),
                  pl.BlockSpec((e, tr), lambda i: (0, i)),
                  pl.BlockSpec((e, LANES), lambda i: (0, 0))],
        out_specs=[pl.BlockSpec((TOP_K, tr), lambda i: (0, i)),
                   pl.BlockSpec((TOP_K, tr), lambda i: (0, i)),
                   pl.BlockSpec((e, 4), lambda i: (0, 0))],
        out_shape=[jax.ShapeDtypeStruct((TOP_K, t), I32),
                   jax.ShapeDtypeStruct((TOP_K, t), F32),
                   jax.ShapeDtypeStruct((e, 4), I32)],
        compiler_params=_cparams(("arbitrary",)),
        name="plan",
    )(w, rank, cnt)


def _row_copy(src, src_row, dst, dst_row, sem):
    return pltpu.make_async_copy(src.at[pl.ds(src_row * ROW_WORDS, ROW_WORDS), :],
                                 dst.at[pl.ds(dst_row * ROW_WORDS, ROW_WORDS), :], sem)


def _dispatch_kernel(pos_ref, seg_ref, hp_ref, xs_ref, zero_ref, sem, zsem):
    tq = pos_ref.shape[0] // TOP_K
    blk_words = MOE_BLOCK * ROW_WORDS
    n_blocks = xs_ref.shape[0] // blk_words

    @pl.when(pl.program_id(0) == 0)
    def _():
        zero_ref[...] = jnp.zeros_like(zero_ref)
        zrow = zero_ref.at[pl.ds(0, ROW_WORDS), :]

        def per_expert(e, n):
            start = seg_ref[e, 0] + seg_ref[e, 1]
            npad = seg_ref[e, 2] - seg_ref[e, 1]

            def fill(r, c):
                pltpu.make_async_copy(
                    zrow, xs_ref.at[pl.ds((start + r) * ROW_WORDS, ROW_WORDS), :], zsem).start()
                return c

            lax.fori_loop(0, npad, fill, 0)
            return n + npad

        n_pad_rows = lax.fori_loop(0, N_EXPERTS, per_expert, 0)

        def drain_row(r, c):
            pltpu.make_async_copy(zrow, xs_ref.at[pl.ds(0, ROW_WORDS), :], zsem).wait()
            return c

        lax.fori_loop(0, n_pad_rows, drain_row, 0)
        n_used = seg_ref[N_EXPERTS - 1, 3]

        def fill_block(b, c):
            pltpu.make_async_copy(
                zero_ref, xs_ref.at[pl.ds(b * blk_words, blk_words), :], zsem).start()
            return c

        lax.fori_loop(n_used, n_blocks, fill_block, 0)

        def drain_block(b, c):
            pltpu.make_async_copy(zero_ref, xs_ref.at[pl.ds(0, blk_words), :], zsem).wait()
            return c

        lax.fori_loop(n_used, n_blocks, drain_block, 0)

    def issue(t, c):
        for k in range(TOP_K):
            _row_copy(hp_ref, t, xs_ref, pos_ref[t * TOP_K + k], sem).start(priority=k % 2)
        return c

    lax.fori_loop(0, tq, issue, 0)
    for k in range(TOP_K):
        pltpu.make_async_copy(hp_ref, xs_ref.at[pl.ds(0, tq * ROW_WORDS), :], sem).wait()


def _dispatch(pos_tm, seg, hp, n_rows):
    t = pos_tm.shape[0] // TOP_K
    tq = min(1024, t)
    return pl.pallas_call(
        _dispatch_kernel,
        grid=(t // tq,),
        in_specs=[pl.BlockSpec((tq * TOP_K,), lambda i: (i,), memory_space=pltpu.SMEM),
                  pl.BlockSpec(seg.shape, lambda i: (0, 0), memory_space=pltpu.SMEM),
                  pl.BlockSpec((tq * ROW_WORDS, LANES), lambda i: (i, 0))],
        out_specs=pl.BlockSpec(memory_space=pl.ANY),
        out_shape=jax.ShapeDtypeStruct((n_rows * ROW_WORDS, LANES), U32),
        scratch_shapes=[pltpu.VMEM((MOE_BLOCK * ROW_WORDS, LANES), U32),
                        pltpu.SemaphoreType.DMA(()), pltpu.SemaphoreType.DMA(())],
        compiler_params=_cparams(("arbitrary",)),
        name="dispatch",
    )(pos_tm, seg, hp)


def _expert_kernel(seg_ref, xs_ref, wg_ref, wu_ref, wd_ref, ys_ref,
                   wgb, wub, wdb, xbuf, ybuf, xsem, ysem):
    e = pl.program_id(0)
    last_step = e == pl.num_programs(0) - 1
    rows = MOE_BLOCK
    blk_words = MOE_BLOCK * ROW_WORDS
    n_blocks = ys_ref.shape[0] // blk_words
    first = lax.div(seg_ref[e * 4], MOE_BLOCK)
    end = seg_ref[e * 4 + 3]
    n_used = seg_ref[(pl.num_programs(0) - 1) * 4 + 3]

    def x_copy(g):
        slot = g & 1
        return pltpu.make_async_copy(xs_ref.at[pl.ds(g * blk_words, blk_words), :],
                                     xbuf.at[slot], xsem.at[slot])

    def y_copy(g):
        slot = g & 1
        return pltpu.make_async_copy(ybuf.at[slot],
                                     ys_ref.at[pl.ds(g * blk_words, blk_words), :], ysem.at[slot])

    @pl.when(jnp.logical_and(e == 0, n_used > 0))
    def _():
        x_copy(0).start()

    @pl.when(end > first)
    def _():
        wgb[...] = wg_ref[...].astype(BF16)
        wub[...] = wu_ref[...].astype(BF16)
        wdb[...] = wd_ref[...].astype(BF16)

        def block(g, c):
            slot = g & 1
            x_copy(g).wait()

            @pl.when(g + 1 < n_used)
            def _():
                x_copy(g + 1).start()

            @pl.when(g >= 2)
            def _():
                y_copy(g - 2).wait()

            xin = xbuf.at[slot]
            his, los = [], []
            for j in range(ROW_WORDS):
                hi, lo = _unpack_pair(xin[pl.ds(j, rows, stride=ROW_WORDS), :])
                his.append(hi.astype(BF16))
                los.append(lo.astype(BF16))
            xb = jnp.concatenate(his + los, axis=1)
            a = jnp.dot(xb, wgb[...], preferred_element_type=F32)
            u = jnp.dot(xb, wub[...], preferred_element_type=F32)
            hid = (_silu(a) * u).astype(BF16)
            out = jnp.dot(hid, wdb[...], preferred_element_type=F32).astype(BF16).astype(F32)
            half = out.shape[1] // 2
            words = _pack_pair(out[:, :half], out[:, half:])
            yout = ybuf.at[slot]
            for j in range(ROW_WORDS):
                yout[pl.ds(j, rows, stride=ROW_WORDS), :] = words[:, j * LANES:(j + 1) * LANES]
            y_copy(g).start()
            return c

        lax.fori_loop(first, end, block, 0)

    @pl.when(last_step)
    def _():
        @pl.when(n_used >= 2)
        def _():
            y_copy(n_used - 2).wait()

        @pl.when(n_used >= 1)
        def _():
            y_copy(n_used - 1).wait()

        ybuf[0] = jnp.zeros(ybuf.shape[1:], ybuf.dtype)

        def fill(g, c):
            pltpu.make_async_copy(ybuf.at[0], ys_ref.at[pl.ds(g * blk_words, blk_words), :],
                                  ysem.at[0]).start()
            return c

        lax.fori_loop(n_used, n_blocks, fill, 0)

        def drain(g, c):
            pltpu.make_async_copy(ybuf.at[0], ys_ref.at[pl.ds(0, blk_words), :], ysem.at[0]).wait()
            return c

        lax.fori_loop(n_used, n_blocks, drain, 0)


def _experts(seg_flat, xs, w_gate, w_up, w_down):
    n_exp, d, f = w_gate.shape
    blk_words = MOE_BLOCK * ROW_WORDS
    grid_spec = pltpu.PrefetchScalarGridSpec(
        num_scalar_prefetch=1,
        grid=(n_exp,),
        in_specs=[
            pl.BlockSpec(memory_space=pl.ANY),
            pl.BlockSpec((None, d, f), lambda e, sg: (e, 0, 0)),
            pl.BlockSpec((None, d, f), lambda e, sg: (e, 0, 0)),
            pl.BlockSpec((None, f, d), lambda e, sg: (e, 0, 0)),
        ],
        out_specs=pl.BlockSpec(memory_space=pl.ANY),
        scratch_shapes=[pltpu.VMEM((d, f), BF16), pltpu.VMEM((d, f), BF16), pltpu.VMEM((f, d), BF16),
                        pltpu.VMEM((2, blk_words, LANES), U32), pltpu.VMEM((2, blk_words, LANES), U32),
                        pltpu.SemaphoreType.DMA((2,)), pltpu.SemaphoreType.DMA((2,))],
    )
    return pl.pallas_call(
        _expert_kernel,
        grid_spec=grid_spec,
        out_shape=jax.ShapeDtypeStruct(xs.shape, U32),
        compiler_params=_cparams(("arbitrary",)),
        name="experts",
    )(seg_flat, xs, w_gate, w_up, w_down)


def _combine_kernel(pos_ref, nxt_ref, ys_ref, wc_ref, h2_ref, x1_ref, sg_ref, su_ref, sd_ref,
                    g2_ref, nf_ref, scf_ref, shf_ref, o_ref, buf, sem):
    i = pl.program_id(0)
    tq = wc_ref.shape[1]
    slot = i & 1

    def gather(p_ref, to_slot):
        def issue(t, c):
            for k in range(TOP_K):
                pltpu.make_async_copy(
                    ys_ref.at[pl.ds(p_ref[t * TOP_K + k] * ROW_WORDS, ROW_WORDS), :],
                    buf.at[to_slot, k, pl.ds(t * ROW_WORDS, ROW_WORDS), :],
                    sem.at[to_slot]).start(priority=k % 2)
            return c

        lax.fori_loop(0, tq, issue, 0)

    @pl.when(i == 0)
    def _():
        gather(pos_ref, 0)

    @pl.when(i + 1 < pl.num_programs(0))
    def _():
        gather(nxt_ref, 1 - slot)

    h2 = h2_ref[...]
    a = jnp.dot(h2, sg_ref[...], preferred_element_type=F32)
    u = jnp.dot(h2, su_ref[...], preferred_element_type=F32)
    shared = jnp.dot((_silu(a) * u).astype(BF16), sd_ref[...], preferred_element_type=F32)

    for k in range(TOP_K):
        pltpu.make_async_copy(ys_ref.at[pl.ds(0, tq * ROW_WORDS), :], buf.at[slot, k],
                              sem.at[slot]).wait()

    wt = jnp.transpose(wc_ref[...])
    his, los = [], []
    for j in range(ROW_WORDS):
        acc_hi = jnp.zeros((tq, LANES), F32)
        acc_lo = jnp.zeros((tq, LANES), F32)
        for k in range(TOP_K):
            hi, lo = _unpack_pair(buf.at[slot, k][pl.ds(j, tq, stride=ROW_WORDS), :])
            wk = wt[:, k:k + 1]
            acc_hi = acc_hi + wk * hi
            acc_lo = acc_lo + wk * lo
        his.append(acc_hi)
        los.append(acc_lo)
    y = jnp.concatenate(his + los, axis=1) + shared
    x2 = x1_ref[...] + g2_ref[0] * y
    o_ref[...] = _norm_mod(x2, nf_ref[...], scf_ref[0], shf_ref[0])


def _combine(pos_tm, ys, w_c, h2, x1, sg, su, sd, g2, nf, scf, shf, seq):
    t, d = x1.shape
    tq = 128
    n_tiles = t // tq
    per_batch = seq // tq
    mod = pl.BlockSpec((1, 1, d), lambda i: (i // per_batch, 0, 0))
    full = lambda a: pl.BlockSpec(a.shape, lambda i: (0, 0))
    return pl.pallas_call(
        _combine_kernel,
        grid=(n_tiles,),
        in_specs=[
            pl.BlockSpec((tq * TOP_K,), lambda i: (i,), memory_space=pltpu.SMEM),
            pl.BlockSpec((tq * TOP_K,), lambda i: (jnp.minimum(i + 1, n_tiles - 1),),
                         memory_space=pltpu.SMEM),
            pl.BlockSpec(memory_space=pl.ANY),
            pl.BlockSpec((TOP_K, tq), lambda i: (0, i)),
            pl.BlockSpec((tq, d), lambda i: (i, 0)),
            pl.BlockSpec((tq, d), lambda i: (i, 0)),
            full(sg), full(su), full(sd),
            mod,
            pl.BlockSpec((1, d), lambda i: (0, 0)),
            mod, mod,
        ],
        out_specs=pl.BlockSpec((tq, d), lambda i: (i, 0)),
        out_shape=jax.ShapeDtypeStruct((t, d), F32),
        scratch_shapes=[pltpu.VMEM((2, TOP_K, tq * ROW_WORDS, LANES), U32),
                        pltpu.SemaphoreType.DMA((2,))],
        compiler_params=_cparams(("arbitrary",)),
        name="combine",
    )(pos_tm, pos_tm, ys, w_c, h2, x1, sg, su, sd, g2, nf.reshape(1, d), scf, shf)


def _rope_tables(seq):
    pos = jnp.arange(seq, dtype=F32)
    inv_freq = ROPE_BASE ** (-jnp.arange(0, HEAD_DIM, 2, dtype=F32) / HEAD_DIM)
    ang = pos[:, None] * inv_freq[None, :]
    cos, sin = jnp.cos(ang), jnp.sin(ang)
    return jnp.concatenate([cos, cos], axis=1), jnp.concatenate([-sin, sin], axis=1)


def kernel(x, c, w_ada, b_ada, norm1_g, w_in, ret_gn_g, w_out, norm2_g, w_router, router_bias,
           w_gate, w_up, w_down, ws_gate, ws_up, ws_down, w_ada_final, b_ada_final, norm_f_g):
    bsz, seq, d = x.shape
    t = bsz * seq
    depth = w_ada.shape[0]
    n_heads = d // (2 * HEAD_DIM)
    sb_w = n_heads * HEAD_DIM
    cos2, sin2 = _rope_tables(seq)
    c_pad = jnp.pad(c, ((0, 8 - bsz % 8), (0, 0)))
    n_rows = t * TOP_K + N_EXPERTS * MOE_BLOCK
    as_mod = lambda v: v.reshape(bsz, 1, d)

    x2 = x.reshape(t, d)
    for l in range(depth):
        mod = _ada(c_pad, w_ada[l], b_ada[l])[:bsz]
        sh1, sc1, g1, sh2, sc2, g2 = [as_mod(m) for m in jnp.split(mod, 6, axis=-1)]
        u = _inproj(x2, norm1_g[l], sc1, sh1, w_in[l].astype(BF16), seq)
        u3 = u.reshape(bsz, seq, -1)
        sb = _sb_attention(u3, n_heads)
        rt = _retention(u3, cos2, sin2, ret_gn_g[l], n_heads, 3 * n_heads)
        wo = w_out[l].astype(BF16)
        x1, h2, hp, lt = _outproj(sb.reshape(t, sb_w), rt.reshape(t, -1), wo[:sb_w], wo[sb_w:], x2,
                                  g1, norm2_g[l], sc2, sh2, jnp.transpose(w_router[l]).astype(BF16), seq)
        w_e, rank_e, cnt = _route(lt, router_bias[l])
        pos_c, w_c, seg = _plan(w_e, rank_e, cnt)
        pos_tm = jnp.transpose(pos_c).reshape(-1)
        xs = _dispatch(pos_tm, seg, hp, n_rows)
        ys = _experts(seg.reshape(-1), xs, w_gate[l], w_up[l], w_down[l])
        if l + 1 < depth:
            raise NotImplementedError("only the final layer fuses the output norm")
        modf = _ada(c_pad, w_ada_final, b_ada_final)[:bsz]
        shf, scf = [as_mod(m) for m in jnp.split(modf, 2, axis=-1)]
        x2 = _combine(pos_tm, ys, w_c, h2, x1, ws_gate[l].astype(BF16), ws_up[l].astype(BF16),
                      ws_down[l].astype(BF16), g2, norm_f_g, scf, shf, seq)
    return x2.reshape(bsz, seq, d)
```

```python
import functools

import jax
import jax.numpy as jnp
from jax import lax
from jax.experimental import pallas as pl
from jax.experimental.pallas import tpu as pltpu

F32 = jnp.float32
BF16 = jnp.bfloat16
I32 = jnp.int32
U32 = jnp.uint32

HEAD_DIM = 128
N_EXPERTS = 64
TOP_K = 8
N_GROUPS = 8
TOPK_GROUP = 4
ROUTED_SCALE = 2.5
EPS = 1e-6
ROPE_BASE = 10000.0

LANES = 128
ROW_WORDS = 8
MOE_BLOCK = 256
V7X_VMEM_LIMIT = 56 * 1024 * 1024
SB_SKIP_BOUND = -115.0


def _cparams(sem, vmem=V7X_VMEM_LIMIT):
    return pltpu.CompilerParams(dimension_semantics=sem, vmem_limit_bytes=vmem)


def _silu(v):
    return v * jax.nn.sigmoid(v)


def _pack_pair(hi_f32, lo_f32):
    hw = pltpu.bitcast(hi_f32, U32) & jnp.uint32(0xFFFF0000)
    lw = pltpu.bitcast(lo_f32, U32) >> 16
    return hw | lw


def _unpack_pair(w):
    hi = pltpu.bitcast(w & jnp.uint32(0xFFFF0000), F32)
    lo = pltpu.bitcast(w << 16, F32)
    return hi, lo


def _ada_kernel(c_ref, w_ref, b_ref, o_ref):
    cs = _silu(c_ref[...]).astype(BF16)
    o_ref[...] = jnp.dot(cs, w_ref[...].astype(BF16), preferred_element_type=F32) + b_ref[...]


def _ada(c_pad, w, b):
    rows, d = c_pad.shape
    n = w.shape[1]
    tn = 1024
    return pl.pallas_call(
        _ada_kernel,
        grid=(n // tn,),
        in_specs=[
            pl.BlockSpec((rows, d), lambda j: (0, 0)),
            pl.BlockSpec((d, tn), lambda j: (0, j)),
            pl.BlockSpec((1, tn), lambda j: (0, j)),
        ],
        out_specs=pl.BlockSpec((rows, tn), lambda j: (0, j)),
        out_shape=jax.ShapeDtypeStruct((rows, n), F32),
        compiler_params=_cparams(("arbitrary",)),
        name="ada",
    )(c_pad, w, b.reshape(1, n))


def _norm_mod(x, g, sc, sh):
    ms = jnp.mean(x * x, axis=-1, keepdims=True)
    return (x * lax.rsqrt(ms + EPS) * g) * (1.0 + sc) + sh


def _inproj_kernel(x_ref, g_ref, sc_ref, sh_ref, w_ref, o_ref, h_ref, *, chunk):
    first = pl.program_id(1) == 0

    @pl.when(first)
    def _():
        g = g_ref[...]
        sc = sc_ref[0]
        sh = sh_ref[0]
        for r in range(x_ref.shape[0] // chunk):
            rows = pl.ds(r * chunk, chunk)
            h = _norm_mod(x_ref[rows, :], g, sc, sh).astype(BF16)
            h_ref[rows, :] = h
            o_ref[rows, :] = jnp.dot(h, w_ref[...], preferred_element_type=F32).astype(BF16)

    @pl.when(jnp.logical_not(first))
    def _():
        o_ref[...] = jnp.dot(h_ref[...], w_ref[...], preferred_element_type=F32).astype(BF16)


def _inproj(x2, g, sc, sh, w_bf, seq):
    t, d = x2.shape
    n = w_bf.shape[1]
    tm = 1024
    tn = n // 4 if (n // 4) % (2 * LANES) == 0 else 1024
    per_batch = seq // tm
    return pl.pallas_call(
        functools.partial(_inproj_kernel, chunk=256),
        grid=(t // tm, n // tn),
        in_specs=[
            pl.BlockSpec((tm, d), lambda i, j: (i, 0)),
            pl.BlockSpec((1, d), lambda i, j: (0, 0)),
            pl.BlockSpec((1, 1, d), lambda i, j: (i // per_batch, 0, 0)),
            pl.BlockSpec((1, 1, d), lambda i, j: (i // per_batch, 0, 0)),
            pl.BlockSpec((d, tn), lambda i, j: (0, j)),
        ],
        out_specs=pl.BlockSpec((tm, tn), lambda i, j: (i, j)),
        out_shape=jax.ShapeDtypeStruct((t, n), BF16),
        scratch_shapes=[pltpu.VMEM((tm, d), BF16)],
        compiler_params=_cparams(("arbitrary", "arbitrary")),
        name="inproj",
    )(x2, g.reshape(1, d), sc, sh, w_bf)


def _sb_kernel(q_ref, k_ref, v_ref, o_ref, *, group):
    seq = q_ref.shape[0]
    blk = LANES
    scale = HEAD_DIM ** -0.5
    row = lax.broadcasted_iota(I32, (blk, blk), 0)
    col = lax.broadcasted_iota(I32, (blk, blk), 1)
    strict = col < row
    tr = lax.broadcasted_iota(I32, (2 * blk, 2 * blk), 0) % blk
    tc = lax.broadcasted_iota(I32, (2 * blk, 2 * blk), 1)
    tri = jnp.where((tc >= blk) | (tr > tc), 1.0, 0.0).astype(BF16)

    def key_rows(j):
        return pl.ds(pl.multiple_of(j * blk, blk), blk)

    def logits(qb, kb):
        z = lax.dot_general(qb, kb, (((1,), (1,)), ((), ())), preferred_element_type=F32) * scale
        sp = jnp.maximum(z, 0.0) + jnp.log(1.0 + jnp.exp(-jnp.abs(z)))
        return -sp, z - sp

    def suffix_sums(log_surv):
        hi = log_surv.astype(BF16)
        lo = (log_surv - hi.astype(F32)).astype(BF16)
        r = jnp.dot(jnp.concatenate([hi, lo], axis=1), tri, preferred_element_type=F32)
        return r[:, :blk], r[:, blk:]

    def tile(qb, j, g, carry, acc):
        log_surv, log_beta = logits(qb, k_ref[key_rows(j), g * blk:(g + 1) * blk])
        after, total = suffix_sums(log_surv)
        a = jnp.exp(log_beta + after + carry)
        acc = acc + jnp.dot(a.astype(BF16), v_ref[key_rows(j), g * blk:(g + 1) * blk],
                            preferred_element_type=F32)
        return carry + total, acc

    def group_start(qbs, i):
        n_near = 3
        js = [i, jnp.maximum(i - 1, 0), jnp.maximum(i - 2, 0)]
        ivec = jnp.full((blk, blk), i, I32)
        masks = [strict, ivec >= 1, ivec >= 2]
        log_betas, parts = [], []
        for g in range(group):
            cols = slice(g * blk, (g + 1) * blk)
            kcat = jnp.concatenate([k_ref[key_rows(j), cols] for j in js], axis=0)
            log_surv, log_beta = logits(qbs[g], kcat)
            log_betas.append(log_beta)
            for m in range(n_near):
                parts.append(jnp.where(masks[m], log_surv[:, m * blk:(m + 1) * blk], 0.0))
        after, total = suffix_sums(jnp.concatenate(parts, axis=0))
        out = []
        for g in range(group):
            cols = slice(g * blk, (g + 1) * blk)
            carry = jnp.zeros((blk, blk), F32)
            weights = []
            for m in range(n_near):
                rows = slice((g * n_near + m) * blk, (g * n_near + m + 1) * blk)
                a = jnp.exp(log_betas[g][:, m * blk:(m + 1) * blk] + after[rows] + carry)
                weights.append(jnp.where(masks[m], a, 0.0).astype(BF16))
                carry = carry + total[rows]
            vcat = jnp.concatenate([v_ref[key_rows(j), cols] for j in js], axis=0)
            acc = jnp.dot(jnp.concatenate(weights, axis=1), vcat, preferred_element_type=F32)
            out.append((carry, acc))
        return out

    def q_block(i, c):
        rows = pl.ds(pl.multiple_of(i * blk, blk), blk)
        qbs = [q_ref[rows, g * blk:(g + 1) * blk] for g in range(group)]
        first = group_start(qbs, i)
        carries = tuple(f[0] for f in first)
        accs = tuple(f[1] for f in first)

        def cond(st):
            j, crs, _ = st
            top = functools.reduce(jnp.maximum, crs)
            return jnp.logical_and(j >= 0, jnp.max(top) > SB_SKIP_BOUND)

        def body(st):
            j, crs, acs = st
            nxt = [tile(qbs[g], j, g, crs[g], acs[g]) for g in range(group)]
            return j - 1, tuple(n[0] for n in nxt), tuple(n[1] for n in nxt)

        _, _, accs = lax.while_loop(cond, body, (i - 3, carries, accs))
        for g in range(group):
            o_ref[rows, g * blk:(g + 1) * blk] = accs[g].astype(BF16)
        return c

    lax.fori_loop(0, seq // blk, q_block, 0)


SB_HEAD_GROUP = 4


def _sb_attention(u3, n_heads):
    b, s, _ = u3.shape
    gw = SB_HEAD_GROUP * HEAD_DIM
    n_groups = n_heads // SB_HEAD_GROUP
    spec = lambda off: pl.BlockSpec((None, s, gw), lambda bi, h: (bi, 0, off + h))
    return pl.pallas_call(
        functools.partial(_sb_kernel, group=SB_HEAD_GROUP),
        grid=(b, n_groups),
        in_specs=[spec(0), spec(n_groups), spec(2 * n_groups)],
        out_specs=pl.BlockSpec((None, s, gw), lambda bi, h: (bi, 0, h)),
        out_shape=jax.ShapeDtypeStruct((b, s, n_heads * HEAD_DIM), BF16),
        compiler_params=_cparams(("arbitrary", "arbitrary")),
        name="sb_attn",
    )(u3, u3, u3)


def _ret_kernel(q_ref, k_ref, v_ref, g_ref, cos_ref, sin_ref, gn_ref, o_ref, *, chunk, unroll):
    seq = q_ref.shape[0]
    dh = HEAD_DIM
    head = pl.program_id(1)

    def log_gamma(shape):
        hv = jnp.full(shape, head, I32).astype(F32)
        return jnp.log(1.0 - jnp.exp2(-5.0 - hv))

    diff = (lax.broadcasted_iota(I32, (chunk, chunk), 0)
            - lax.broadcasted_iota(I32, (chunk, chunk), 1)).astype(F32)
    lower = diff >= 0.0
    dmat = jnp.where(lower, jnp.exp(jnp.where(lower, diff, 0.0) * log_gamma((chunk, chunk))), 0.0)
    pos = lax.broadcasted_iota(I32, (chunk, dh), 0).astype(F32)
    k_decay = jnp.exp((chunk - 1.0 - pos) * log_gamma((chunk, dh)))
    q_decay = jnp.exp((pos + 1.0) * log_gamma((chunk, dh)))
    chunk_decay = jnp.exp(chunk * log_gamma((dh, dh)))
    gn = gn_ref[...]
    scale = dh ** -0.5

    def rope(t, cs, sn):
        return t * cs + pltpu.roll(t, dh // 2, 1) * sn

    def body(n, state):
        rows = pl.ds(pl.multiple_of(n * chunk, chunk), chunk)
        cs = cos_ref[rows, :]
        sn = sin_ref[rows, :]
        qc = rope(q_ref[rows, :].astype(F32), cs, sn)
        kc = rope(k_ref[rows, :].astype(F32), cs, sn) * scale
        vc = v_ref[rows, :]
        scores = lax.dot_general(qc.astype(BF16), kc.astype(BF16), (((1,), (1,)), ((), ())),
                                 preferred_element_type=F32) * dmat
        intra = jnp.dot(scores.astype(BF16), vc, preferred_element_type=F32)
        cross = jnp.dot((qc * q_decay).astype(BF16), state.astype(BF16), preferred_element_type=F32)
        out = intra + cross
        kd_t = jnp.transpose(kc * k_decay).astype(BF16)
        state = state * chunk_decay + jnp.dot(kd_t, vc, preferred_element_type=F32)
        ms = jnp.mean(out * out, axis=-1, keepdims=True)
        y = out * lax.rsqrt(ms + EPS) * gn * _silu(g_ref[rows, :].astype(F32))
        o_ref[rows, :] = y.astype(BF16)
        return state

    lax.fori_loop(0, seq // chunk, body, jnp.zeros((dh, dh), F32), unroll=unroll)


RET_CHUNK = 256
RET_UNROLL = 2


def _retention(u3, cos2, sin2, gn, n_heads, col0):
    b, s, _ = u3.shape
    spec = lambda off: pl.BlockSpec((None, s, HEAD_DIM), lambda bi, h: (bi, 0, col0 + off + h))
    return pl.pallas_call(
        functools.partial(_ret_kernel, chunk=RET_CHUNK, unroll=RET_UNROLL),
        grid=(b, n_heads),
        in_specs=[spec(0), spec(n_heads), spec(2 * n_heads), spec(3 * n_heads),
                  pl.BlockSpec((s, HEAD_DIM), lambda bi, h: (0, 0)),
                  pl.BlockSpec((s, HEAD_DIM), lambda bi, h: (0, 0)),
                  pl.BlockSpec((1, HEAD_DIM), lambda bi, h: (0, h))],
        out_specs=pl.BlockSpec((None, s, HEAD_DIM), lambda bi, h: (bi, 0, h)),
        out_shape=jax.ShapeDtypeStruct((b, s, n_heads * HEAD_DIM), BF16),
        compiler_params=_cparams(("arbitrary", "arbitrary")),
        name="retention",
    )(u3, u3, u3, u3, cos2, sin2, gn.reshape(1, -1))


def _outproj_kernel(sb_ref, rt_ref, wa_ref, wb_ref, x_ref, g1_ref, n2_ref, sc_ref, sh_ref, wr_ref,
                    x1_ref, h2_ref, hp_ref, lt_ref, *, chunk):
    d = x_ref.shape[1]
    half = d // 2
    g1 = g1_ref[0]
    n2 = n2_ref[...]
    sc = sc_ref[0]
    sh = sh_ref[0]
    for r in range(x_ref.shape[0] // chunk):
        rows = pl.ds(r * chunk, chunk)
        o = jnp.dot(sb_ref[rows, :], wa_ref[...], preferred_element_type=F32)
        o = o + jnp.dot(rt_ref[rows, :], wb_ref[...], preferred_element_type=F32)
        x1 = x_ref[rows, :] + g1 * o
        x1_ref[rows, :] = x1
        h2 = _norm_mod(x1, n2, sc, sh).astype(BF16)
        h2_ref[rows, :] = h2
        lt_ref[:, rows] = lax.dot_general(wr_ref[...], h2, (((1,), (1,)), ((), ())),
                                          preferred_element_type=F32)
        words = _pack_pair(h2[:, :half].astype(F32), h2[:, half:].astype(F32))
        for j in range(ROW_WORDS):
            hp_ref[pl.ds(r * chunk * ROW_WORDS + j, chunk, stride=ROW_WORDS), :] = (
                words[:, j * LANES:(j + 1) * LANES])


def _outproj(sb2, rt2, wa, wb, x2, g1, n2, sc2, sh2, wr_t, seq):
    t, d = x2.shape
    tm = 512
    per_batch = seq // tm
    mod = pl.BlockSpec((1, 1, d), lambda i: (i // per_batch, 0, 0))
    return pl.pallas_call(
        functools.partial(_outproj_kernel, chunk=512),
        grid=(t // tm,),
        in_specs=[
            pl.BlockSpec((tm, sb2.shape[1]), lambda i: (i, 0)),
            pl.BlockSpec((tm, rt2.shape[1]), lambda i: (i, 0)),
            pl.BlockSpec(wa.shape, lambda i: (0, 0)),
            pl.BlockSpec(wb.shape, lambda i: (0, 0)),
            pl.BlockSpec((tm, d), lambda i: (i, 0)),
            mod,
            pl.BlockSpec((1, d), lambda i: (0, 0)),
            mod, mod,
            pl.BlockSpec(wr_t.shape, lambda i: (0, 0)),
        ],
        out_specs=[
            pl.BlockSpec((tm, d), lambda i: (i, 0)),
            pl.BlockSpec((tm, d), lambda i: (i, 0)),
            pl.BlockSpec((tm * ROW_WORDS, LANES), lambda i: (i, 0)),
            pl.BlockSpec((N_EXPERTS, tm), lambda i: (0, i)),
        ],
        out_shape=[
            jax.ShapeDtypeStruct((t, d), F32),
            jax.ShapeDtypeStruct((t, d), BF16),
            jax.ShapeDtypeStruct((t * ROW_WORDS, LANES), U32),
            jax.ShapeDtypeStruct((N_EXPERTS, t), F32),
        ],
        compiler_params=_cparams(("arbitrary",)),
        name="outproj",
    )(sb2, rt2, wa, wb, x2, g1, n2.reshape(1, d), sc2, sh2, wr_t)


def _beats(row, allv, row_idx, idx):
    return (row > allv) | ((row == allv) & (row_idx < idx))


def _route_kernel(lt_ref, bias_ref, w_ref, rank_ref, cnt_ref, carry_ref):
    tr = lt_ref.shape[1]
    gsz = N_EXPERTS // N_GROUPS
    neg_inf = jnp.float32(-jnp.inf)

    @pl.when(pl.program_id(0) == 0)
    def _():
        carry_ref[...] = jnp.zeros_like(carry_ref)

    scores = jax.nn.sigmoid(lt_ref[...])
    choice = scores + bias_ref[...]
    sub = lax.broadcasted_iota(I32, (gsz, tr), 0)
    gs_rows = []
    for g in range(N_GROUPS):
        cg = choice[g * gsz:(g + 1) * gsz, :]
        m1 = jnp.max(cg, axis=0, keepdims=True)
        first = jnp.min(jnp.where(cg == m1, sub, gsz), axis=0, keepdims=True)
        m2 = jnp.max(jnp.where(sub == first, neg_inf, cg), axis=0, keepdims=True)
        gs_rows.append(m1 + m2)
    gs = jnp.concatenate(gs_rows, axis=0)
    gi = lax.broadcasted_iota(I32, (N_GROUPS, tr), 0)
    grank = jnp.zeros((N_GROUPS, tr), I32)
    for g in range(N_GROUPS):
        grank = grank + _beats(gs[g:g + 1, :], gs, g, gi).astype(I32)
    gmask = grank < TOPK_GROUP
    emask = jnp.concatenate(
        [jnp.broadcast_to(gmask[g:g + 1, :], (gsz, tr)) for g in range(N_GROUPS)], axis=0)
    masked = jnp.where(emask, choice, neg_inf)
    ei = lax.broadcasted_iota(I32, (N_EXPERTS, tr), 0)
    erank = jnp.zeros((N_EXPERTS, tr), I32)
    for e in range(N_EXPERTS):
        erank = erank + _beats(masked[e:e + 1, :], masked, e, ei).astype(I32)
    sel = erank < TOP_K
    ssel = jnp.where(sel, scores, 0.0)
    denom = jnp.sum(ssel, axis=0, keepdims=True)
    w_ref[...] = ssel / denom * ROUTED_SCALE

    self = sel.astype(F32)
    upper = (lax.broadcasted_iota(I32, (tr, tr), 0) < lax.broadcasted_iota(I32, (tr, tr), 1))
    prefix = jnp.dot(self.astype(BF16), upper.astype(BF16), preferred_element_type=F32)
    carry = carry_ref[...]
    rank = prefix + carry[:, 0:1]
    rank_ref[...] = jnp.where(sel, rank, -1.0).astype(I32)
    carry = carry + jnp.sum(self, axis=1, keepdims=True)
    carry_ref[...] = carry
    cnt_ref[...] = carry


def _route(lt, bias):
    e, t = lt.shape
    tr = min(512, t)
    return pl.pallas_call(
        _route_kernel,
        grid=(t // tr,),
        in_specs=[pl.BlockSpec((e, tr), lambda i: (0, i)),
                  pl.BlockSpec((e, 1), lambda i: (0, 0))],
        out_specs=[pl.BlockSpec((e, tr), lambda i: (0, i)),
                   pl.BlockSpec((e, tr), lambda i: (0, i)),
                   pl.BlockSpec((e, LANES), lambda i: (0, 0))],
        out_shape=[jax.ShapeDtypeStruct((e, t), F32),
                   jax.ShapeDtypeStruct((e, t), I32),
                   jax.ShapeDtypeStruct((e, LANES), F32)],
        scratch_shapes=[pltpu.VMEM((e, LANES), F32)],
        compiler_params=_cparams(("arbitrary",)),
        name="route",
    )(lt, bias.reshape(e, 1))


def _plan_kernel(w_ref, rank_ref, cnt_ref, pos_ref, wc_ref, seg_ref):
    cnt = cnt_ref[...]
    nblk = jnp.floor((cnt + (MOE_BLOCK - 1.0)) * (1.0 / MOE_BLOCK))
    lower = (lax.broadcasted_iota(I32, (N_EXPERTS, N_EXPERTS), 1)
             < lax.broadcasted_iota(I32, (N_EXPERTS, N_EXPERTS), 0)).astype(BF16)
    bstart = jnp.dot(lower, nblk.astype(BF16), preferred_element_type=F32)
    bend = bstart + nblk

    @pl.when(pl.program_id(0) == 0)
    def _():
        seg_ref[...] = jnp.concatenate(
            [bstart[:, 0:1] * MOE_BLOCK, cnt[:, 0:1], nblk[:, 0:1] * MOE_BLOCK, bend[:, 0:1]],
            axis=1).astype(I32)

    rank = rank_ref[...]
    sel = rank >= 0
    pos = bstart[:, 0:1] * MOE_BLOCK + rank.astype(F32)
    slot = jnp.dot(lower, sel.astype(BF16), preferred_element_type=F32)
    w = w_ref[...]
    pos_rows, w_rows = [], []
    for k in range(TOP_K):
        m = sel & (slot == float(k))
        pos_rows.append(jnp.sum(jnp.where(m, pos, 0.0), axis=0, keepdims=True))
        w_rows.append(jnp.sum(jnp.where(m, w, 0.0), axis=0, keepdims=True))
    pos_ref[...] = jnp.concatenate(pos_rows, axis=0).astype(I32)
    wc_ref[...] = jnp.concatenate(w_rows, axis=0)


def _plan(w, rank, cnt):
    e, t = w.shape
    tr = min(1024, t)
    return pl.pallas_call(
        _plan_kernel,
        grid=(t // tr,),
        in_specs=[pl.BlockSpec((e, tr), lambda i: (0, i)),
                  pl.BlockSpec((e, tr), lambda i: (0, i)),
                  pl.BlockSpec((e, LANES), lambda i: (0, 0))],
        out_specs=[pl.BlockSpec((TOP_K, tr), lambda i: (0, i)),
                   pl.BlockSpec((TOP_K, tr), lambda i: (0, i)),
                   pl.BlockSpec((e, 4), lambda i: (0, 0))],
        out_shape=[jax.ShapeDtypeStruct((TOP_K, t), I32),
                   jax.ShapeDtypeStruct((TOP_K, t), F32),
                   jax.ShapeDtypeStruct((e, 4), I32)],
        compiler_params=_cparams(("arbitrary",)),
        name="plan",
    )(w, rank, cnt)


def _row_copy(src, src_row, dst, dst_row, sem):
    return pltpu.make_async_copy(src.at[pl.ds(src_row * ROW_WORDS, ROW_WORDS), :],
                                 dst.at[pl.ds(dst_row * ROW_WORDS, ROW_WORDS), :], sem)


def _dispatch_kernel(pos_ref, seg_ref, hp_ref, xs_ref, zero_ref, sem, zsem):
    tq = pos_ref.shape[0] // TOP_K
    blk_words = MOE_BLOCK * ROW_WORDS
    n_blocks = xs_ref.shape[0] // blk_words

    @pl.when(pl.program_id(0) == 0)
    def _():
        zero_ref[...] = jnp.zeros_like(zero_ref)
        zrow = zero_ref.at[pl.ds(0, ROW_WORDS), :]

        def per_expert(e, n):
            start = seg_ref[e, 0] + seg_ref[e, 1]
            npad = seg_ref[e, 2] - seg_ref[e, 1]

            def fill(r, c):
                pltpu.make_async_copy(
                    zrow, xs_ref.at[pl.ds((start + r) * ROW_WORDS, ROW_WORDS), :], zsem).start()
                return c

            lax.fori_loop(0, npad, fill, 0)
            return n + npad

        n_pad_rows = lax.fori_loop(0, N_EXPERTS, per_expert, 0)

        def drain_row(r, c):
            pltpu.make_async_copy(zrow, xs_ref.at[pl.ds(0, ROW_WORDS), :], zsem).wait()
            return c

        lax.fori_loop(0, n_pad_rows, drain_row, 0)
        n_used = seg_ref[N_EXPERTS - 1, 3]

        def fill_block(b, c):
            pltpu.make_async_copy(
                zero_ref, xs_ref.at[pl.ds(b * blk_words, blk_words), :], zsem).start()
            return c

        lax.fori_loop(n_used, n_blocks, fill_block, 0)

        def drain_block(b, c):
            pltpu.make_async_copy(zero_ref, xs_ref.at[pl.ds(0, blk_words), :], zsem).wait()
            return c

        lax.fori_loop(n_used, n_blocks, drain_block, 0)

    def issue(t, c):
        for k in range(TOP_K):
            _row_copy(hp_ref, t, xs_ref, pos_ref[t * TOP_K + k], sem).start(priority=k % 2)
        return c

    lax.fori_loop(0, tq, issue, 0)
    for k in range(TOP_K):
        pltpu.make_async_copy(hp_ref, xs_ref.at[pl.ds(0, tq * ROW_WORDS), :], sem).wait()


def _dispatch(pos_tm, seg, hp, n_rows):
    t = pos_tm.shape[0] // TOP_K
    tq = min(1024, t)
    return pl.pallas_call(
        _dispatch_kernel,
        grid=(t // tq,),
        in_specs=[pl.BlockSpec((tq * TOP_K,), lambda i: (i,), memory_space=pltpu.SMEM),
                  pl.BlockSpec(seg.shape, lambda i: (0, 0), memory_space=pltpu.SMEM),
                  pl.BlockSpec((tq * ROW_WORDS, LANES), lambda i: (i, 0))],
        out_specs=pl.BlockSpec(memory_space=pl.ANY),
        out_shape=jax.ShapeDtypeStruct((n_rows * ROW_WORDS, LANES), U32),
        scratch_shapes=[pltpu.VMEM((MOE_BLOCK * ROW_WORDS, LANES), U32),
                        pltpu.SemaphoreType.DMA(()), pltpu.SemaphoreType.DMA(())],
        compiler_params=_cparams(("arbitrary",)),
        name="dispatch",
    )(pos_tm, seg, hp)


def _expert_kernel(seg_ref, xs_ref, wg_ref, wu_ref, wd_ref, ys_ref,
                   wgb, wub, wdb, ring, one, xsem, ysem, tsem):
    e = pl.program_id(0)
    blk_words = MOE_BLOCK * ROW_WORDS
    n_blocks = ys_ref.shape[0] // blk_words
    first = lax.div(seg_ref[e * 4], MOE_BLOCK)
    end = seg_ref[e * 4 + 3]
    n_used = seg_ref[(pl.num_programs(0) - 1) * 4 + 3]
    n_pairs = lax.div(end - first, 2)
    odd = (end - first) & 1
    tail = end - 1

    def pair_rows(p):
        return pl.ds((first + 2 * p) * blk_words, 2 * blk_words)

    def pair_in(p):
        return pltpu.make_async_copy(xs_ref.at[pair_rows(p), :], ring.at[p & 1], xsem.at[p & 1])

    def pair_out(p):
        return pltpu.make_async_copy(ring.at[p & 1], ys_ref.at[pair_rows(p), :], ysem.at[p & 1])

    tail_in = pltpu.make_async_copy(xs_ref.at[pl.ds(tail * blk_words, blk_words), :], one, tsem.at[0])
    tail_out = pltpu.make_async_copy(one, ys_ref.at[pl.ds(tail * blk_words, blk_words), :], tsem.at[1])

    def mlp_in_place(buf, rows):
        his, los = [], []
        for j in range(ROW_WORDS):
            hi, lo = _unpack_pair(buf[pl.ds(j, rows, stride=ROW_WORDS), :])
            his.append(hi.astype(BF16))
            los.append(lo.astype(BF16))
        xb = jnp.concatenate(his + los, axis=1)
        a = jnp.dot(xb, wgb[...], preferred_element_type=F32)
        u = jnp.dot(xb, wub[...], preferred_element_type=F32)
        hid = (_silu(a) * u).astype(BF16)
        out = jnp.dot(hid, wdb[...], preferred_element_type=F32).astype(BF16).astype(F32)
        half = out.shape[1] // 2
        words = _pack_pair(out[:, :half], out[:, half:])
        for j in range(ROW_WORDS):
            buf[pl.ds(j, rows, stride=ROW_WORDS), :] = words[:, j * LANES:(j + 1) * LANES]

    @pl.when(end > first)
    def _():
        @pl.when(n_pairs > 0)
        def _():
            pair_in(0).start()

        @pl.when(odd == 1)
        def _():
            tail_in.start()

        wgb[...] = wg_ref[...].astype(BF16)
        wub[...] = wu_ref[...].astype(BF16)
        wdb[...] = wd_ref[...].astype(BF16)

        def pair(p, c):
            pair_in(p).wait()

            @pl.when(p + 1 < n_pairs)
            def _():
                @pl.when(p >= 1)
                def _():
                    pair_out(p - 1).wait()

                pair_in(p + 1).start()

            mlp_in_place(ring.at[p & 1], 2 * MOE_BLOCK)
            pair_out(p).start()
            return c

        lax.fori_loop(0, n_pairs, pair, 0)

        @pl.when(odd == 1)
        def _():
            tail_in.wait()
            mlp_in_place(one, MOE_BLOCK)
            tail_out.start()

        @pl.when(n_pairs >= 2)
        def _():
            pair_out(n_pairs - 2).wait()

        @pl.when(n_pairs >= 1)
        def _():
            pair_out(n_pairs - 1).wait()

        @pl.when(odd == 1)
        def _():
            tail_out.wait()

    @pl.when(e == pl.num_programs(0) - 1)
    def _():
        one[...] = jnp.zeros_like(one)

        def fill(g, c):
            pltpu.make_async_copy(one, ys_ref.at[pl.ds(g * blk_words, blk_words), :],
                                  tsem.at[1]).start()
            return c

        lax.fori_loop(n_used, n_blocks, fill, 0)

        def drain(g, c):
            pltpu.make_async_copy(one, ys_ref.at[pl.ds(0, blk_words), :], tsem.at[1]).wait()
            return c

        lax.fori_loop(n_used, n_blocks, drain, 0)


def _experts(seg_flat, xs, w_gate, w_up, w_down):
    n_exp, d, f = w_gate.shape
    blk_words = MOE_BLOCK * ROW_WORDS
    grid_spec = pltpu.PrefetchScalarGridSpec(
        num_scalar_prefetch=1,
        grid=(n_exp,),
        in_specs=[
            pl.BlockSpec(memory_space=pl.ANY),
            pl.BlockSpec((None, d, f), lambda e, sg: (e, 0, 0)),
            pl.BlockSpec((None, d, f), lambda e, sg: (e, 0, 0)),
            pl.BlockSpec((None, f, d), lambda e, sg: (e, 0, 0)),
        ],
        out_specs=pl.BlockSpec(memory_space=pl.ANY),
        scratch_shapes=[pltpu.VMEM((d, f), BF16), pltpu.VMEM((d, f), BF16), pltpu.VMEM((f, d), BF16),
                        pltpu.VMEM((2, 2 * blk_words, LANES), U32), pltpu.VMEM((blk_words, LANES), U32),
                        pltpu.SemaphoreType.DMA((2,)), pltpu.SemaphoreType.DMA((2,)),
                        pltpu.SemaphoreType.DMA((2,))],
    )
    return pl.pallas_call(
        _expert_kernel,
        grid_spec=grid_spec,
        out_shape=jax.ShapeDtypeStruct(xs.shape, U32),
        compiler_params=_cparams(("arbitrary",)),
        name="experts",
    )(seg_flat, xs, w_gate, w_up, w_down)


def _combine_kernel(pos_ref, nxt_ref, ys_ref, wc_ref, h2_ref, x1_ref, sg_ref, su_ref, sd_ref,
                    g2_ref, nf_ref, scf_ref, shf_ref, o_ref, buf, sem):
    i = pl.program_id(0)
    tq = wc_ref.shape[1]
    slot = i & 1

    def gather(p_ref, to_slot):
        def issue(t, c):
            for k in range(TOP_K):
                pltpu.make_async_copy(
                    ys_ref.at[pl.ds(p_ref[t * TOP_K + k] * ROW_WORDS, ROW_WORDS), :],
                    buf.at[to_slot, k, pl.ds(t * ROW_WORDS, ROW_WORDS), :],
                    sem.at[to_slot]).start(priority=k % 2)
            return c

        lax.fori_loop(0, tq, issue, 0)

    @pl.when(i == 0)
    def _():
        gather(pos_ref, 0)

    @pl.when(i + 1 < pl.num_programs(0))
    def _():
        gather(nxt_ref, 1 - slot)

    h2 = h2_ref[...]
    a = jnp.dot(h2, sg_ref[...], preferred_element_type=F32)
    u = jnp.dot(h2, su_ref[...], preferred_element_type=F32)
    shared = jnp.dot((_silu(a) * u).astype(BF16), sd_ref[...], preferred_element_type=F32)

    for k in range(TOP_K):
        pltpu.make_async_copy(ys_ref.at[pl.ds(0, tq * ROW_WORDS), :], buf.at[slot, k],
                              sem.at[slot]).wait()

    wt = jnp.transpose(wc_ref[...])
    his, los = [], []
    for j in range(ROW_WORDS):
        acc_hi = jnp.zeros((tq, LANES), F32)
        acc_lo = jnp.zeros((tq, LANES), F32)
        for k in range(TOP_K):
            hi, lo = _unpack_pair(buf.at[slot, k][pl.ds(j, tq, stride=ROW_WORDS), :])
            wk = wt[:, k:k + 1]
            acc_hi = acc_hi + wk * hi
            acc_lo = acc_lo + wk * lo
        his.append(acc_hi)
        los.append(acc_lo)
    y = jnp.concatenate(his + los, axis=1) + shared
    x2 = x1_ref[...] + g2_ref[0] * y
    o_ref[...] = _norm_mod(x2, nf_ref[...], scf_ref[0], shf_ref[0])


def _combine(pos_tm, ys, w_c, h2, x1, sg, su, sd, g2, nf, scf, shf, seq):
    t, d = x1.shape
    tq = 128
    n_tiles = t // tq
    per_batch = seq // tq
    mod = pl.BlockSpec((1, 1, d), lambda i: (i // per_batch, 0, 0))
    full = lambda a: pl.BlockSpec(a.shape, lambda i: (0, 0))
    return pl.pallas_call(
        _combine_kernel,
        grid=(n_tiles,),
        in_specs=[
            pl.BlockSpec((tq * TOP_K,), lambda i: (i,), memory_space=pltpu.SMEM),
            pl.BlockSpec((tq * TOP_K,), lambda i: (jnp.minimum(i + 1, n_tiles - 1),),
                         memory_space=pltpu.SMEM),
            pl.BlockSpec(memory_space=pl.ANY),
            pl.BlockSpec((TOP_K, tq), lambda i: (0, i)),
            pl.BlockSpec((tq, d), lambda i: (i, 0)),
            pl.BlockSpec((tq, d), lambda i: (i, 0)),
            full(sg), full(su), full(sd),
            mod,
            pl.BlockSpec((1, d), lambda i: (0, 0)),
            mod, mod,
        ],
        out_specs=pl.BlockSpec((tq, d), lambda i: (i, 0)),
        out_shape=jax.ShapeDtypeStruct((t, d), F32),
        scratch_shapes=[pltpu.VMEM((2, TOP_K, tq * ROW_WORDS, LANES), U32),
                        pltpu.SemaphoreType.DMA((2,))],
        compiler_params=_cparams(("arbitrary",)),
        name="combine",
    )(pos_tm, pos_tm, ys, w_c, h2, x1, sg, su, sd, g2, nf.reshape(1, d), scf, shf)


def _rope_tables(seq):
    pos = jnp.arange(seq, dtype=F32)
    inv_freq = ROPE_BASE ** (-jnp.arange(0, HEAD_DIM, 2, dtype=F32) / HEAD_DIM)
    ang = pos[:, None] * inv_freq[None, :]
    cos, sin = jnp.cos(ang), jnp.sin(ang)
    return jnp.concatenate([cos, cos], axis=1), jnp.concatenate([-sin, sin], axis=1)


def kernel(x, c, w_ada, b_ada, norm1_g, w_in, ret_gn_g, w_out, norm2_g, w_router, router_bias,
           w_gate, w_up, w_down, ws_gate, ws_up, ws_down, w_ada_final, b_ada_final, norm_f_g):
    bsz, seq, d = x.shape
    t = bsz * seq
    depth = w_ada.shape[0]
    n_heads = d // (2 * HEAD_DIM)
    sb_w = n_heads * HEAD_DIM
    cos2, sin2 = _rope_tables(seq)
    c_pad = jnp.pad(c, ((0, 8 - bsz % 8), (0, 0)))
    n_rows = t * TOP_K + N_EXPERTS * MOE_BLOCK
    as_mod = lambda v: v.reshape(bsz, 1, d)

    x2 = x.reshape(t, d)
    for l in range(depth):
        mod = _ada(c_pad, w_ada[l], b_ada[l])[:bsz]
        sh1, sc1, g1, sh2, sc2, g2 = [as_mod(m) for m in jnp.split(mod, 6, axis=-1)]
        u = _inproj(x2, norm1_g[l], sc1, sh1, w_in[l].astype(BF16), seq)
        u3 = u.reshape(bsz, seq, -1)
        sb = _sb_attention(u3, n_heads)
        rt = _retention(u3, cos2, sin2, ret_gn_g[l], n_heads, 3 * n_heads)
        wo = w_out[l].astype(BF16)
        x1, h2, hp, lt = _outproj(sb.reshape(t, sb_w), rt.reshape(t, -1), wo[:sb_w], wo[sb_w:], x2,
                                  g1, norm2_g[l], sc2, sh2, jnp.transpose(w_router[l]).astype(BF16), seq)
        w_e, rank_e, cnt = _route(lt, router_bias[l])
        pos_c, w_c, seg = _plan(w_e, rank_e, cnt)
        pos_tm = jnp.transpose(pos_c).reshape(-1)
        xs = _dispatch(pos_tm, seg, hp, n_rows)
        ys = _experts(seg.reshape(-1), xs, w_gate[l], w_up[l], w_down[l])
        if l + 1 < depth:
            raise NotImplementedError("only the final layer fuses the output norm")
        modf = _ada(c_pad, w_ada_final, b_ada_final)[:bsz]
        shf, scf = [as_mod(m) for m in jnp.split(modf, 2, axis=-1)]
        x2 = _combine(pos_tm, ys, w_c, h2, x1, ws_gate[l].astype(BF16), ws_up[l].astype(BF16),
                      ws_down[l].astype(BF16), g2, norm_f_g, scf, shf, seq)
    return x2.reshape(bsz, seq, d)
```

```python
import functools

import jax
import jax.numpy as jnp
from jax import lax
from jax.experimental import pallas as pl
from jax.experimental.pallas import tpu as pltpu

F32 = jnp.float32
BF16 = jnp.bfloat16
I32 = jnp.int32
U32 = jnp.uint32

HEAD_DIM = 128
N_EXPERTS = 64
TOP_K = 8
N_GROUPS = 8
TOPK_GROUP = 4
ROUTED_SCALE = 2.5
EPS = 1e-6
ROPE_BASE = 10000.0

LANES = 128
ROW_WORDS = 8
MOE_BLOCK = 256
V7X_VMEM_LIMIT = 56 * 1024 * 1024
SB_SKIP_BOUND = -115.0


def _cparams(sem, vmem=V7X_VMEM_LIMIT):
    return pltpu.CompilerParams(dimension_semantics=sem, vmem_limit_bytes=vmem)


def _silu(v):
    return v * jax.nn.sigmoid(v)


def _pack_pair(hi_f32, lo_f32):
    hw = pltpu.bitcast(hi_f32, U32) & jnp.uint32(0xFFFF0000)
    lw = pltpu.bitcast(lo_f32, U32) >> 16
    return hw | lw


def _unpack_pair(w):
    hi = pltpu.bitcast(w & jnp.uint32(0xFFFF0000), F32)
    lo = pltpu.bitcast(w << 16, F32)
    return hi, lo


def _ada_kernel(c_ref, w_ref, b_ref, o_ref):
    cs = _silu(c_ref[...]).astype(BF16)
    o_ref[...] = jnp.dot(cs, w_ref[...].astype(BF16), preferred_element_type=F32) + b_ref[...]


def _ada(c_pad, w, b):
    rows, d = c_pad.shape
    n = w.shape[1]
    tn = 1024
    return pl.pallas_call(
        _ada_kernel,
        grid=(n // tn,),
        in_specs=[
            pl.BlockSpec((rows, d), lambda j: (0, 0)),
            pl.BlockSpec((d, tn), lambda j: (0, j)),
            pl.BlockSpec((1, tn), lambda j: (0, j)),
        ],
        out_specs=pl.BlockSpec((rows, tn), lambda j: (0, j)),
        out_shape=jax.ShapeDtypeStruct((rows, n), F32),
        compiler_params=_cparams(("arbitrary",)),
        name="ada",
    )(c_pad, w, b.reshape(1, n))


def _norm_mod(x, g, sc, sh):
    ms = jnp.mean(x * x, axis=-1, keepdims=True)
    return (x * lax.rsqrt(ms + EPS) * g) * (1.0 + sc) + sh


def _inproj_kernel(x_ref, g_ref, sc_ref, sh_ref, w_ref, o_ref, h_ref, *, chunk):
    first = pl.program_id(1) == 0

    @pl.when(first)
    def _():
        g = g_ref[...]
        sc = sc_ref[0]
        sh = sh_ref[0]
        for r in range(x_ref.shape[0] // chunk):
            rows = pl.ds(r * chunk, chunk)
            h = _norm_mod(x_ref[rows, :], g, sc, sh).astype(BF16)
            h_ref[rows, :] = h
            o_ref[rows, :] = jnp.dot(h, w_ref[...], preferred_element_type=F32).astype(BF16)

    @pl.when(jnp.logical_not(first))
    def _():
        o_ref[...] = jnp.dot(h_ref[...], w_ref[...], preferred_element_type=F32).astype(BF16)


def _inproj(x2, g, sc, sh, w_bf, seq):
    t, d = x2.shape
    n = w_bf.shape[1]
    tm = 1024
    tn = n // 4 if (n // 4) % (2 * LANES) == 0 else 1024
    per_batch = seq // tm
    return pl.pallas_call(
        functools.partial(_inproj_kernel, chunk=256),
        grid=(t // tm, n // tn),
        in_specs=[
            pl.BlockSpec((tm, d), lambda i, j: (i, 0)),
            pl.BlockSpec((1, d), lambda i, j: (0, 0)),
            pl.BlockSpec((1, 1, d), lambda i, j: (i // per_batch, 0, 0)),
            pl.BlockSpec((1, 1, d), lambda i, j: (i // per_batch, 0, 0)),
            pl.BlockSpec((d, tn), lambda i, j: (0, j)),
        ],
        out_specs=pl.BlockSpec((tm, tn), lambda i, j: (i, j)),
        out_shape=jax.ShapeDtypeStruct((t, n), BF16),
        scratch_shapes=[pltpu.VMEM((tm, d), BF16)],
        compiler_params=_cparams(("arbitrary", "arbitrary")),
        name="inproj",
    )(x2, g.reshape(1, d), sc, sh, w_bf)


def _sb_kernel(q_ref, k_ref, v_ref, o_ref, *, group):
    seq = q_ref.shape[0]
    blk = LANES
    scale = HEAD_DIM ** -0.5
    row = lax.broadcasted_iota(I32, (blk, blk), 0)
    col = lax.broadcasted_iota(I32, (blk, blk), 1)
    strict = col < row
    tr = lax.broadcasted_iota(I32, (2 * blk, 2 * blk), 0) % blk
    tc = lax.broadcasted_iota(I32, (2 * blk, 2 * blk), 1)
    tri = jnp.where((tc >= blk) | (tr > tc), 1.0, 0.0).astype(BF16)

    def key_rows(j):
        return pl.ds(pl.multiple_of(j * blk, blk), blk)

    def logits(qb, kb):
        z = lax.dot_general(qb, kb, (((1,), (1,)), ((), ())), preferred_element_type=F32) * scale
        sp = jnp.maximum(z, 0.0) + jnp.log(1.0 + jnp.exp(-jnp.abs(z)))
        return -sp, z - sp

    def suffix_sums(log_surv):
        hi = log_surv.astype(BF16)
        lo = (log_surv - hi.astype(F32)).astype(BF16)
        r = jnp.dot(jnp.concatenate([hi, lo], axis=1), tri, preferred_element_type=F32)
        return r[:, :blk], r[:, blk:]

    def tile(qb, j, g, carry, acc):
        log_surv, log_beta = logits(qb, k_ref[key_rows(j), g * blk:(g + 1) * blk])
        after, total = suffix_sums(log_surv)
        a = jnp.exp(log_beta + after + carry)
        acc = acc + jnp.dot(a.astype(BF16), v_ref[key_rows(j), g * blk:(g + 1) * blk],
                            preferred_element_type=F32)
        return carry + total, acc

    def group_start(qbs, i):
        n_near = 3
        js = [i, jnp.maximum(i - 1, 0), jnp.maximum(i - 2, 0)]
        ivec = jnp.full((blk, blk), i, I32)
        masks = [strict, ivec >= 1, ivec >= 2]
        log_betas, parts = [], []
        for g in range(group):
            cols = slice(g * blk, (g + 1) * blk)
            kcat = jnp.concatenate([k_ref[key_rows(j), cols] for j in js], axis=0)
            log_surv, log_beta = logits(qbs[g], kcat)
            log_betas.append(log_beta)
            for m in range(n_near):
                parts.append(jnp.where(masks[m], log_surv[:, m * blk:(m + 1) * blk], 0.0))
        after, total = suffix_sums(jnp.concatenate(parts, axis=0))
        out = []
        for g in range(group):
            cols = slice(g * blk, (g + 1) * blk)
            carry = jnp.zeros((blk, blk), F32)
            weights = []
            for m in range(n_near):
                rows = slice((g * n_near + m) * blk, (g * n_near + m + 1) * blk)
                a = jnp.exp(log_betas[g][:, m * blk:(m + 1) * blk] + after[rows] + carry)
                weights.append(jnp.where(masks[m], a, 0.0).astype(BF16))
                carry = carry + total[rows]
            vcat = jnp.concatenate([v_ref[key_rows(j), cols] for j in js], axis=0)
            acc = jnp.dot(jnp.concatenate(weights, axis=1), vcat, preferred_element_type=F32)
            out.append((carry, acc))
        return out

    q_blocks = q_ref.shape[0] // blk

    def q_block(local, c):
        i = pl.program_id(1) * q_blocks + local
        rows = pl.ds(pl.multiple_of(local * blk, blk), blk)
        qbs = [q_ref[rows, g * blk:(g + 1) * blk] for g in range(group)]
        first = group_start(qbs, i)
        carries = tuple(f[0] for f in first)
        accs = tuple(f[1] for f in first)

        def cond(st):
            j, crs, _ = st
            top = functools.reduce(jnp.maximum, crs)
            return jnp.logical_and(j >= 0, jnp.max(top) > SB_SKIP_BOUND)

        def body(st):
            j, crs, acs = st
            nxt = [tile(qbs[g], j, g, crs[g], acs[g]) for g in range(group)]
            return j - 1, tuple(n[0] for n in nxt), tuple(n[1] for n in nxt)

        _, _, accs = lax.while_loop(cond, body, (i - 3, carries, accs))
        for g in range(group):
            o_ref[rows, g * blk:(g + 1) * blk] = accs[g].astype(BF16)
        return c

    lax.fori_loop(0, q_blocks, q_block, 0)


SB_ROW_TILE = 512


def _sb_attention(u3, n_heads):
    b, s, _ = u3.shape
    w = n_heads * HEAD_DIM
    tq = min(SB_ROW_TILE, s)
    return pl.pallas_call(
        functools.partial(_sb_kernel, group=n_heads),
        grid=(b, s // tq),
        in_specs=[pl.BlockSpec((None, tq, w), lambda bi, qi: (bi, qi, 0)),
                  pl.BlockSpec((None, s, w), lambda bi, qi: (bi, 0, 1)),
                  pl.BlockSpec((None, s, w), lambda bi, qi: (bi, 0, 2))],
        out_specs=pl.BlockSpec((None, tq, w), lambda bi, qi: (bi, qi, 0)),
        out_shape=jax.ShapeDtypeStruct((b, s, w), BF16),
        compiler_params=_cparams(("arbitrary", "arbitrary")),
        name="sb_attn",
    )(u3, u3, u3)


def _ret_kernel(q_ref, k_ref, v_ref, g_ref, cos_ref, sin_ref, gn_ref, o_ref, *, chunk, unroll):
    seq = q_ref.shape[0]
    dh = HEAD_DIM
    head = pl.program_id(1)

    def log_gamma(shape):
        hv = jnp.full(shape, head, I32).astype(F32)
        return jnp.log(1.0 - jnp.exp2(-5.0 - hv))

    diff = (lax.broadcasted_iota(I32, (chunk, chunk), 0)
            - lax.broadcasted_iota(I32, (chunk, chunk), 1)).astype(F32)
    lower = diff >= 0.0
    dmat = jnp.where(lower, jnp.exp(jnp.where(lower, diff, 0.0) * log_gamma((chunk, chunk))), 0.0)
    pos = lax.broadcasted_iota(I32, (chunk, dh), 0).astype(F32)
    k_decay = jnp.exp((chunk - 1.0 - pos) * log_gamma((chunk, dh)))
    q_decay = jnp.exp((pos + 1.0) * log_gamma((chunk, dh)))
    chunk_decay = jnp.exp(chunk * log_gamma((dh, dh)))
    gn = gn_ref[...]
    scale = dh ** -0.5

    def rope(t, cs, sn):
        return t * cs + pltpu.roll(t, dh // 2, 1) * sn

    def body(n, state):
        rows = pl.ds(pl.multiple_of(n * chunk, chunk), chunk)
        cs = cos_ref[rows, :]
        sn = sin_ref[rows, :]
        qc = rope(q_ref[rows, :].astype(F32), cs, sn)
        kc = rope(k_ref[rows, :].astype(F32), cs, sn) * scale
        vc = v_ref[rows, :]
        scores = lax.dot_general(qc.astype(BF16), kc.astype(BF16), (((1,), (1,)), ((), ())),
                                 preferred_element_type=F32) * dmat
        intra = jnp.dot(scores.astype(BF16), vc, preferred_element_type=F32)
        cross = jnp.dot((qc * q_decay).astype(BF16), state.astype(BF16), preferred_element_type=F32)
        out = intra + cross
        kd_t = jnp.transpose(kc * k_decay).astype(BF16)
        state = state * chunk_decay + jnp.dot(kd_t, vc, preferred_element_type=F32)
        ms = jnp.mean(out * out, axis=-1, keepdims=True)
        y = out * lax.rsqrt(ms + EPS) * gn * _silu(g_ref[rows, :].astype(F32))
        o_ref[rows, :] = y.astype(BF16)
        return state

    lax.fori_loop(0, seq // chunk, body, jnp.zeros((dh, dh), F32), unroll=unroll)


RET_CHUNK = 256
RET_UNROLL = 2


def _retention(u3, cos2, sin2, gn, n_heads, col0):
    b, s, _ = u3.shape
    spec = lambda off: pl.BlockSpec((None, s, HEAD_DIM), lambda bi, h: (bi, 0, col0 + off + h))
    return pl.pallas_call(
        functools.partial(_ret_kernel, chunk=RET_CHUNK, unroll=RET_UNROLL),
        grid=(b, n_heads),
        in_specs=[spec(0), spec(n_heads), spec(2 * n_heads), spec(3 * n_heads),
                  pl.BlockSpec((s, HEAD_DIM), lambda bi, h: (0, 0)),
                  pl.BlockSpec((s, HEAD_DIM), lambda bi, h: (0, 0)),
                  pl.BlockSpec((1, HEAD_DIM), lambda bi, h: (0, h))],
        out_specs=pl.BlockSpec((None, s, HEAD_DIM), lambda bi, h: (bi, 0, h)),
        out_shape=jax.ShapeDtypeStruct((b, s, n_heads * HEAD_DIM), BF16),
        compiler_params=_cparams(("arbitrary", "arbitrary")),
        name="retention",
    )(u3, u3, u3, u3, cos2, sin2, gn.reshape(1, -1))


def _outproj_kernel(sb_ref, rt_ref, wa_ref, wb_ref, x_ref, g1_ref, n2_ref, sc_ref, sh_ref, wr_ref,
                    x1_ref, h2_ref, hp_ref, lt_ref, *, chunk):
    d = x_ref.shape[1]
    half = d // 2
    g1 = g1_ref[0]
    n2 = n2_ref[...]
    sc = sc_ref[0]
    sh = sh_ref[0]
    for r in range(x_ref.shape[0] // chunk):
        rows = pl.ds(r * chunk, chunk)
        o = jnp.dot(sb_ref[rows, :], wa_ref[...], preferred_element_type=F32)
        o = o + jnp.dot(rt_ref[rows, :], wb_ref[...], preferred_element_type=F32)
        x1 = x_ref[rows, :] + g1 * o
        x1_ref[rows, :] = x1
        h2 = _norm_mod(x1, n2, sc, sh).astype(BF16)
        h2_ref[rows, :] = h2
        lt_ref[:, rows] = lax.dot_general(wr_ref[...], h2, (((1,), (1,)), ((), ())),
                                          preferred_element_type=F32)
        words = _pack_pair(h2[:, :half].astype(F32), h2[:, half:].astype(F32))
        for j in range(ROW_WORDS):
            hp_ref[pl.ds(r * chunk * ROW_WORDS + j, chunk, stride=ROW_WORDS), :] = (
                words[:, j * LANES:(j + 1) * LANES])


def _outproj(sb2, rt2, wa, wb, x2, g1, n2, sc2, sh2, wr_t, seq):
    t, d = x2.shape
    tm = 512
    per_batch = seq // tm
    mod = pl.BlockSpec((1, 1, d), lambda i: (i // per_batch, 0, 0))
    return pl.pallas_call(
        functools.partial(_outproj_kernel, chunk=512),
        grid=(t // tm,),
        in_specs=[
            pl.BlockSpec((tm, sb2.shape[1]), lambda i: (i, 0)),
            pl.BlockSpec((tm, rt2.shape[1]), lambda i: (i, 0)),
            pl.BlockSpec(wa.shape, lambda i: (0, 0)),
            pl.BlockSpec(wb.shape, lambda i: (0, 0)),
            pl.BlockSpec((tm, d), lambda i: (i, 0)),
            mod,
            pl.BlockSpec((1, d), lambda i: (0, 0)),
            mod, mod,
            pl.BlockSpec(wr_t.shape, lambda i: (0, 0)),
        ],
        out_specs=[
            pl.BlockSpec((tm, d), lambda i: (i, 0)),
            pl.BlockSpec((tm, d), lambda i: (i, 0)),
            pl.BlockSpec((tm * ROW_WORDS, LANES), lambda i: (i, 0)),
            pl.BlockSpec((N_EXPERTS, tm), lambda i: (0, i)),
        ],
        out_shape=[
            jax.ShapeDtypeStruct((t, d), F32),
            jax.ShapeDtypeStruct((t, d), BF16),
            jax.ShapeDtypeStruct((t * ROW_WORDS, LANES), U32),
            jax.ShapeDtypeStruct((N_EXPERTS, t), F32),
        ],
        compiler_params=_cparams(("arbitrary",)),
        name="outproj",
    )(sb2, rt2, wa, wb, x2, g1, n2.reshape(1, d), sc2, sh2, wr_t)


def _beats(row, allv, row_idx, idx):
    return (row > allv) | ((row == allv) & (row_idx < idx))


def _route_kernel(lt_ref, bias_ref, w_ref, rank_ref, cnt_ref, carry_ref):
    tr = lt_ref.shape[1]
    gsz = N_EXPERTS // N_GROUPS
    neg_inf = jnp.float32(-jnp.inf)

    @pl.when(pl.program_id(0) == 0)
    def _():
        carry_ref[...] = jnp.zeros_like(carry_ref)

    scores = jax.nn.sigmoid(lt_ref[...])
    choice = scores + bias_ref[...]
    sub = lax.broadcasted_iota(I32, (gsz, tr), 0)
    gs_rows = []
    for g in range(N_GROUPS):
        cg = choice[g * gsz:(g + 1) * gsz, :]
        m1 = jnp.max(cg, axis=0, keepdims=True)
        first = jnp.min(jnp.where(cg == m1, sub, gsz), axis=0, keepdims=True)
        m2 = jnp.max(jnp.where(sub == first, neg_inf, cg), axis=0, keepdims=True)
        gs_rows.append(m1 + m2)
    gs = jnp.concatenate(gs_rows, axis=0)
    gi = lax.broadcasted_iota(I32, (N_GROUPS, tr), 0)
    grank = jnp.zeros((N_GROUPS, tr), I32)
    for g in range(N_GROUPS):
        grank = grank + _beats(gs[g:g + 1, :], gs, g, gi).astype(I32)
    gmask = grank < TOPK_GROUP
    emask = jnp.concatenate(
        [jnp.broadcast_to(gmask[g:g + 1, :], (gsz, tr)) for g in range(N_GROUPS)], axis=0)
    masked = jnp.where(emask, choice, neg_inf)
    ei = lax.broadcasted_iota(I32, (N_EXPERTS, tr), 0)
    erank = jnp.zeros((N_EXPERTS, tr), I32)
    for e in range(N_EXPERTS):
        erank = erank + _beats(masked[e:e + 1, :], masked, e, ei).astype(I32)
    sel = erank < TOP_K
    ssel = jnp.where(sel, scores, 0.0)
    denom = jnp.sum(ssel, axis=0, keepdims=True)
    w_ref[...] = ssel / denom * ROUTED_SCALE

    self = sel.astype(F32)
    upper = (lax.broadcasted_iota(I32, (tr, tr), 0) < lax.broadcasted_iota(I32, (tr, tr), 1))
    prefix = jnp.dot(self.astype(BF16), upper.astype(BF16), preferred_element_type=F32)
    carry = carry_ref[...]
    rank = prefix + carry[:, 0:1]
    rank_ref[...] = jnp.where(sel, rank, -1.0).astype(I32)
    carry = carry + jnp.sum(self, axis=1, keepdims=True)
    carry_ref[...] = carry
    cnt_ref[...] = carry


def _route(lt, bias):
    e, t = lt.shape
    tr = min(512, t)
    return pl.pallas_call(
        _route_kernel,
        grid=(t // tr,),
        in_specs=[pl.BlockSpec((e, tr), lambda i: (0, i)),
                  pl.BlockSpec((e, 1), lambda i: (0, 0))],
        out_specs=[pl.BlockSpec((e, tr), lambda i: (0, i)),
                   pl.BlockSpec((e, tr), lambda i: (0, i)),
                   pl.BlockSpec((e, LANES), lambda i: (0, 0))],
        out_shape=[jax.ShapeDtypeStruct((e, t), F32),
                   jax.ShapeDtypeStruct((e, t), I32),
                   jax.ShapeDtypeStruct((e, LANES), F32)],
        scratch_shapes=[pltpu.VMEM((e, LANES), F32)],
        compiler_params=_cparams(("arbitrary",)),
        name="route",
    )(lt, bias.reshape(e, 1))


def _plan_kernel(w_ref, rank_ref, cnt_ref, pos_ref, wc_ref, seg_ref):
    cnt = cnt_ref[...]
    nblk = jnp.floor((cnt + (MOE_BLOCK - 1.0)) * (1.0 / MOE_BLOCK))
    lower = (lax.broadcasted_iota(I32, (N_EXPERTS, N_EXPERTS), 1)
             < lax.broadcasted_iota(I32, (N_EXPERTS, N_EXPERTS), 0)).astype(BF16)
    bstart = jnp.dot(lower, nblk.astype(BF16), preferred_element_type=F32)
    bend = bstart + nblk

    @pl.when(pl.program_id(0) == 0)
    def _():
        seg_ref[...] = jnp.concatenate(
            [bstart[:, 0:1] * MOE_BLOCK, cnt[:, 0:1], nblk[:, 0:1] * MOE_BLOCK, bend[:, 0:1]],
            axis=1).astype(I32)

    rank = rank_ref[...]
    sel = rank >= 0
    pos = bstart[:, 0:1] * MOE_BLOCK + rank.astype(F32)
    slot = jnp.dot(lower, sel.astype(BF16), preferred_element_type=F32)
    w = w_ref[...]
    pos_rows, w_rows = [], []
    for k in range(TOP_K):
        m = sel & (slot == float(k))
        pos_rows.append(jnp.sum(jnp.where(m, pos, 0.0), axis=0, keepdims=True))
        w_rows.append(jnp.sum(jnp.where(m, w, 0.0), axis=0, keepdims=True))
    pos_ref[...] = jnp.concatenate(pos_rows, axis=0).astype(I32)
    wc_ref[...] = jnp.concatenate(w_rows, axis=0)


def _plan(w, rank, cnt):
    e, t = w.shape
    tr = min(1024, t)
    return pl.pallas_call(
        _plan_kernel,
        grid=(t // tr,),
        in_specs=[pl.BlockSpec((e, tr), lambda i: (0, i)),
                  pl.BlockSpec((e, tr), lambda i: (0, i)),
                  pl.BlockSpec((e, LANES), lambda i: (0, 0))],
        out_specs=[pl.BlockSpec((TOP_K, tr), lambda i: (0, i)),
                   pl.BlockSpec((TOP_K, tr), lambda i: (0, i)),
                   pl.BlockSpec((e, 4), lambda i: (0, 0))],
        out_shape=[jax.ShapeDtypeStruct((TOP_K, t), I32),
                   jax.ShapeDtypeStruct((TOP_K, t), F32),
                   jax.ShapeDtypeStruct((e, 4), I32)],
        compiler_params=_cparams(("arbitrary",)),
        name="plan",
    )(w, rank, cnt)


def _row_copy(src, src_row, dst, dst_row, sem):
    return pltpu.make_async_copy(src.at[pl.ds(src_row * ROW_WORDS, ROW_WORDS), :],
                                 dst.at[pl.ds(dst_row * ROW_WORDS, ROW_WORDS), :], sem)


def _dispatch_kernel(pos_ref, seg_ref, hp_ref, xs_ref, zero_ref, sem, zsem):
    tq = pos_ref.shape[0] // TOP_K
    blk_words = MOE_BLOCK * ROW_WORDS
    n_blocks = xs_ref.shape[0] // blk_words

    @pl.when(pl.program_id(0) == 0)
    def _():
        zero_ref[...] = jnp.zeros_like(zero_ref)
        zrow = zero_ref.at[pl.ds(0, ROW_WORDS), :]

        def per_expert(e, n):
            start = seg_ref[e, 0] + seg_ref[e, 1]
            npad = seg_ref[e, 2] - seg_ref[e, 1]

            def fill(r, c):
                pltpu.make_async_copy(
                    zrow, xs_ref.at[pl.ds((start + r) * ROW_WORDS, ROW_WORDS), :], zsem).start()
                return c

            lax.fori_loop(0, npad, fill, 0)
            return n + npad

        n_pad_rows = lax.fori_loop(0, N_EXPERTS, per_expert, 0)

        def drain_row(r, c):
            pltpu.make_async_copy(zrow, xs_ref.at[pl.ds(0, ROW_WORDS), :], zsem).wait()
            return c

        lax.fori_loop(0, n_pad_rows, drain_row, 0)
        n_used = seg_ref[N_EXPERTS - 1, 3]

        def fill_block(b, c):
            pltpu.make_async_copy(
                zero_ref, xs_ref.at[pl.ds(b * blk_words, blk_words), :], zsem).start()
            return c

        lax.fori_loop(n_used, n_blocks, fill_block, 0)

        def drain_block(b, c):
            pltpu.make_async_copy(zero_ref, xs_ref.at[pl.ds(0, blk_words), :], zsem).wait()
            return c

        lax.fori_loop(n_used, n_blocks, drain_block, 0)

    def issue(t, c):
        for k in range(TOP_K):
            _row_copy(hp_ref, t, xs_ref, pos_ref[t * TOP_K + k], sem).start(priority=k % 2)
        return c

    lax.fori_loop(0, tq, issue, 0)
    for k in range(TOP_K):
        pltpu.make_async_copy(hp_ref, xs_ref.at[pl.ds(0, tq * ROW_WORDS), :], sem).wait()


def _dispatch(pos_tm, seg, hp, n_rows):
    t = pos_tm.shape[0] // TOP_K
    tq = min(1024, t)
    return pl.pallas_call(
        _dispatch_kernel,
        grid=(t // tq,),
        in_specs=[pl.BlockSpec((tq * TOP_K,), lambda i: (i,), memory_space=pltpu.SMEM),
                  pl.BlockSpec(seg.shape, lambda i: (0, 0), memory_space=pltpu.SMEM),
                  pl.BlockSpec((tq * ROW_WORDS, LANES), lambda i: (i, 0))],
        out_specs=pl.BlockSpec(memory_space=pl.ANY),
        out_shape=jax.ShapeDtypeStruct((n_rows * ROW_WORDS, LANES), U32),
        scratch_shapes=[pltpu.VMEM((MOE_BLOCK * ROW_WORDS, LANES), U32),
                        pltpu.SemaphoreType.DMA(()), pltpu.SemaphoreType.DMA(())],
        compiler_params=_cparams(("arbitrary",)),
        name="dispatch",
    )(pos_tm, seg, hp)


def _expert_kernel(seg_ref, xs_ref, wg_ref, wu_ref, wd_ref, ys_ref,
                   wgb, wub, wdb, xbuf, ybuf, xsem, ysem):
    e = pl.program_id(0)
    last_step = e == pl.num_programs(0) - 1
    rows = MOE_BLOCK
    blk_words = MOE_BLOCK * ROW_WORDS
    n_blocks = ys_ref.shape[0] // blk_words
    first = lax.div(seg_ref[e * 4], MOE_BLOCK)
    end = seg_ref[e * 4 + 3]
    n_used = seg_ref[(pl.num_programs(0) - 1) * 4 + 3]

    def x_copy(g):
        slot = g & 1
        return pltpu.make_async_copy(xs_ref.at[pl.ds(g * blk_words, blk_words), :],
                                     xbuf.at[slot], xsem.at[slot])

    def y_copy(g):
        slot = g & 1
        return pltpu.make_async_copy(ybuf.at[slot],
                                     ys_ref.at[pl.ds(g * blk_words, blk_words), :], ysem.at[slot])

    @pl.when(jnp.logical_and(e == 0, n_used > 0))
    def _():
        x_copy(0).start(priority=1)

    @pl.when(end > first)
    def _():
        wgb[...] = wg_ref[...].astype(BF16)
        wub[...] = wu_ref[...].astype(BF16)
        wdb[...] = wd_ref[...].astype(BF16)

        def block(g, c):
            slot = g & 1
            x_copy(g).wait()

            @pl.when(g + 1 < n_used)
            def _():
                x_copy(g + 1).start(priority=1)

            @pl.when(g >= 2)
            def _():
                y_copy(g - 2).wait()

            xin = xbuf.at[slot]
            his, los = [], []
            for j in range(ROW_WORDS):
                hi, lo = _unpack_pair(xin[pl.ds(j, rows, stride=ROW_WORDS), :])
                his.append(hi.astype(BF16))
                los.append(lo.astype(BF16))
            xb = jnp.concatenate(his + los, axis=1)
            a = jnp.dot(xb, wgb[...], preferred_element_type=F32)
            u = jnp.dot(xb, wub[...], preferred_element_type=F32)
            hid = (_silu(a) * u).astype(BF16)
            out = jnp.dot(hid, wdb[...], preferred_element_type=F32).astype(BF16).astype(F32)
            half = out.shape[1] // 2
            words = _pack_pair(out[:, :half], out[:, half:])
            yout = ybuf.at[slot]
            for j in range(ROW_WORDS):
                yout[pl.ds(j, rows, stride=ROW_WORDS), :] = words[:, j * LANES:(j + 1) * LANES]
            y_copy(g).start(priority=1)
            return c

        lax.fori_loop(first, end, block, 0)

    @pl.when(last_step)
    def _():
        @pl.when(n_used >= 2)
        def _():
            y_copy(n_used - 2).wait()

        @pl.when(n_used >= 1)
        def _():
            y_copy(n_used - 1).wait()

        ybuf[0] = jnp.zeros(ybuf.shape[1:], ybuf.dtype)

        def fill(g, c):
            pltpu.make_async_copy(ybuf.at[0], ys_ref.at[pl.ds(g * blk_words, blk_words), :],
                                  ysem.at[0]).start()
            return c

        lax.fori_loop(n_used, n_blocks, fill, 0)

        def drain(g, c):
            pltpu.make_async_copy(ybuf.at[0], ys_ref.at[pl.ds(0, blk_words), :], ysem.at[0]).wait()
            return c

        lax.fori_loop(n_used, n_blocks, drain, 0)


def _experts(seg_flat, xs, w_gate, w_up, w_down):
    n_exp, d, f = w_gate.shape
    blk_words = MOE_BLOCK * ROW_WORDS
    grid_spec = pltpu.PrefetchScalarGridSpec(
        num_scalar_prefetch=1,
        grid=(n_exp,),
        in_specs=[
            pl.BlockSpec(memory_space=pl.ANY),
            pl.BlockSpec((None, d, f), lambda e, sg: (e, 0, 0)),
            pl.BlockSpec((None, d, f), lambda e, sg: (e, 0, 0)),
            pl.BlockSpec((None, f, d), lambda e, sg: (e, 0, 0)),
        ],
        out_specs=pl.BlockSpec(memory_space=pl.ANY),
        scratch_shapes=[pltpu.VMEM((d, f), BF16), pltpu.VMEM((d, f), BF16), pltpu.VMEM((f, d), BF16),
                        pltpu.VMEM((2, blk_words, LANES), U32), pltpu.VMEM((2, blk_words, LANES), U32),
                        pltpu.SemaphoreType.DMA((2,)), pltpu.SemaphoreType.DMA((2,))],
    )
    return pl.pallas_call(
        _expert_kernel,
        grid_spec=grid_spec,
        out_shape=jax.ShapeDtypeStruct(xs.shape, U32),
        compiler_params=_cparams(("arbitrary",)),
        name="experts",
    )(seg_flat, xs, w_gate, w_up, w_down)


def _combine_kernel(pos_ref, nxt_ref, ys_ref, wc_ref, h2_ref, x1_ref, sg_ref, su_ref, sd_ref,
                    g2_ref, nf_ref, scf_ref, shf_ref, o_ref, buf, sem):
    i = pl.program_id(0)
    tq = wc_ref.shape[1]
    slot = i & 1

    def gather(p_ref, to_slot):
        def issue(t, c):
            for k in range(TOP_K):
                pltpu.make_async_copy(
                    ys_ref.at[pl.ds(p_ref[t * TOP_K + k] * ROW_WORDS, ROW_WORDS), :],
                    buf.at[to_slot, k, pl.ds(t * ROW_WORDS, ROW_WORDS), :],
                    sem.at[to_slot]).start(priority=k % 2)
            return c

        lax.fori_loop(0, tq, issue, 0)

    @pl.when(i == 0)
    def _():
        gather(pos_ref, 0)

    @pl.when(i + 1 < pl.num_programs(0))
    def _():
        gather(nxt_ref, 1 - slot)

    h2 = h2_ref[...]
    a = jnp.dot(h2, sg_ref[...], preferred_element_type=F32)
    u = jnp.dot(h2, su_ref[...], preferred_element_type=F32)
    shared = jnp.dot((_silu(a) * u).astype(BF16), sd_ref[...], preferred_element_type=F32)

    for k in range(TOP_K):
        pltpu.make_async_copy(ys_ref.at[pl.ds(0, tq * ROW_WORDS), :], buf.at[slot, k],
                              sem.at[slot]).wait()

    wt = jnp.transpose(wc_ref[...])
    his, los = [], []
    for j in range(ROW_WORDS):
        acc_hi = jnp.zeros((tq, LANES), F32)
        acc_lo = jnp.zeros((tq, LANES), F32)
        for k in range(TOP_K):
            hi, lo = _unpack_pair(buf.at[slot, k][pl.ds(j, tq, stride=ROW_WORDS), :])
            wk = wt[:, k:k + 1]
            acc_hi = acc_hi + wk * hi
            acc_lo = acc_lo + wk * lo
        his.append(acc_hi)
        los.append(acc_lo)
    y = jnp.concatenate(his + los, axis=1) + shared
    x2 = x1_ref[...] + g2_ref[0] * y
    o_ref[...] = _norm_mod(x2, nf_ref[...], scf_ref[0], shf_ref[0])


def _combine(pos_tm, ys, w_c, h2, x1, sg, su, sd, g2, nf, scf, shf, seq):
    t, d = x1.shape
    tq = 128
    n_tiles = t // tq
    per_batch = seq // tq
    mod = pl.BlockSpec((1, 1, d), lambda i: (i // per_batch, 0, 0))
    full = lambda a: pl.BlockSpec(a.shape, lambda i: (0, 0))
    return pl.pallas_call(
        _combine_kernel,
        grid=(n_tiles,),
        in_specs=[
            pl.BlockSpec((tq * TOP_K,), lambda i: (i,), memory_space=pltpu.SMEM),
            pl.BlockSpec((tq * TOP_K,), lambda i: (jnp.minimum(i + 1, n_tiles - 1),),
                         memory_space=pltpu.SMEM),
            pl.BlockSpec(memory_space=pl.ANY),
            pl.BlockSpec((TOP_K, tq), lambda i: (0, i)),
            pl.BlockSpec((tq, d), lambda i: (i, 0)),
            pl.BlockSpec((tq, d), lambda i: (i, 0)),
            full(sg), full(su), full(sd),
            mod,
            pl.BlockSpec((1, d), lambda i: (0, 0)),
            mod, mod,
        ],
        out_specs=pl.BlockSpec((tq, d), lambda i: (i, 0)),
        out_shape=jax.ShapeDtypeStruct((t, d), F32),
        scratch_shapes=[pltpu.VMEM((2, TOP_K, tq * ROW_WORDS, LANES), U32),
                        pltpu.SemaphoreType.DMA((2,))],
        compiler_params=_cparams(("arbitrary",)),
        name="combine",
    )(pos_tm, pos_tm, ys, w_c, h2, x1, sg, su, sd, g2, nf.reshape(1, d), scf, shf)


def _rope_tables(seq):
    pos = jnp.arange(seq, dtype=F32)
    inv_freq = ROPE_BASE ** (-jnp.arange(0, HEAD_DIM, 2, dtype=F32) / HEAD_DIM)
    ang = pos[:, None] * inv_freq[None, :]
    cos, sin = jnp.cos(ang), jnp.sin(ang)
    return jnp.concatenate([cos, cos], axis=1), jnp.concatenate([-sin, sin], axis=1)


def kernel(x, c, w_ada, b_ada, norm1_g, w_in, ret_gn_g, w_out, norm2_g, w_router, router_bias,
           w_gate, w_up, w_down, ws_gate, ws_up, ws_down, w_ada_final, b_ada_final, norm_f_g):
    bsz, seq, d = x.shape
    t = bsz * seq
    depth = w_ada.shape[0]
    n_heads = d // (2 * HEAD_DIM)
    sb_w = n_heads * HEAD_DIM
    cos2, sin2 = _rope_tables(seq)
    c_pad = jnp.pad(c, ((0, 8 - bsz % 8), (0, 0)))
    n_rows = t * TOP_K + N_EXPERTS * MOE_BLOCK
    as_mod = lambda v: v.reshape(bsz, 1, d)

    x2 = x.reshape(t, d)
    for l in range(depth):
        mod = _ada(c_pad, w_ada[l], b_ada[l])[:bsz]
        sh1, sc1, g1, sh2, sc2, g2 = [as_mod(m) for m in jnp.split(mod, 6, axis=-1)]
        u = _inproj(x2, norm1_g[l], sc1, sh1, w_in[l].astype(BF16), seq)
        u3 = u.reshape(bsz, seq, -1)
        sb = _sb_attention(u3, n_heads)
        rt = _retention(u3, cos2, sin2, ret_gn_g[l], n_heads, 3 * n_heads)
        wo = w_out[l].astype(BF16)
        x1, h2, hp, lt = _outproj(sb.reshape(t, sb_w), rt.reshape(t, -1), wo[:sb_w], wo[sb_w:], x2,
                                  g1, norm2_g[l], sc2, sh2, jnp.transpose(w_router[l]).astype(BF16), seq)
        w_e, rank_e, cnt = _route(lt, router_bias[l])
        pos_c, w_c, seg = _plan(w_e, rank_e, cnt)
        pos_tm = jnp.transpose(pos_c).reshape(-1)
        xs = _dispatch(pos_tm, seg, hp, n_rows)
        ys = _experts(seg.reshape(-1), xs, w_gate[l], w_up[l], w_down[l])
        if l + 1 < depth:
            raise NotImplementedError("only the final layer fuses the output norm")
        modf = _ada(c_pad, w_ada_final, b_ada_final)[:bsz]
        shf, scf = [as_mod(m) for m in jnp.split(modf, 2, axis=-1)]
        x2 = _combine(pos_tm, ys, w_c, h2, x1, ws_gate[l].astype(BF16), ws_up[l].astype(BF16),
                      ws_down[l].astype(BF16), g2, norm_f_g, scf, shf, seq)
    return x2.reshape(bsz, seq, d)
```

```python
import functools

import jax
import jax.numpy as jnp
from jax import lax
from jax.experimental import pallas as pl
from jax.experimental.pallas import tpu as pltpu

F32 = jnp.float32
BF16 = jnp.bfloat16
I32 = jnp.int32
U32 = jnp.uint32

HEAD_DIM = 128
N_EXPERTS = 64
TOP_K = 8
N_GROUPS = 8
TOPK_GROUP = 4
ROUTED_SCALE = 2.5
EPS = 1e-6
ROPE_BASE = 10000.0

LANES = 128
ROW_WORDS = 8
MOE_BLOCK = 256
V7X_VMEM_LIMIT = 56 * 1024 * 1024
LOG2_E = 1.4426950408889634
SB_SKIP_BOUND = -115.0 * LOG2_E


def _cparams(sem, vmem=V7X_VMEM_LIMIT):
    return pltpu.CompilerParams(dimension_semantics=sem, vmem_limit_bytes=vmem)


def _silu(v):
    return v * jax.nn.sigmoid(v)


def _pack_pair(hi_f32, lo_f32):
    hw = pltpu.bitcast(hi_f32, U32) & jnp.uint32(0xFFFF0000)
    lw = pltpu.bitcast(lo_f32, U32) >> 16
    return hw | lw


def _unpack_pair(w):
    hi = pltpu.bitcast(w & jnp.uint32(0xFFFF0000), F32)
    lo = pltpu.bitcast(w << 16, F32)
    return hi, lo


def _ada_kernel(c_ref, w_ref, b_ref, o_ref):
    cs = _silu(c_ref[...]).astype(BF16)
    o_ref[...] = jnp.dot(cs, w_ref[...].astype(BF16), preferred_element_type=F32) + b_ref[...]


def _ada(c_pad, w, b):
    rows, d = c_pad.shape
    n = w.shape[1]
    tn = 1024
    return pl.pallas_call(
        _ada_kernel,
        grid=(n // tn,),
        in_specs=[
            pl.BlockSpec((rows, d), lambda j: (0, 0)),
            pl.BlockSpec((d, tn), lambda j: (0, j)),
            pl.BlockSpec((1, tn), lambda j: (0, j)),
        ],
        out_specs=pl.BlockSpec((rows, tn), lambda j: (0, j)),
        out_shape=jax.ShapeDtypeStruct((rows, n), F32),
        compiler_params=_cparams(("arbitrary",)),
        name="ada",
    )(c_pad, w, b.reshape(1, n))


def _norm_mod(x, g, sc, sh):
    ms = jnp.mean(x * x, axis=-1, keepdims=True)
    return (x * lax.rsqrt(ms + EPS) * g) * (1.0 + sc) + sh


def _inproj_kernel(x_ref, g_ref, sc_ref, sh_ref, w_ref, o_ref, h_ref, *, chunk):
    first = pl.program_id(1) == 0

    @pl.when(first)
    def _():
        g = g_ref[...]
        sc = sc_ref[0]
        sh = sh_ref[0]
        for r in range(x_ref.shape[0] // chunk):
            rows = pl.ds(r * chunk, chunk)
            h = _norm_mod(x_ref[rows, :], g, sc, sh).astype(BF16)
            h_ref[rows, :] = h
            o_ref[rows, :] = jnp.dot(h, w_ref[...], preferred_element_type=F32).astype(BF16)

    @pl.when(jnp.logical_not(first))
    def _():
        o_ref[...] = jnp.dot(h_ref[...], w_ref[...], preferred_element_type=F32).astype(BF16)


def _inproj(x2, g, sc, sh, w_bf, seq):
    t, d = x2.shape
    n = w_bf.shape[1]
    tm = 1024
    tn = n // 4 if (n // 4) % (2 * LANES) == 0 else 1024
    per_batch = seq // tm
    return pl.pallas_call(
        functools.partial(_inproj_kernel, chunk=256),
        grid=(t // tm, n // tn),
        in_specs=[
            pl.BlockSpec((tm, d), lambda i, j: (i, 0)),
            pl.BlockSpec((1, d), lambda i, j: (0, 0)),
            pl.BlockSpec((1, 1, d), lambda i, j: (i // per_batch, 0, 0)),
            pl.BlockSpec((1, 1, d), lambda i, j: (i // per_batch, 0, 0)),
            pl.BlockSpec((d, tn), lambda i, j: (0, j)),
        ],
        out_specs=pl.BlockSpec((tm, tn), lambda i, j: (i, j)),
        out_shape=jax.ShapeDtypeStruct((t, n), BF16),
        scratch_shapes=[pltpu.VMEM((tm, d), BF16)],
        compiler_params=_cparams(("arbitrary", "arbitrary")),
        name="inproj",
    )(x2, g.reshape(1, d), sc, sh, w_bf)


def _sb_kernel(q_ref, k_ref, v_ref, o_ref, *, group):
    seq = q_ref.shape[0]
    blk = LANES
    scale = HEAD_DIM ** -0.5 * LOG2_E
    row = lax.broadcasted_iota(I32, (blk, blk), 0)
    col = lax.broadcasted_iota(I32, (blk, blk), 1)
    strict = col < row
    tr = lax.broadcasted_iota(I32, (2 * blk, 2 * blk), 0) % blk
    tc = lax.broadcasted_iota(I32, (2 * blk, 2 * blk), 1)
    tri = jnp.where((tc >= blk) | (tr > tc), 1.0, 0.0).astype(BF16)

    def key_rows(j):
        return pl.ds(pl.multiple_of(j * blk, blk), blk)

    def logits(qb, kb):
        z = lax.dot_general(qb, kb, (((1,), (1,)), ((), ())), preferred_element_type=F32) * scale
        sp = jnp.maximum(z, 0.0) + jnp.log2(1.0 + jnp.exp2(-jnp.abs(z)))
        return -sp, z - sp

    def suffix_sums(log_surv):
        hi = log_surv.astype(BF16)
        lo = (log_surv - hi.astype(F32)).astype(BF16)
        r = jnp.dot(jnp.concatenate([hi, lo], axis=1), tri, preferred_element_type=F32)
        return r[:, :blk], r[:, blk:]

    def tile(qb, j, g, carry, acc):
        log_surv, log_beta = logits(qb, k_ref[key_rows(j), g * blk:(g + 1) * blk])
        after, total = suffix_sums(log_surv)
        a = jnp.exp2(log_beta + after + carry)
        acc = acc + jnp.dot(a.astype(BF16), v_ref[key_rows(j), g * blk:(g + 1) * blk],
                            preferred_element_type=F32)
        return carry + total, acc

    def group_start(qbs, i):
        n_near = 3
        js = [i, jnp.maximum(i - 1, 0), jnp.maximum(i - 2, 0)]
        ivec = jnp.full((blk, blk), i, I32)
        masks = [strict, ivec >= 1, ivec >= 2]
        log_betas, parts = [], []
        for g in range(group):
            cols = slice(g * blk, (g + 1) * blk)
            kcat = jnp.concatenate([k_ref[key_rows(j), cols] for j in js], axis=0)
            log_surv, log_beta = logits(qbs[g], kcat)
            log_betas.append(log_beta)
            for m in range(n_near):
                parts.append(jnp.where(masks[m], log_surv[:, m * blk:(m + 1) * blk], 0.0))
        after, total = suffix_sums(jnp.concatenate(parts, axis=0))
        out = []
        for g in range(group):
            cols = slice(g * blk, (g + 1) * blk)
            carry = jnp.zeros((blk, blk), F32)
            weights = []
            for m in range(n_near):
                rows = slice((g * n_near + m) * blk, (g * n_near + m + 1) * blk)
                a = jnp.exp2(log_betas[g][:, m * blk:(m + 1) * blk] + after[rows] + carry)
                weights.append(jnp.where(masks[m], a, 0.0).astype(BF16))
                carry = carry + total[rows]
            vcat = jnp.concatenate([v_ref[key_rows(j), cols] for j in js], axis=0)
            acc = jnp.dot(jnp.concatenate(weights, axis=1), vcat, preferred_element_type=F32)
            out.append((carry, acc))
        return out

    q_blocks = q_ref.shape[0] // blk

    def q_block(local, c):
        i = pl.program_id(1) * q_blocks + local
        rows = pl.ds(pl.multiple_of(local * blk, blk), blk)
        qbs = [q_ref[rows, g * blk:(g + 1) * blk] for g in range(group)]
        first = group_start(qbs, i)
        carries = tuple(f[0] for f in first)
        accs = tuple(f[1] for f in first)

        def cond(st):
            j, crs, _ = st
            top = functools.reduce(jnp.maximum, crs)
            return jnp.logical_and(j >= 0, jnp.max(top) > SB_SKIP_BOUND)

        def body(st):
            j, crs, acs = st
            nxt = [tile(qbs[g], j, g, crs[g], acs[g]) for g in range(group)]
            return j - 1, tuple(n[0] for n in nxt), tuple(n[1] for n in nxt)

        _, _, accs = lax.while_loop(cond, body, (i - 3, carries, accs))
        for g in range(group):
            o_ref[rows, g * blk:(g + 1) * blk] = accs[g].astype(BF16)
        return c

    lax.fori_loop(0, q_blocks, q_block, 0)


SB_ROW_TILE = 512


def _sb_attention(u3, n_heads):
    b, s, _ = u3.shape
    w = n_heads * HEAD_DIM
    tq = min(SB_ROW_TILE, s)
    return pl.pallas_call(
        functools.partial(_sb_kernel, group=n_heads),
        grid=(b, s // tq),
        in_specs=[pl.BlockSpec((None, tq, w), lambda bi, qi: (bi, qi, 0)),
                  pl.BlockSpec((None, s, w), lambda bi, qi: (bi, 0, 1)),
                  pl.BlockSpec((None, s, w), lambda bi, qi: (bi, 0, 2))],
        out_specs=pl.BlockSpec((None, tq, w), lambda bi, qi: (bi, qi, 0)),
        out_shape=jax.ShapeDtypeStruct((b, s, w), BF16),
        compiler_params=_cparams(("arbitrary", "arbitrary")),
        name="sb_attn",
    )(u3, u3, u3)


def _ret_kernel(q_ref, k_ref, v_ref, g_ref, cos_ref, sin_ref, gn_ref, o_ref, *, chunk, unroll):
    seq = q_ref.shape[0]
    dh = HEAD_DIM
    head = pl.program_id(1)

    def log_gamma(shape):
        hv = jnp.full(shape, head, I32).astype(F32)
        return jnp.log(1.0 - jnp.exp2(-5.0 - hv))

    diff = (lax.broadcasted_iota(I32, (chunk, chunk), 0)
            - lax.broadcasted_iota(I32, (chunk, chunk), 1)).astype(F32)
    lower = diff >= 0.0
    dmat = jnp.where(lower, jnp.exp(jnp.where(lower, diff, 0.0) * log_gamma((chunk, chunk))), 0.0)
    pos = lax.broadcasted_iota(I32, (chunk, dh), 0).astype(F32)
    k_decay = jnp.exp((chunk - 1.0 - pos) * log_gamma((chunk, dh)))
    q_decay = jnp.exp((pos + 1.0) * log_gamma((chunk, dh)))
    chunk_decay = jnp.exp(chunk * log_gamma((dh, dh)))
    gn = gn_ref[...]
    scale = dh ** -0.5

    def rope(t, cs, sn):
        return t * cs + pltpu.roll(t, dh // 2, 1) * sn

    def body(n, state):
        rows = pl.ds(pl.multiple_of(n * chunk, chunk), chunk)
        cs = cos_ref[rows, :]
        sn = sin_ref[rows, :]
        qc = rope(q_ref[rows, :].astype(F32), cs, sn)
        kc = rope(k_ref[rows, :].astype(F32), cs, sn) * scale
        vc = v_ref[rows, :]
        scores = lax.dot_general(qc.astype(BF16), kc.astype(BF16), (((1,), (1,)), ((), ())),
                                 preferred_element_type=F32) * dmat
        intra = jnp.dot(scores.astype(BF16), vc, preferred_element_type=F32)
        cross = jnp.dot((qc * q_decay).astype(BF16), state.astype(BF16), preferred_element_type=F32)
        out = intra + cross
        kd_t = jnp.transpose(kc * k_decay).astype(BF16)
        state = state * chunk_decay + jnp.dot(kd_t, vc, preferred_element_type=F32)
        ms = jnp.mean(out * out, axis=-1, keepdims=True)
        y = out * lax.rsqrt(ms + EPS) * gn * _silu(g_ref[rows, :].astype(F32))
        o_ref[rows, :] = y.astype(BF16)
        return state

    lax.fori_loop(0, seq // chunk, body, jnp.zeros((dh, dh), F32), unroll=unroll)


RET_CHUNK = 256
RET_UNROLL = 2


def _retention(u3, cos2, sin2, gn, n_heads, col0):
    b, s, _ = u3.shape
    spec = lambda off: pl.BlockSpec((None, s, HEAD_DIM), lambda bi, h: (bi, 0, col0 + off + h))
    return pl.pallas_call(
        functools.partial(_ret_kernel, chunk=RET_CHUNK, unroll=RET_UNROLL),
        grid=(b, n_heads),
        in_specs=[spec(0), spec(n_heads), spec(2 * n_heads), spec(3 * n_heads),
                  pl.BlockSpec((s, HEAD_DIM), lambda bi, h: (0, 0)),
                  pl.BlockSpec((s, HEAD_DIM), lambda bi, h: (0, 0)),
                  pl.BlockSpec((1, HEAD_DIM), lambda bi, h: (0, h))],
        out_specs=pl.BlockSpec((None, s, HEAD_DIM), lambda bi, h: (bi, 0, h)),
        out_shape=jax.ShapeDtypeStruct((b, s, n_heads * HEAD_DIM), BF16),
        compiler_params=_cparams(("arbitrary", "arbitrary")),
        name="retention",
    )(u3, u3, u3, u3, cos2, sin2, gn.reshape(1, -1))


def _outproj_kernel(sb_ref, rt_ref, wa_ref, wb_ref, x_ref, g1_ref, n2_ref, sc_ref, sh_ref, wr_ref,
                    x1_ref, h2_ref, hp_ref, lt_ref, *, chunk):
    d = x_ref.shape[1]
    half = d // 2
    g1 = g1_ref[0]
    n2 = n2_ref[...]
    sc = sc_ref[0]
    sh = sh_ref[0]
    for r in range(x_ref.shape[0] // chunk):
        rows = pl.ds(r * chunk, chunk)
        o = jnp.dot(sb_ref[rows, :], wa_ref[...], preferred_element_type=F32)
        o = o + jnp.dot(rt_ref[rows, :], wb_ref[...], preferred_element_type=F32)
        x1 = x_ref[rows, :] + g1 * o
        x1_ref[rows, :] = x1
        h2 = _norm_mod(x1, n2, sc, sh).astype(BF16)
        h2_ref[rows, :] = h2
        lt_ref[:, rows] = lax.dot_general(wr_ref[...], h2, (((1,), (1,)), ((), ())),
                                          preferred_element_type=F32)
        words = _pack_pair(h2[:, :half].astype(F32), h2[:, half:].astype(F32))
        for j in range(ROW_WORDS):
            hp_ref[pl.ds(r * chunk * ROW_WORDS + j, chunk, stride=ROW_WORDS), :] = (
                words[:, j * LANES:(j + 1) * LANES])


def _outproj(sb2, rt2, wa, wb, x2, g1, n2, sc2, sh2, wr_t, seq):
    t, d = x2.shape
    tm = 512
    per_batch = seq // tm
    mod = pl.BlockSpec((1, 1, d), lambda i: (i // per_batch, 0, 0))
    return pl.pallas_call(
        functools.partial(_outproj_kernel, chunk=512),
        grid=(t // tm,),
        in_specs=[
            pl.BlockSpec((tm, sb2.shape[1]), lambda i: (i, 0)),
            pl.BlockSpec((tm, rt2.shape[1]), lambda i: (i, 0)),
            pl.BlockSpec(wa.shape, lambda i: (0, 0)),
            pl.BlockSpec(wb.shape, lambda i: (0, 0)),
            pl.BlockSpec((tm, d), lambda i: (i, 0)),
            mod,
            pl.BlockSpec((1, d), lambda i: (0, 0)),
            mod, mod,
            pl.BlockSpec(wr_t.shape, lambda i: (0, 0)),
        ],
        out_specs=[
            pl.BlockSpec((tm, d), lambda i: (i, 0)),
            pl.BlockSpec((tm, d), lambda i: (i, 0)),
            pl.BlockSpec((tm * ROW_WORDS, LANES), lambda i: (i, 0)),
            pl.BlockSpec((N_EXPERTS, tm), lambda i: (0, i)),
        ],
        out_shape=[
            jax.ShapeDtypeStruct((t, d), F32),
            jax.ShapeDtypeStruct((t, d), BF16),
            jax.ShapeDtypeStruct((t * ROW_WORDS, LANES), U32),
            jax.ShapeDtypeStruct((N_EXPERTS, t), F32),
        ],
        compiler_params=_cparams(("arbitrary",)),
        name="outproj",
    )(sb2, rt2, wa, wb, x2, g1, n2.reshape(1, d), sc2, sh2, wr_t)


def _beats(row, allv, row_idx, idx):
    return (row > allv) | ((row == allv) & (row_idx < idx))


def _route_kernel(lt_ref, bias_ref, w_ref, rank_ref, cnt_ref, carry_ref):
    tr = lt_ref.shape[1]
    gsz = N_EXPERTS // N_GROUPS
    neg_inf = jnp.float32(-jnp.inf)

    @pl.when(pl.program_id(0) == 0)
    def _():
        carry_ref[...] = jnp.zeros_like(carry_ref)

    scores = jax.nn.sigmoid(lt_ref[...])
    choice = scores + bias_ref[...]
    sub = lax.broadcasted_iota(I32, (gsz, tr), 0)
    gs_rows = []
    for g in range(N_GROUPS):
        cg = choice[g * gsz:(g + 1) * gsz, :]
        m1 = jnp.max(cg, axis=0, keepdims=True)
        first = jnp.min(jnp.where(cg == m1, sub, gsz), axis=0, keepdims=True)
        m2 = jnp.max(jnp.where(sub == first, neg_inf, cg), axis=0, keepdims=True)
        gs_rows.append(m1 + m2)
    gs = jnp.concatenate(gs_rows, axis=0)
    gi = lax.broadcasted_iota(I32, (N_GROUPS, tr), 0)
    grank = jnp.zeros((N_GROUPS, tr), I32)
    for g in range(N_GROUPS):
        grank = grank + _beats(gs[g:g + 1, :], gs, g, gi).astype(I32)
    gmask = grank < TOPK_GROUP
    emask = jnp.concatenate(
        [jnp.broadcast_to(gmask[g:g + 1, :], (gsz, tr)) for g in range(N_GROUPS)], axis=0)
    masked = jnp.where(emask, choice, neg_inf)
    ei = lax.broadcasted_iota(I32, (N_EXPERTS, tr), 0)
    erank = jnp.zeros((N_EXPERTS, tr), I32)
    for e in range(N_EXPERTS):
        erank = erank + _beats(masked[e:e + 1, :], masked, e, ei).astype(I32)
    sel = erank < TOP_K
    ssel = jnp.where(sel, scores, 0.0)
    denom = jnp.sum(ssel, axis=0, keepdims=True)
    w_ref[...] = ssel / denom * ROUTED_SCALE

    self = sel.astype(F32)
    upper = (lax.broadcasted_iota(I32, (tr, tr), 0) < lax.broadcasted_iota(I32, (tr, tr), 1))
    prefix = jnp.dot(self.astype(BF16), upper.astype(BF16), preferred_element_type=F32)
    carry = carry_ref[...]
    rank = prefix + carry[:, 0:1]
    rank_ref[...] = jnp.where(sel, rank, -1.0).astype(I32)
    carry = carry + jnp.sum(self, axis=1, keepdims=True)
    carry_ref[...] = carry
    cnt_ref[...] = carry


def _route(lt, bias):
    e, t = lt.shape
    tr = min(512, t)
    return pl.pallas_call(
        _route_kernel,
        grid=(t // tr,),
        in_specs=[pl.BlockSpec((e, tr), lambda i: (0, i)),
                  pl.BlockSpec((e, 1), lambda i: (0, 0))],
        out_specs=[pl.BlockSpec((e, tr), lambda i: (0, i)),
                   pl.BlockSpec((e, tr), lambda i: (0, i)),
                   pl.BlockSpec((e, LANES), lambda i: (0, 0))],
        out_shape=[jax.ShapeDtypeStruct((e, t), F32),
                   jax.ShapeDtypeStruct((e, t), I32),
                   jax.ShapeDtypeStruct((e, LANES), F32)],
        scratch_shapes=[pltpu.VMEM((e, LANES), F32)],
        compiler_params=_cparams(("arbitrary",)),
        name="route",
    )(lt, bias.reshape(e, 1))


def _plan_kernel(w_ref, rank_ref, cnt_ref, pos_ref, wc_ref, seg_ref):
    cnt = cnt_ref[...]
    nblk = jnp.floor((cnt + (MOE_BLOCK - 1.0)) * (1.0 / MOE_BLOCK))
    lower = (lax.broadcasted_iota(I32, (N_EXPERTS, N_EXPERTS), 1)
             < lax.broadcasted_iota(I32, (N_EXPERTS, N_EXPERTS), 0)).astype(BF16)
    bstart = jnp.dot(lower, nblk.astype(BF16), preferred_element_type=F32)
    bend = bstart + nblk

    @pl.when(pl.program_id(0) == 0)
    def _():
        seg_ref[...] = jnp.concatenate(
            [bstart[:, 0:1] * MOE_BLOCK, cnt[:, 0:1], nblk[:, 0:1] * MOE_BLOCK, bend[:, 0:1]],
            axis=1).astype(I32)

    rank = rank_ref[...]
    sel = rank >= 0
    pos = bstart[:, 0:1] * MOE_BLOCK + rank.astype(F32)
    slot = jnp.dot(lower, sel.astype(BF16), preferred_element_type=F32)
    w = w_ref[...]
    pos_rows, w_rows = [], []
    for k in range(TOP_K):
        m = sel & (slot == float(k))
        pos_rows.append(jnp.sum(jnp.where(m, pos, 0.0), axis=0, keepdims=True))
        w_rows.append(jnp.sum(jnp.where(m, w, 0.0), axis=0, keepdims=True))
    pos_ref[...] = jnp.concatenate(pos_rows, axis=0).astype(I32)
    wc_ref[...] = jnp.concatenate(w_rows, axis=0)


def _plan(w, rank, cnt):
    e, t = w.shape
    tr = min(1024, t)
    return pl.pallas_call(
        _plan_kernel,
        grid=(t // tr,),
        in_specs=[pl.BlockSpec((e, tr), lambda i: (0, i)),
                  pl.BlockSpec((e, tr), lambda i: (0, i)),
                  pl.BlockSpec((e, LANES), lambda i: (0, 0))],
        out_specs=[pl.BlockSpec((TOP_K, tr), lambda i: (0, i)),
                   pl.BlockSpec((TOP_K, tr), lambda i: (0, i)),
                   pl.BlockSpec((e, 4), lambda i: (0, 0))],
        out_shape=[jax.ShapeDtypeStruct((TOP_K, t), I32),
                   jax.ShapeDtypeStruct((TOP_K, t), F32),
                   jax.ShapeDtypeStruct((e, 4), I32)],
        compiler_params=_cparams(("arbitrary",)),
        name="plan",
    )(w, rank, cnt)


def _row_copy(src, src_row, dst, dst_row, sem):
    return pltpu.make_async_copy(src.at[pl.ds(src_row * ROW_WORDS, ROW_WORDS), :],
                                 dst.at[pl.ds(dst_row * ROW_WORDS, ROW_WORDS), :], sem)


def _dispatch_kernel(pos_ref, seg_ref, hp_ref, xs_ref, zero_ref, sem, zsem):
    tq = pos_ref.shape[0] // TOP_K
    blk_words = MOE_BLOCK * ROW_WORDS
    n_blocks = xs_ref.shape[0] // blk_words

    @pl.when(pl.program_id(0) == 0)
    def _():
        zero_ref[...] = jnp.zeros_like(zero_ref)
        zrow = zero_ref.at[pl.ds(0, ROW_WORDS), :]

        def per_expert(e, n):
            start = seg_ref[e, 0] + seg_ref[e, 1]
            npad = seg_ref[e, 2] - seg_ref[e, 1]

            def fill(r, c):
                pltpu.make_async_copy(
                    zrow, xs_ref.at[pl.ds((start + r) * ROW_WORDS, ROW_WORDS), :], zsem).start()
                return c

            lax.fori_loop(0, npad, fill, 0)
            return n + npad

        n_pad_rows = lax.fori_loop(0, N_EXPERTS, per_expert, 0)

        def drain_row(r, c):
            pltpu.make_async_copy(zrow, xs_ref.at[pl.ds(0, ROW_WORDS), :], zsem).wait()
            return c

        lax.fori_loop(0, n_pad_rows, drain_row, 0)
        n_used = seg_ref[N_EXPERTS - 1, 3]

        def fill_block(b, c):
            pltpu.make_async_copy(
                zero_ref, xs_ref.at[pl.ds(b * blk_words, blk_words), :], zsem).start()
            return c

        lax.fori_loop(n_used, n_blocks, fill_block, 0)

        def drain_block(b, c):
            pltpu.make_async_copy(zero_ref, xs_ref.at[pl.ds(0, blk_words), :], zsem).wait()
            return c

        lax.fori_loop(n_used, n_blocks, drain_block, 0)

    def issue(t, c):
        for k in range(TOP_K):
            _row_copy(hp_ref, t, xs_ref, pos_ref[t * TOP_K + k], sem).start(priority=k % 2)
        return c

    lax.fori_loop(0, tq, issue, 0)
    for k in range(TOP_K):
        pltpu.make_async_copy(hp_ref, xs_ref.at[pl.ds(0, tq * ROW_WORDS), :], sem).wait()


def _dispatch(pos_tm, seg, hp, n_rows):
    t = pos_tm.shape[0] // TOP_K
    tq = min(1024, t)
    return pl.pallas_call(
        _dispatch_kernel,
        grid=(t // tq,),
        in_specs=[pl.BlockSpec((tq * TOP_K,), lambda i: (i,), memory_space=pltpu.SMEM),
                  pl.BlockSpec(seg.shape, lambda i: (0, 0), memory_space=pltpu.SMEM),
                  pl.BlockSpec((tq * ROW_WORDS, LANES), lambda i: (i, 0))],
        out_specs=pl.BlockSpec(memory_space=pl.ANY),
        out_shape=jax.ShapeDtypeStruct((n_rows * ROW_WORDS, LANES), U32),
        scratch_shapes=[pltpu.VMEM((MOE_BLOCK * ROW_WORDS, LANES), U32),
                        pltpu.SemaphoreType.DMA(()), pltpu.SemaphoreType.DMA(())],
        compiler_params=_cparams(("arbitrary",)),
        name="dispatch",
    )(pos_tm, seg, hp)


def _expert_kernel(seg_ref, xs_ref, wg_ref, wu_ref, wd_ref, ys_ref,
                   wg32, wu32, wd32, wgb, wub, wdb, xbuf, ybuf, wsem, xsem, ysem):
    e = pl.program_id(0)
    n_exp = pl.num_programs(0)
    last_step = e == n_exp - 1

    def weight_copies(ex):
        slot = ex & 1
        return [pltpu.make_async_copy(src.at[ex], dst.at[slot], wsem.at[slot])
                for src, dst in ((wg_ref, wg32), (wu_ref, wu32), (wd_ref, wd32))]

    @pl.when(e == 0)
    def _():
        for cp in weight_copies(0):
            cp.start()

    for cp in weight_copies(e):
        cp.wait()
    rows = MOE_BLOCK
    blk_words = MOE_BLOCK * ROW_WORDS
    n_blocks = ys_ref.shape[0] // blk_words
    first = lax.div(seg_ref[e * 4], MOE_BLOCK)
    end = seg_ref[e * 4 + 3]
    n_used = seg_ref[(pl.num_programs(0) - 1) * 4 + 3]

    def x_copy(g):
        slot = g & 1
        return pltpu.make_async_copy(xs_ref.at[pl.ds(g * blk_words, blk_words), :],
                                     xbuf.at[slot], xsem.at[slot])

    def y_copy(g):
        slot = g & 1
        return pltpu.make_async_copy(ybuf.at[slot],
                                     ys_ref.at[pl.ds(g * blk_words, blk_words), :], ysem.at[slot])

    @pl.when(jnp.logical_and(e == 0, n_used > 0))
    def _():
        x_copy(0).start(priority=1)

    @pl.when(end > first)
    def _():
        wslot = e & 1
        wgb[...] = wg32[wslot].astype(BF16)
        wub[...] = wu32[wslot].astype(BF16)
        wdb[...] = wd32[wslot].astype(BF16)

    @pl.when(e + 1 < n_exp)
    def _():
        for cp in weight_copies(e + 1):
            cp.start()

    @pl.when(end > first)
    def _():
        def block(g, c):
            slot = g & 1
            x_copy(g).wait()

            @pl.when(g + 1 < n_used)
            def _():
                x_copy(g + 1).start(priority=1)

            @pl.when(g >= 2)
            def _():
                y_copy(g - 2).wait()

            xin = xbuf.at[slot]
            his, los = [], []
            for j in range(ROW_WORDS):
                hi, lo = _unpack_pair(xin[pl.ds(j, rows, stride=ROW_WORDS), :])
                his.append(hi.astype(BF16))
                los.append(lo.astype(BF16))
            xb = jnp.concatenate(his + los, axis=1)
            a = jnp.dot(xb, wgb[...], preferred_element_type=F32)
            u = jnp.dot(xb, wub[...], preferred_element_type=F32)
            hid = (_silu(a) * u).astype(BF16)
            out = jnp.dot(hid, wdb[...], preferred_element_type=F32).astype(BF16).astype(F32)
            half = out.shape[1] // 2
            words = _pack_pair(out[:, :half], out[:, half:])
            yout = ybuf.at[slot]
            for j in range(ROW_WORDS):
                yout[pl.ds(j, rows, stride=ROW_WORDS), :] = words[:, j * LANES:(j + 1) * LANES]
            y_copy(g).start(priority=1)
            return c

        lax.fori_loop(first, end, block, 0)

    @pl.when(last_step)
    def _():
        @pl.when(n_used >= 2)
        def _():
            y_copy(n_used - 2).wait()

        @pl.when(n_used >= 1)
        def _():
            y_copy(n_used - 1).wait()

        ybuf[0] = jnp.zeros(ybuf.shape[1:], ybuf.dtype)

        def fill(g, c):
            pltpu.make_async_copy(ybuf.at[0], ys_ref.at[pl.ds(g * blk_words, blk_words), :],
                                  ysem.at[0]).start()
            return c

        lax.fori_loop(n_used, n_blocks, fill, 0)

        def drain(g, c):
            pltpu.make_async_copy(ybuf.at[0], ys_ref.at[pl.ds(0, blk_words), :], ysem.at[0]).wait()
            return c

        lax.fori_loop(n_used, n_blocks, drain, 0)


def _experts(seg_flat, xs, w_gate, w_up, w_down):
    n_exp, d, f = w_gate.shape
    blk_words = MOE_BLOCK * ROW_WORDS
    grid_spec = pltpu.PrefetchScalarGridSpec(
        num_scalar_prefetch=1,
        grid=(n_exp,),
        in_specs=[
            pl.BlockSpec(memory_space=pl.ANY),
            pl.BlockSpec(memory_space=pl.ANY),
            pl.BlockSpec(memory_space=pl.ANY),
            pl.BlockSpec(memory_space=pl.ANY),
        ],
        out_specs=pl.BlockSpec(memory_space=pl.ANY),
        scratch_shapes=[pltpu.VMEM((2, d, f), F32), pltpu.VMEM((2, d, f), F32), pltpu.VMEM((2, f, d), F32),
                        pltpu.VMEM((d, f), BF16), pltpu.VMEM((d, f), BF16), pltpu.VMEM((f, d), BF16),
                        pltpu.VMEM((2, blk_words, LANES), U32), pltpu.VMEM((2, blk_words, LANES), U32),
                        pltpu.SemaphoreType.DMA((2,)), pltpu.SemaphoreType.DMA((2,)),
                        pltpu.SemaphoreType.DMA((2,))],
    )
    return pl.pallas_call(
        _expert_kernel,
        grid_spec=grid_spec,
        out_shape=jax.ShapeDtypeStruct(xs.shape, U32),
        compiler_params=_cparams(("arbitrary",)),
        name="experts",
    )(seg_flat, xs, w_gate, w_up, w_down)


def _combine_kernel(pos_ref, nxt_ref, ys_ref, wc_ref, h2_ref, x1_ref, sg_ref, su_ref, sd_ref,
                    g2_ref, nf_ref, scf_ref, shf_ref, o_ref, buf, sem):
    i = pl.program_id(0)
    tq = wc_ref.shape[1]
    slot = i & 1

    def gather(p_ref, to_slot):
        def issue(t, c):
            for k in range(TOP_K):
                pltpu.make_async_copy(
                    ys_ref.at[pl.ds(p_ref[t * TOP_K + k] * ROW_WORDS, ROW_WORDS), :],
                    buf.at[to_slot, k, pl.ds(t * ROW_WORDS, ROW_WORDS), :],
                    sem.at[to_slot]).start(priority=k % 2)
            return c

        lax.fori_loop(0, tq, issue, 0)

    @pl.when(i == 0)
    def _():
        gather(pos_ref, 0)

    @pl.when(i + 1 < pl.num_programs(0))
    def _():
        gather(nxt_ref, 1 - slot)

    h2 = h2_ref[...]
    a = jnp.dot(h2, sg_ref[...], preferred_element_type=F32)
    u = jnp.dot(h2, su_ref[...], preferred_element_type=F32)
    shared = jnp.dot((_silu(a) * u).astype(BF16), sd_ref[...], preferred_element_type=F32)

    for k in range(TOP_K):
        pltpu.make_async_copy(ys_ref.at[pl.ds(0, tq * ROW_WORDS), :], buf.at[slot, k],
                              sem.at[slot]).wait()

    wt = jnp.transpose(wc_ref[...])
    his, los = [], []
    for j in range(ROW_WORDS):
        acc_hi = jnp.zeros((tq, LANES), F32)
        acc_lo = jnp.zeros((tq, LANES), F32)
        for k in range(TOP_K):
            hi, lo = _unpack_pair(buf.at[slot, k][pl.ds(j, tq, stride=ROW_WORDS), :])
            wk = wt[:, k:k + 1]
            acc_hi = acc_hi + wk * hi
            acc_lo = acc_lo + wk * lo
        his.append(acc_hi)
        los.append(acc_lo)
    y = jnp.concatenate(his + los, axis=1) + shared
    x2 = x1_ref[...] + g2_ref[0] * y
    o_ref[...] = _norm_mod(x2, nf_ref[...], scf_ref[0], shf_ref[0])


def _combine(pos_tm, ys, w_c, h2, x1, sg, su, sd, g2, nf, scf, shf, seq):
    t, d = x1.shape
    tq = 128
    n_tiles = t // tq
    per_batch = seq // tq
    mod = pl.BlockSpec((1, 1, d), lambda i: (i // per_batch, 0, 0))
    full = lambda a: pl.BlockSpec(a.shape, lambda i: (0, 0))
    return pl.pallas_call(
        _combine_kernel,
        grid=(n_tiles,),
        in_specs=[
            pl.BlockSpec((tq * TOP_K,), lambda i: (i,), memory_space=pltpu.SMEM),
            pl.BlockSpec((tq * TOP_K,), lambda i: (jnp.minimum(i + 1, n_tiles - 1),),
                         memory_space=pltpu.SMEM),
            pl.BlockSpec(memory_space=pl.ANY),
            pl.BlockSpec((TOP_K, tq), lambda i: (0, i)),
            pl.BlockSpec((tq, d), lambda i: (i, 0)),
            pl.BlockSpec((tq, d), lambda i: (i, 0)),
            full(sg), full(su), full(sd),
            mod,
            pl.BlockSpec((1, d), lambda i: (0, 0)),
            mod, mod,
        ],
        out_specs=pl.BlockSpec((tq, d), lambda i: (i, 0)),
        out_shape=jax.ShapeDtypeStruct((t, d), F32),
        scratch_shapes=[pltpu.VMEM((2, TOP_K, tq * ROW_WORDS, LANES), U32),
                        pltpu.SemaphoreType.DMA((2,))],
        compiler_params=_cparams(("arbitrary",)),
        name="combine",
    )(pos_tm, pos_tm, ys, w_c, h2, x1, sg, su, sd, g2, nf.reshape(1, d), scf, shf)


def _rope_tables(seq):
    pos = jnp.arange(seq, dtype=F32)
    inv_freq = ROPE_BASE ** (-jnp.arange(0, HEAD_DIM, 2, dtype=F32) / HEAD_DIM)
    ang = pos[:, None] * inv_freq[None, :]
    cos, sin = jnp.cos(ang), jnp.sin(ang)
    return jnp.concatenate([cos, cos], axis=1), jnp.concatenate([-sin, sin], axis=1)


def kernel(x, c, w_ada, b_ada, norm1_g, w_in, ret_gn_g, w_out, norm2_g, w_router, router_bias,
           w_gate, w_up, w_down, ws_gate, ws_up, ws_down, w_ada_final, b_ada_final, norm_f_g):
    bsz, seq, d = x.shape
    t = bsz * seq
    depth = w_ada.shape[0]
    n_heads = d // (2 * HEAD_DIM)
    sb_w = n_heads * HEAD_DIM
    cos2, sin2 = _rope_tables(seq)
    c_pad = jnp.pad(c, ((0, 8 - bsz % 8), (0, 0)))
    n_rows = t * TOP_K + N_EXPERTS * MOE_BLOCK
    as_mod = lambda v: v.reshape(bsz, 1, d)

    x2 = x.reshape(t, d)
    for l in range(depth):
        mod = _ada(c_pad, w_ada[l], b_ada[l])[:bsz]
        sh1, sc1, g1, sh2, sc2, g2 = [as_mod(m) for m in jnp.split(mod, 6, axis=-1)]
        u = _inproj(x2, norm1_g[l], sc1, sh1, w_in[l].astype(BF16), seq)
        u3 = u.reshape(bsz, seq, -1)
        sb = _sb_attention(u3, n_heads)
        rt = _retention(u3, cos2, sin2, ret_gn_g[l], n_heads, 3 * n_heads)
        wo = w_out[l].astype(BF16)
        x1, h2, hp, lt = _outproj(sb.reshape(t, sb_w), rt.reshape(t, -1), wo[:sb_w], wo[sb_w:], x2,
                                  g1, norm2_g[l], sc2, sh2, jnp.transpose(w_router[l]).astype(BF16), seq)
        w_e, rank_e, cnt = _route(lt, router_bias[l])
        pos_c, w_c, seg = _plan(w_e, rank_e, cnt)
        pos_tm = jnp.transpose(pos_c).reshape(-1)
        xs = _dispatch(pos_tm, seg, hp, n_rows)
        ys = _experts(seg.reshape(-1), xs, w_gate[l], w_up[l], w_down[l])
        if l + 1 < depth:
            raise NotImplementedError("only the final layer fuses the output norm")
        modf = _ada(c_pad, w_ada_final, b_ada_final)[:bsz]
        shf, scf = [as_mod(m) for m in jnp.split(modf, 2, axis=-1)]
        x2 = _combine(pos_tm, ys, w_c, h2, x1, ws_gate[l].astype(BF16), ws_up[l].astype(BF16),
                      ws_down[l].astype(BF16), g2, norm_f_g, scf, shf, seq)
    return x2.reshape(bsz, seq, d)
```

```python
import functools

import jax
import jax.numpy as jnp
from jax import lax
from jax.experimental import pallas as pl
from jax.experimental.pallas import tpu as pltpu

F32 = jnp.float32
BF16 = jnp.bfloat16
I32 = jnp.int32
U32 = jnp.uint32

HEAD_DIM = 128
N_EXPERTS = 64
TOP_K = 8
N_GROUPS = 8
TOPK_GROUP = 4
ROUTED_SCALE = 2.5
EPS = 1e-6
ROPE_BASE = 10000.0

LANES = 128
ROW_WORDS = 8
MOE_BLOCK = 256
V7X_VMEM_LIMIT = 56 * 1024 * 1024
LOG2_E = 1.4426950408889634
SB_SKIP_BOUND = -115.0 * LOG2_E


def _cparams(sem, vmem=V7X_VMEM_LIMIT):
    return pltpu.CompilerParams(dimension_semantics=sem, vmem_limit_bytes=vmem)


def _silu(v):
    return v * jax.nn.sigmoid(v)


def _pack_pair(hi_f32, lo_f32):
    hw = pltpu.bitcast(hi_f32, U32) & jnp.uint32(0xFFFF0000)
    lw = pltpu.bitcast(lo_f32, U32) >> 16
    return hw | lw


def _unpack_pair(w):
    hi = pltpu.bitcast(w & jnp.uint32(0xFFFF0000), F32)
    lo = pltpu.bitcast(w << 16, F32)
    return hi, lo


def _ada_kernel(c_ref, w_ref, b_ref, o_ref):
    cs = _silu(c_ref[...]).astype(BF16)
    o_ref[...] = jnp.dot(cs, w_ref[...].astype(BF16), preferred_element_type=F32) + b_ref[...]


def _ada(c_pad, w, b):
    rows, d = c_pad.shape
    n = w.shape[1]
    tn = 1024
    return pl.pallas_call(
        _ada_kernel,
        grid=(n // tn,),
        in_specs=[
            pl.BlockSpec((rows, d), lambda j: (0, 0)),
            pl.BlockSpec((d, tn), lambda j: (0, j)),
            pl.BlockSpec((1, tn), lambda j: (0, j)),
        ],
        out_specs=pl.BlockSpec((rows, tn), lambda j: (0, j)),
        out_shape=jax.ShapeDtypeStruct((rows, n), F32),
        compiler_params=_cparams(("arbitrary",)),
        name="ada",
    )(c_pad, w, b.reshape(1, n))


def _norm_mod(x, g, sc, sh):
    ms = jnp.mean(x * x, axis=-1, keepdims=True)
    return (x * lax.rsqrt(ms + EPS) * g) * (1.0 + sc) + sh


def _inproj_kernel(x_ref, g_ref, sc_ref, sh_ref, w_ref, o_ref, h_ref, *, chunk):
    first = pl.program_id(1) == 0

    @pl.when(first)
    def _():
        g = g_ref[...]
        sc = sc_ref[0]
        sh = sh_ref[0]
        for r in range(x_ref.shape[0] // chunk):
            rows = pl.ds(r * chunk, chunk)
            h = _norm_mod(x_ref[rows, :], g, sc, sh).astype(BF16)
            h_ref[rows, :] = h
            o_ref[rows, :] = jnp.dot(h, w_ref[...], preferred_element_type=F32).astype(BF16)

    @pl.when(jnp.logical_not(first))
    def _():
        o_ref[...] = jnp.dot(h_ref[...], w_ref[...], preferred_element_type=F32).astype(BF16)


def _inproj(x2, g, sc, sh, w_bf, seq):
    t, d = x2.shape
    n = w_bf.shape[1]
    tm = 1024
    tn = n // 4 if (n // 4) % (2 * LANES) == 0 else 1024
    per_batch = seq // tm
    return pl.pallas_call(
        functools.partial(_inproj_kernel, chunk=256),
        grid=(t // tm, n // tn),
        in_specs=[
            pl.BlockSpec((tm, d), lambda i, j: (i, 0)),
            pl.BlockSpec((1, d), lambda i, j: (0, 0)),
            pl.BlockSpec((1, 1, d), lambda i, j: (i // per_batch, 0, 0)),
            pl.BlockSpec((1, 1, d), lambda i, j: (i // per_batch, 0, 0)),
            pl.BlockSpec((d, tn), lambda i, j: (0, j)),
        ],
        out_specs=pl.BlockSpec((tm, tn), lambda i, j: (i, j)),
        out_shape=jax.ShapeDtypeStruct((t, n), BF16),
        scratch_shapes=[pltpu.VMEM((tm, d), BF16)],
        compiler_params=_cparams(("arbitrary", "arbitrary")),
        name="inproj",
    )(x2, g.reshape(1, d), sc, sh, w_bf)


def _sb_kernel(q_ref, k_ref, v_ref, o_ref, *, group):
    seq = q_ref.shape[0]
    blk = LANES
    scale = HEAD_DIM ** -0.5 * LOG2_E
    row = lax.broadcasted_iota(I32, (blk, blk), 0)
    col = lax.broadcasted_iota(I32, (blk, blk), 1)
    strict = col < row
    tr = lax.broadcasted_iota(I32, (2 * blk, 2 * blk), 0) % blk
    tc = lax.broadcasted_iota(I32, (2 * blk, 2 * blk), 1)
    tri = jnp.where((tc >= blk) | (tr > tc), 1.0, 0.0).astype(BF16)

    def key_rows(j):
        return pl.ds(pl.multiple_of(j * blk, blk), blk)

    def logits(qb, kb):
        z = lax.dot_general(qb, kb, (((1,), (1,)), ((), ())), preferred_element_type=F32) * scale
        sp = jnp.maximum(z, 0.0) + jnp.log2(1.0 + jnp.exp2(-jnp.abs(z)))
        return -sp, z - sp

    def suffix_sums(log_surv):
        hi = log_surv.astype(BF16)
        lo = (log_surv - hi.astype(F32)).astype(BF16)
        r = jnp.dot(jnp.concatenate([hi, lo], axis=1), tri, preferred_element_type=F32)
        return r[:, :blk], r[:, blk:]

    def tile(qb, j, g, carry, acc):
        log_surv, log_beta = logits(qb, k_ref[key_rows(j), g * blk:(g + 1) * blk])
        after, total = suffix_sums(log_surv)
        a = jnp.exp2(log_beta + after + carry)
        acc = acc + jnp.dot(a.astype(BF16), v_ref[key_rows(j), g * blk:(g + 1) * blk],
                            preferred_element_type=F32)
        return carry + total, acc

    def group_start(qbs, i):
        n_near = 3
        js = [i, jnp.maximum(i - 1, 0), jnp.maximum(i - 2, 0)]
        ivec = jnp.full((blk, blk), i, I32)
        masks = [strict, ivec >= 1, ivec >= 2]
        log_betas, parts = [], []
        for g in range(group):
            cols = slice(g * blk, (g + 1) * blk)
            kcat = jnp.concatenate([k_ref[key_rows(j), cols] for j in js], axis=0)
            log_surv, log_beta = logits(qbs[g], kcat)
            log_betas.append(log_beta)
            for m in range(n_near):
                parts.append(jnp.where(masks[m], log_surv[:, m * blk:(m + 1) * blk], 0.0))
        after, total = suffix_sums(jnp.concatenate(parts, axis=0))
        out = []
        for g in range(group):
            cols = slice(g * blk, (g + 1) * blk)
            carry = jnp.zeros((blk, blk), F32)
            weights = []
            for m in range(n_near):
                rows = slice((g * n_near + m) * blk, (g * n_near + m + 1) * blk)
                a = jnp.exp2(log_betas[g][:, m * blk:(m + 1) * blk] + after[rows] + carry)
                weights.append(jnp.where(masks[m], a, 0.0).astype(BF16))
                carry = carry + total[rows]
            vcat = jnp.concatenate([v_ref[key_rows(j), cols] for j in js], axis=0)
            acc = jnp.dot(jnp.concatenate(weights, axis=1), vcat, preferred_element_type=F32)
            out.append((carry, acc))
        return out

    q_blocks = q_ref.shape[0] // blk

    def q_block(local, c):
        i = pl.program_id(1) * q_blocks + local
        rows = pl.ds(pl.multiple_of(local * blk, blk), blk)
        qbs = [q_ref[rows, g * blk:(g + 1) * blk] for g in range(group)]
        first = group_start(qbs, i)
        carries = tuple(f[0] for f in first)
        accs = tuple(f[1] for f in first)

        def cond(st):
            j, crs, _ = st
            top = functools.reduce(jnp.maximum, crs)
            return jnp.logical_and(j >= 0, jnp.max(top) > SB_SKIP_BOUND)

        def body(st):
            j, crs, acs = st
            nxt = [tile(qbs[g], j, g, crs[g], acs[g]) for g in range(group)]
            return j - 1, tuple(n[0] for n in nxt), tuple(n[1] for n in nxt)

        _, _, accs = lax.while_loop(cond, body, (i - 3, carries, accs))
        for g in range(group):
            o_ref[rows, g * blk:(g + 1) * blk] = accs[g].astype(BF16)
        return c

    lax.fori_loop(0, q_blocks, q_block, 0)


SB_ROW_TILE = 512


def _sb_attention(u3, n_heads):
    b, s, _ = u3.shape
    w = n_heads * HEAD_DIM
    tq = min(SB_ROW_TILE, s)
    return pl.pallas_call(
        functools.partial(_sb_kernel, group=n_heads),
        grid=(b, s // tq),
        in_specs=[pl.BlockSpec((None, tq, w), lambda bi, qi: (bi, qi, 0)),
                  pl.BlockSpec((None, s, w), lambda bi, qi: (bi, 0, 1)),
                  pl.BlockSpec((None, s, w), lambda bi, qi: (bi, 0, 2))],
        out_specs=pl.BlockSpec((None, tq, w), lambda bi, qi: (bi, qi, 0)),
        out_shape=jax.ShapeDtypeStruct((b, s, w), BF16),
        compiler_params=_cparams(("arbitrary", "arbitrary")),
        name="sb_attn",
    )(u3, u3, u3)


def _ret_kernel(q_ref, k_ref, v_ref, g_ref, cos_ref, sin_ref, gn_ref, o_ref, *, chunk, unroll):
    seq = q_ref.shape[0]
    dh = HEAD_DIM
    head = pl.program_id(1)

    def log_gamma(shape):
        hv = jnp.full(shape, head, I32).astype(F32)
        return jnp.log(1.0 - jnp.exp2(-5.0 - hv))

    diff = (lax.broadcasted_iota(I32, (chunk, chunk), 0)
            - lax.broadcasted_iota(I32, (chunk, chunk), 1)).astype(F32)
    lower = diff >= 0.0
    dmat = jnp.where(lower, jnp.exp(jnp.where(lower, diff, 0.0) * log_gamma((chunk, chunk))), 0.0)
    pos = lax.broadcasted_iota(I32, (chunk, dh), 0).astype(F32)
    k_decay = jnp.exp((chunk - 1.0 - pos) * log_gamma((chunk, dh)))
    q_decay = jnp.exp((pos + 1.0) * log_gamma((chunk, dh)))
    chunk_decay = jnp.exp(chunk * log_gamma((dh, dh)))
    gn = gn_ref[...]
    scale = dh ** -0.5

    def rope(t, cs, sn):
        return t * cs + pltpu.roll(t, dh // 2, 1) * sn

    def body(n, state):
        rows = pl.ds(pl.multiple_of(n * chunk, chunk), chunk)
        cs = cos_ref[rows, :]
        sn = sin_ref[rows, :]
        qc = rope(q_ref[rows, :].astype(F32), cs, sn)
        kc = rope(k_ref[rows, :].astype(F32), cs, sn) * scale
        vc = v_ref[rows, :]
        scores = lax.dot_general(qc.astype(BF16), kc.astype(BF16), (((1,), (1,)), ((), ())),
                                 preferred_element_type=F32) * dmat
        intra = jnp.dot(scores.astype(BF16), vc, preferred_element_type=F32)
        cross = jnp.dot((qc * q_decay).astype(BF16), state.astype(BF16), preferred_element_type=F32)
        out = intra + cross
        kd_t = jnp.transpose(kc * k_decay).astype(BF16)
        state = state * chunk_decay + jnp.dot(kd_t, vc, preferred_element_type=F32)
        ms = jnp.mean(out * out, axis=-1, keepdims=True)
        y = out * lax.rsqrt(ms + EPS) * gn * _silu(g_ref[rows, :].astype(F32))
        o_ref[rows, :] = y.astype(BF16)
        return state

    lax.fori_loop(0, seq // chunk, body, jnp.zeros((dh, dh), F32), unroll=unroll)


RET_CHUNK = 256
RET_UNROLL = 2


def _retention(u3, cos2, sin2, gn, n_heads, col0):
    b, s, _ = u3.shape
    spec = lambda off: pl.BlockSpec((None, s, HEAD_DIM), lambda bi, h: (bi, 0, col0 + off + h))
    return pl.pallas_call(
        functools.partial(_ret_kernel, chunk=RET_CHUNK, unroll=RET_UNROLL),
        grid=(b, n_heads),
        in_specs=[spec(0), spec(n_heads), spec(2 * n_heads), spec(3 * n_heads),
                  pl.BlockSpec((s, HEAD_DIM), lambda bi, h: (0, 0)),
                  pl.BlockSpec((s, HEAD_DIM), lambda bi, h: (0, 0)),
                  pl.BlockSpec((1, HEAD_DIM), lambda bi, h: (0, h))],
        out_specs=pl.BlockSpec((None, s, HEAD_DIM), lambda bi, h: (bi, 0, h)),
        out_shape=jax.ShapeDtypeStruct((b, s, n_heads * HEAD_DIM), BF16),
        compiler_params=_cparams(("arbitrary", "arbitrary")),
        name="retention",
    )(u3, u3, u3, u3, cos2, sin2, gn.reshape(1, -1))


def _outproj_kernel(sb_ref, rt_ref, wa_ref, wb_ref, x_ref, g1_ref, n2_ref, sc_ref, sh_ref, wr_ref,
                    x1_ref, h2_ref, hp_ref, lt_ref, *, chunk):
    d = x_ref.shape[1]
    half = d // 2
    g1 = g1_ref[0]
    n2 = n2_ref[...]
    sc = sc_ref[0]
    sh = sh_ref[0]
    for r in range(x_ref.shape[0] // chunk):
        rows = pl.ds(r * chunk, chunk)
        o = jnp.dot(sb_ref[rows, :], wa_ref[...], preferred_element_type=F32)
        o = o + jnp.dot(rt_ref[rows, :], wb_ref[...], preferred_element_type=F32)
        x1 = x_ref[rows, :] + g1 * o
        x1_ref[rows, :] = x1
        h2 = _norm_mod(x1, n2, sc, sh).astype(BF16)
        h2_ref[rows, :] = h2
        lt_ref[:, rows] = lax.dot_general(wr_ref[...], h2, (((1,), (1,)), ((), ())),
                                          preferred_element_type=F32)
        words = _pack_pair(h2[:, :half].astype(F32), h2[:, half:].astype(F32))
        for j in range(ROW_WORDS):
            hp_ref[pl.ds(r * chunk * ROW_WORDS + j, chunk, stride=ROW_WORDS), :] = (
                words[:, j * LANES:(j + 1) * LANES])


def _outproj(sb2, rt2, wa, wb, x2, g1, n2, sc2, sh2, wr_t, seq):
    t, d = x2.shape
    tm = 512
    per_batch = seq // tm
    mod = pl.BlockSpec((1, 1, d), lambda i: (i // per_batch, 0, 0))
    return pl.pallas_call(
        functools.partial(_outproj_kernel, chunk=512),
        grid=(t // tm,),
        in_specs=[
            pl.BlockSpec((tm, sb2.shape[1]), lambda i: (i, 0)),
            pl.BlockSpec((tm, rt2.shape[1]), lambda i: (i, 0)),
            pl.BlockSpec(wa.shape, lambda i: (0, 0)),
            pl.BlockSpec(wb.shape, lambda i: (0, 0)),
            pl.BlockSpec((tm, d), lambda i: (i, 0)),
            mod,
            pl.BlockSpec((1, d), lambda i: (0, 0)),
            mod, mod,
            pl.BlockSpec(wr_t.shape, lambda i: (0, 0)),
        ],
        out_specs=[
            pl.BlockSpec((tm, d), lambda i: (i, 0)),
            pl.BlockSpec((tm, d), lambda i: (i, 0)),
            pl.BlockSpec((tm * ROW_WORDS, LANES), lambda i: (i, 0)),
            pl.BlockSpec((N_EXPERTS, tm), lambda i: (0, i)),
        ],
        out_shape=[
            jax.ShapeDtypeStruct((t, d), F32),
            jax.ShapeDtypeStruct((t, d), BF16),
            jax.ShapeDtypeStruct((t * ROW_WORDS, LANES), U32),
            jax.ShapeDtypeStruct((N_EXPERTS, t), F32),
        ],
        compiler_params=_cparams(("arbitrary",)),
        name="outproj",
    )(sb2, rt2, wa, wb, x2, g1, n2.reshape(1, d), sc2, sh2, wr_t)


def _beats(row, allv, row_idx, idx):
    return (row > allv) | ((row == allv) & (row_idx < idx))


def _route_kernel(lt_ref, bias_ref, w_ref, rank_ref, cnt_ref, carry_ref):
    tr = lt_ref.shape[1]
    gsz = N_EXPERTS // N_GROUPS
    neg_inf = jnp.float32(-jnp.inf)

    @pl.when(pl.program_id(0) == 0)
    def _():
        carry_ref[...] = jnp.zeros_like(carry_ref)

    scores = jax.nn.sigmoid(lt_ref[...])
    choice = scores + bias_ref[...]
    sub = lax.broadcasted_iota(I32, (gsz, tr), 0)
    gs_rows = []
    for g in range(N_GROUPS):
        cg = choice[g * gsz:(g + 1) * gsz, :]
        m1 = jnp.max(cg, axis=0, keepdims=True)
        first = jnp.min(jnp.where(cg == m1, sub, gsz), axis=0, keepdims=True)
        m2 = jnp.max(jnp.where(sub == first, neg_inf, cg), axis=0, keepdims=True)
        gs_rows.append(m1 + m2)
    gs = jnp.concatenate(gs_rows, axis=0)
    gi = lax.broadcasted_iota(I32, (N_GROUPS, tr), 0)
    grank = jnp.zeros((N_GROUPS, tr), I32)
    for g in range(N_GROUPS):
        grank = grank + _beats(gs[g:g + 1, :], gs, g, gi).astype(I32)
    gmask = grank < TOPK_GROUP
    emask = jnp.concatenate(
        [jnp.broadcast_to(gmask[g:g + 1, :], (gsz, tr)) for g in range(N_GROUPS)], axis=0)
    masked = jnp.where(emask, choice, neg_inf)
    ei = lax.broadcasted_iota(I32, (N_EXPERTS, tr), 0)
    erank = jnp.zeros((N_EXPERTS, tr), I32)
    for e in range(N_EXPERTS):
        erank = erank + _beats(masked[e:e + 1, :], masked, e, ei).astype(I32)
    sel = erank < TOP_K
    ssel = jnp.where(sel, scores, 0.0)
    denom = jnp.sum(ssel, axis=0, keepdims=True)
    w_ref[...] = ssel / denom * ROUTED_SCALE

    self = sel.astype(F32)
    upper = (lax.broadcasted_iota(I32, (tr, tr), 0) < lax.broadcasted_iota(I32, (tr, tr), 1))
    prefix = jnp.dot(self.astype(BF16), upper.astype(BF16), preferred_element_type=F32)
    carry = carry_ref[...]
    rank = prefix + carry[:, 0:1]
    rank_ref[...] = jnp.where(sel, rank, -1.0).astype(I32)
    carry = carry + jnp.sum(self, axis=1, keepdims=True)
    carry_ref[...] = carry
    cnt_ref[...] = carry


def _route(lt, bias):
    e, t = lt.shape
    tr = min(512, t)
    return pl.pallas_call(
        _route_kernel,
        grid=(t // tr,),
        in_specs=[pl.BlockSpec((e, tr), lambda i: (0, i)),
                  pl.BlockSpec((e, 1), lambda i: (0, 0))],
        out_specs=[pl.BlockSpec((e, tr), lambda i: (0, i)),
                   pl.BlockSpec((e, tr), lambda i: (0, i)),
                   pl.BlockSpec((e, LANES), lambda i: (0, 0))],
        out_shape=[jax.ShapeDtypeStruct((e, t), F32),
                   jax.ShapeDtypeStruct((e, t), I32),
                   jax.ShapeDtypeStruct((e, LANES), F32)],
        scratch_shapes=[pltpu.VMEM((e, LANES), F32)],
        compiler_params=_cparams(("arbitrary",)),
        name="route",
    )(lt, bias.reshape(e, 1))


def _plan_kernel(w_ref, rank_ref, cnt_ref, pos_ref, wc_ref, seg_ref):
    cnt = cnt_ref[...]
    nblk = jnp.floor((cnt + (MOE_BLOCK - 1.0)) * (1.0 / MOE_BLOCK))
    lower = (lax.broadcasted_iota(I32, (N_EXPERTS, N_EXPERTS), 1)
             < lax.broadcasted_iota(I32, (N_EXPERTS, N_EXPERTS), 0)).astype(BF16)
    bstart = jnp.dot(lower, nblk.astype(BF16), preferred_element_type=F32)
    bend = bstart + nblk

    @pl.when(pl.program_id(0) == 0)
    def _():
        seg_ref[...] = jnp.concatenate(
            [bstart[:, 0:1] * MOE_BLOCK, cnt[:, 0:1], nblk[:, 0:1] * MOE_BLOCK, bend[:, 0:1]],
            axis=1).astype(I32)

    rank = rank_ref[...]
    sel = rank >= 0
    pos = bstart[:, 0:1] * MOE_BLOCK + rank.astype(F32)
    slot = jnp.dot(lower, sel.astype(BF16), preferred_element_type=F32)
    w = w_ref[...]
    pos_rows, w_rows = [], []
    for k in range(TOP_K):
        m = sel & (slot == float(k))
        pos_rows.append(jnp.sum(jnp.where(m, pos, 0.0), axis=0, keepdims=True))
        w_rows.append(jnp.sum(jnp.where(m, w, 0.0), axis=0, keepdims=True))
    pos_ref[...] = jnp.concatenate(pos_rows, axis=0).astype(I32)
    wc_ref[...] = jnp.concatenate(w_rows, axis=0)


def _plan(w, rank, cnt):
    e, t = w.shape
    tr = min(1024, t)
    return pl.pallas_call(
        _plan_kernel,
        grid=(t // tr,),
        in_specs=[pl.BlockSpec((e, tr), lambda i: (0, i)),
                  pl.BlockSpec((e, tr), lambda i: (0, i)),
                  pl.BlockSpec((e, LANES), lambda i: (0, 0))],
        out_specs=[pl.BlockSpec((TOP_K, tr), lambda i: (0, i)),
                   pl.BlockSpec((TOP_K, tr), lambda i: (0, i)),
                   pl.BlockSpec((e, 4), lambda i: (0, 0))],
        out_shape=[jax.ShapeDtypeStruct((TOP_K, t), I32),
                   jax.ShapeDtypeStruct((TOP_K, t), F32),
                   jax.ShapeDtypeStruct((e, 4), I32)],
        compiler_params=_cparams(("arbitrary",)),
        name="plan",
    )(w, rank, cnt)


def _row_copy(src, src_row, dst, dst_row, sem):
    return pltpu.make_async_copy(src.at[pl.ds(src_row * ROW_WORDS, ROW_WORDS), :],
                                 dst.at[pl.ds(dst_row * ROW_WORDS, ROW_WORDS), :], sem)


def _dispatch_kernel(pos_ref, seg_ref, hp_ref, xs_ref, zero_ref, sem, zsem):
    tq = pos_ref.shape[0] // TOP_K
    blk_words = MOE_BLOCK * ROW_WORDS
    n_blocks = xs_ref.shape[0] // blk_words

    @pl.when(pl.program_id(0) == 0)
    def _():
        zero_ref[...] = jnp.zeros_like(zero_ref)
        zrow = zero_ref.at[pl.ds(0, ROW_WORDS), :]

        def per_expert(e, n):
            start = seg_ref[e, 0] + seg_ref[e, 1]
            npad = seg_ref[e, 2] - seg_ref[e, 1]

            def fill(r, c):
                pltpu.make_async_copy(
                    zrow, xs_ref.at[pl.ds((start + r) * ROW_WORDS, ROW_WORDS), :], zsem).start()
                return c

            lax.fori_loop(0, npad, fill, 0)
            return n + npad

        n_pad_rows = lax.fori_loop(0, N_EXPERTS, per_expert, 0)

        def drain_row(r, c):
            pltpu.make_async_copy(zrow, xs_ref.at[pl.ds(0, ROW_WORDS), :], zsem).wait()
            return c

        lax.fori_loop(0, n_pad_rows, drain_row, 0)
        n_used = seg_ref[N_EXPERTS - 1, 3]

        def fill_block(b, c):
            pltpu.make_async_copy(
                zero_ref, xs_ref.at[pl.ds(b * blk_words, blk_words), :], zsem).start()
            return c

        lax.fori_loop(n_used, n_blocks, fill_block, 0)

        def drain_block(b, c):
            pltpu.make_async_copy(zero_ref, xs_ref.at[pl.ds(0, blk_words), :], zsem).wait()
            return c

        lax.fori_loop(n_used, n_blocks, drain_block, 0)

    def issue(t, c):
        for k in range(TOP_K):
            _row_copy(hp_ref, t, xs_ref, pos_ref[t * TOP_K + k], sem).start(priority=k % 2)
        return c

    lax.fori_loop(0, tq, issue, 0)
    for k in range(TOP_K):
        pltpu.make_async_copy(hp_ref, xs_ref.at[pl.ds(0, tq * ROW_WORDS), :], sem).wait()


def _dispatch(pos_tm, seg, hp, n_rows):
    t = pos_tm.shape[0] // TOP_K
    tq = min(1024, t)
    return pl.pallas_call(
        _dispatch_kernel,
        grid=(t // tq,),
        in_specs=[pl.BlockSpec((tq * TOP_K,), lambda i: (i,), memory_space=pltpu.SMEM),
                  pl.BlockSpec(seg.shape, lambda i: (0, 0), memory_space=pltpu.SMEM),
                  pl.BlockSpec((tq * ROW_WORDS, LANES), lambda i: (i, 0))],
        out_specs=pl.BlockSpec(memory_space=pl.ANY),
        out_shape=jax.ShapeDtypeStruct((n_rows * ROW_WORDS, LANES), U32),
        scratch_shapes=[pltpu.VMEM((MOE_BLOCK * ROW_WORDS, LANES), U32),
                        pltpu.SemaphoreType.DMA(()), pltpu.SemaphoreType.DMA(())],
        compiler_params=_cparams(("arbitrary",)),
        name="dispatch",
    )(pos_tm, seg, hp)


def _expert_kernel(seg_ref, xs_ref, wg_ref, wu_ref, wd_ref, ys_ref,
                   wg32, wu32, wd32, wgb, wub, wdb, xbuf, ybuf, wsem, xsem, ysem):
    e = pl.program_id(0)
    n_exp = pl.num_programs(0)
    last_step = e == n_exp - 1

    def weight_copies(ex):
        slot = ex & 1
        return [pltpu.make_async_copy(src.at[ex], dst.at[slot], wsem.at[slot])
                for src, dst in ((wg_ref, wg32), (wu_ref, wu32), (wd_ref, wd32))]

    @pl.when(e == 0)
    def _():
        for cp in weight_copies(0):
            cp.start()

    for cp in weight_copies(e):
        cp.wait()
    rows = MOE_BLOCK
    blk_words = MOE_BLOCK * ROW_WORDS
    n_blocks = ys_ref.shape[0] // blk_words
    first = lax.div(seg_ref[e * 4], MOE_BLOCK)
    end = seg_ref[e * 4 + 3]
    n_used = seg_ref[(pl.num_programs(0) - 1) * 4 + 3]

    x_slots = xbuf.shape[0]
    x_ahead = x_slots - 2

    def x_copy(g):
        slot = g & (x_slots - 1)
        return pltpu.make_async_copy(xs_ref.at[pl.ds(g * blk_words, blk_words), :],
                                     xbuf.at[slot], xsem.at[slot])

    def y_copy(g):
        slot = g & 1
        return pltpu.make_async_copy(ybuf.at[slot],
                                     ys_ref.at[pl.ds(g * blk_words, blk_words), :], ysem.at[slot])

    for g0 in range(x_ahead):
        @pl.when(jnp.logical_and(e == 0, n_used > g0))
        def _():
            x_copy(g0).start()

    @pl.when(end > first)
    def _():
        wslot = e & 1
        wgb[...] = wg32[wslot].astype(BF16)
        wub[...] = wu32[wslot].astype(BF16)
        wdb[...] = wd32[wslot].astype(BF16)

    @pl.when(e + 1 < n_exp)
    def _():
        for cp in weight_copies(e + 1):
            cp.start(priority=1)

    @pl.when(end > first)
    def _():
        def block(g, c):
            slot = g & 1
            x_copy(g).wait()

            @pl.when(g + x_ahead < n_used)
            def _():
                x_copy(g + x_ahead).start()

            @pl.when(g >= 2)
            def _():
                y_copy(g - 2).wait()

            xin = xbuf.at[g & (x_slots - 1)]
            his, los = [], []
            for j in range(ROW_WORDS):
                hi, lo = _unpack_pair(xin[pl.ds(j, rows, stride=ROW_WORDS), :])
                his.append(hi.astype(BF16))
                los.append(lo.astype(BF16))
            xb = jnp.concatenate(his + los, axis=1)
            a = jnp.dot(xb, wgb[...], preferred_element_type=F32)
            u = jnp.dot(xb, wub[...], preferred_element_type=F32)
            hid = (_silu(a) * u).astype(BF16)
            out = jnp.dot(hid, wdb[...], preferred_element_type=F32).astype(BF16).astype(F32)
            half = out.shape[1] // 2
            words = _pack_pair(out[:, :half], out[:, half:])
            yout = ybuf.at[slot]
            for j in range(ROW_WORDS):
                yout[pl.ds(j, rows, stride=ROW_WORDS), :] = words[:, j * LANES:(j + 1) * LANES]
            y_copy(g).start()
            return c

        lax.fori_loop(first, end, block, 0)

    @pl.when(last_step)
    def _():
        @pl.when(n_used >= 2)
        def _():
            y_copy(n_used - 2).wait()

        @pl.when(n_used >= 1)
        def _():
            y_copy(n_used - 1).wait()

        ybuf[0] = jnp.zeros(ybuf.shape[1:], ybuf.dtype)

        def fill(g, c):
            pltpu.make_async_copy(ybuf.at[0], ys_ref.at[pl.ds(g * blk_words, blk_words), :],
                                  ysem.at[0]).start()
            return c

        lax.fori_loop(n_used, n_blocks, fill, 0)

        def drain(g, c):
            pltpu.make_async_copy(ybuf.at[0], ys_ref.at[pl.ds(0, blk_words), :], ysem.at[0]).wait()
            return c

        lax.fori_loop(n_used, n_blocks, drain, 0)


def _experts(seg_flat, xs, w_gate, w_up, w_down):
    n_exp, d, f = w_gate.shape
    blk_words = MOE_BLOCK * ROW_WORDS
    grid_spec = pltpu.PrefetchScalarGridSpec(
        num_scalar_prefetch=1,
        grid=(n_exp,),
        in_specs=[
            pl.BlockSpec(memory_space=pl.ANY),
            pl.BlockSpec(memory_space=pl.ANY),
            pl.BlockSpec(memory_space=pl.ANY),
            pl.BlockSpec(memory_space=pl.ANY),
        ],
        out_specs=pl.BlockSpec(memory_space=pl.ANY),
        scratch_shapes=[pltpu.VMEM((2, d, f), F32), pltpu.VMEM((2, d, f), F32), pltpu.VMEM((2, f, d), F32),
                        pltpu.VMEM((d, f), BF16), pltpu.VMEM((d, f), BF16), pltpu.VMEM((f, d), BF16),
                        pltpu.VMEM((4, blk_words, LANES), U32), pltpu.VMEM((2, blk_words, LANES), U32),
                        pltpu.SemaphoreType.DMA((2,)), pltpu.SemaphoreType.DMA((4,)),
                        pltpu.SemaphoreType.DMA((2,))],
    )
    return pl.pallas_call(
        _expert_kernel,
        grid_spec=grid_spec,
        out_shape=jax.ShapeDtypeStruct(xs.shape, U32),
        compiler_params=_cparams(("arbitrary",)),
        name="experts",
    )(seg_flat, xs, w_gate, w_up, w_down)


def _combine_kernel(pos_ref, nxt_ref, ys_ref, wc_ref, h2_ref, x1_ref, sg_ref, su_ref, sd_ref,
                    g2_ref, nf_ref, scf_ref, shf_ref, o_ref, buf, sem):
    i = pl.program_id(0)
    tq = wc_ref.shape[1]
    slot = i & 1

    def gather(p_ref, to_slot):
        def issue(t, c):
            for k in range(TOP_K):
                pltpu.make_async_copy(
                    ys_ref.at[pl.ds(p_ref[t * TOP_K + k] * ROW_WORDS, ROW_WORDS), :],
                    buf.at[to_slot, k, pl.ds(t * ROW_WORDS, ROW_WORDS), :],
                    sem.at[to_slot]).start(priority=k % 2)
            return c

        lax.fori_loop(0, tq, issue, 0)

    @pl.when(i == 0)
    def _():
        gather(pos_ref, 0)

    @pl.when(i + 1 < pl.num_programs(0))
    def _():
        gather(nxt_ref, 1 - slot)

    h2 = h2_ref[...]
    a = jnp.dot(h2, sg_ref[...], preferred_element_type=F32)
    u = jnp.dot(h2, su_ref[...], preferred_element_type=F32)
    shared = jnp.dot((_silu(a) * u).astype(BF16), sd_ref[...], preferred_element_type=F32)

    for k in range(TOP_K):
        pltpu.make_async_copy(ys_ref.at[pl.ds(0, tq * ROW_WORDS), :], buf.at[slot, k],
                              sem.at[slot]).wait()

    wt = jnp.transpose(wc_ref[...])
    his, los = [], []
    for j in range(ROW_WORDS):
        acc_hi = jnp.zeros((tq, LANES), F32)
        acc_lo = jnp.zeros((tq, LANES), F32)
        for k in range(TOP_K):
            hi, lo = _unpack_pair(buf.at[slot, k][pl.ds(j, tq, stride=ROW_WORDS), :])
            wk = wt[:, k:k + 1]
            acc_hi = acc_hi + wk * hi
            acc_lo = acc_lo + wk * lo
        his.append(acc_hi)
        los.append(acc_lo)
    y = jnp.concatenate(his + los, axis=1) + shared
    x2 = x1_ref[...] + g2_ref[0] * y
    o_ref[...] = _norm_mod(x2, nf_ref[...], scf_ref[0], shf_ref[0])


def _combine(pos_tm, ys, w_c, h2, x1, sg, su, sd, g2, nf, scf, shf, seq):
    t, d = x1.shape
    tq = 128
    n_tiles = t // tq
    per_batch = seq // tq
    mod = pl.BlockSpec((1, 1, d), lambda i: (i // per_batch, 0, 0))
    full = lambda a: pl.BlockSpec(a.shape, lambda i: (0, 0))
    return pl.pallas_call(
        _combine_kernel,
        grid=(n_tiles,),
        in_specs=[
            pl.BlockSpec((tq * TOP_K,), lambda i: (i,), memory_space=pltpu.SMEM),
            pl.BlockSpec((tq * TOP_K,), lambda i: (jnp.minimum(i + 1, n_tiles - 1),),
                         memory_space=pltpu.SMEM),
            pl.BlockSpec(memory_space=pl.ANY),
            pl.BlockSpec((TOP_K, tq), lambda i: (0, i)),
            pl.BlockSpec((tq, d), lambda i: (i, 0)),
            pl.BlockSpec((tq, d), lambda i: (i, 0)),
            full(sg), full(su), full(sd),
            mod,
            pl.BlockSpec((1, d), lambda i: (0, 0)),
            mod, mod,
        ],
        out_specs=pl.BlockSpec((tq, d), lambda i: (i, 0)),
        out_shape=jax.ShapeDtypeStruct((t, d), F32),
        scratch_shapes=[pltpu.VMEM((2, TOP_K, tq * ROW_WORDS, LANES), U32),
                        pltpu.SemaphoreType.DMA((2,))],
        compiler_params=_cparams(("arbitrary",)),
        name="combine",
    )(pos_tm, pos_tm, ys, w_c, h2, x1, sg, su, sd, g2, nf.reshape(1, d), scf, shf)


def _rope_tables(seq):
    pos = jnp.arange(seq, dtype=F32)
    inv_freq = ROPE_BASE ** (-jnp.arange(0, HEAD_DIM, 2, dtype=F32) / HEAD_DIM)
    ang = pos[:, None] * inv_freq[None, :]
    cos, sin = jnp.cos(ang), jnp.sin(ang)
    return jnp.concatenate([cos, cos], axis=1), jnp.concatenate([-sin, sin], axis=1)


def kernel(x, c, w_ada, b_ada, norm1_g, w_in, ret_gn_g, w_out, norm2_g, w_router, router_bias,
           w_gate, w_up, w_down, ws_gate, ws_up, ws_down, w_ada_final, b_ada_final, norm_f_g):
    bsz, seq, d = x.shape
    t = bsz * seq
    depth = w_ada.shape[0]
    n_heads = d // (2 * HEAD_DIM)
    sb_w = n_heads * HEAD_DIM
    cos2, sin2 = _rope_tables(seq)
    c_pad = jnp.pad(c, ((0, 8 - bsz % 8), (0, 0)))
    n_rows = t * TOP_K + N_EXPERTS * MOE_BLOCK
    as_mod = lambda v: v.reshape(bsz, 1, d)

    x2 = x.reshape(t, d)
    for l in range(depth):
        mod = _ada(c_pad, w_ada[l], b_ada[l])[:bsz]
        sh1, sc1, g1, sh2, sc2, g2 = [as_mod(m) for m in jnp.split(mod, 6, axis=-1)]
        u = _inproj(x2, norm1_g[l], sc1, sh1, w_in[l].astype(BF16), seq)
        u3 = u.reshape(bsz, seq, -1)
        sb = _sb_attention(u3, n_heads)
        rt = _retention(u3, cos2, sin2, ret_gn_g[l], n_heads, 3 * n_heads)
        wo = w_out[l].astype(BF16)
        x1, h2, hp, lt = _outproj(sb.reshape(t, sb_w), rt.reshape(t, -1), wo[:sb_w], wo[sb_w:], x2,
                                  g1, norm2_g[l], sc2, sh2, jnp.transpose(w_router[l]).astype(BF16), seq)
        w_e, rank_e, cnt = _route(lt, router_bias[l])
        pos_c, w_c, seg = _plan(w_e, rank_e, cnt)
        pos_tm = jnp.transpose(pos_c).reshape(-1)
        xs = _dispatch(pos_tm, seg, hp, n_rows)
        ys = _experts(seg.reshape(-1), xs, w_gate[l], w_up[l], w_down[l])
        if l + 1 < depth:
            raise NotImplementedError("only the final layer fuses the output norm")
        modf = _ada(c_pad, w_ada_final, b_ada_final)[:bsz]
        shf, scf = [as_mod(m) for m in jnp.split(modf, 2, axis=-1)]
        x2 = _combine(pos_tm, ys, w_c, h2, x1, ws_gate[l].astype(BF16), ws_up[l].astype(BF16),
                      ws_down[l].astype(BF16), g2, norm_f_g, scf, shf, seq)
    return x2.reshape(bsz, seq, d)
```

```python
import functools

import jax
import jax.numpy as jnp
from jax import lax
from jax.experimental import pallas as pl
from jax.experimental.pallas import tpu as pltpu

F32 = jnp.float32
BF16 = jnp.bfloat16
I32 = jnp.int32
U32 = jnp.uint32

HEAD_DIM = 128
N_EXPERTS = 64
TOP_K = 8
N_GROUPS = 8
TOPK_GROUP = 4
ROUTED_SCALE = 2.5
EPS = 1e-6
ROPE_BASE = 10000.0

LANES = 128
ROW_WORDS = 8
MOE_BLOCK = 256
V7X_VMEM_LIMIT = 56 * 1024 * 1024
LOG2_E = 1.4426950408889634
SB_SKIP_BOUND = -115.0 * LOG2_E


def _cparams(sem, vmem=V7X_VMEM_LIMIT):
    return pltpu.CompilerParams(dimension_semantics=sem, vmem_limit_bytes=vmem)


def _silu(v):
    return v * jax.nn.sigmoid(v)


def _pack_pair(hi_f32, lo_f32):
    hw = pltpu.bitcast(hi_f32, U32) & jnp.uint32(0xFFFF0000)
    lw = pltpu.bitcast(lo_f32, U32) >> 16
    return hw | lw


def _unpack_pair(w):
    hi = pltpu.bitcast(w & jnp.uint32(0xFFFF0000), F32)
    lo = pltpu.bitcast(w << 16, F32)
    return hi, lo


def _ada_kernel(c_ref, w_ref, b_ref, o_ref):
    cs = _silu(c_ref[...]).astype(BF16)
    o_ref[...] = jnp.dot(cs, w_ref[...].astype(BF16), preferred_element_type=F32) + b_ref[...]


def _ada(c_pad, w, b):
    rows, d = c_pad.shape
    n = w.shape[1]
    tn = 1024
    return pl.pallas_call(
        _ada_kernel,
        grid=(n // tn,),
        in_specs=[
            pl.BlockSpec((rows, d), lambda j: (0, 0)),
            pl.BlockSpec((d, tn), lambda j: (0, j)),
            pl.BlockSpec((1, tn), lambda j: (0, j)),
        ],
        out_specs=pl.BlockSpec((rows, tn), lambda j: (0, j)),
        out_shape=jax.ShapeDtypeStruct((rows, n), F32),
        compiler_params=_cparams(("arbitrary",)),
        name="ada",
    )(c_pad, w, b.reshape(1, n))


def _norm_mod(x, g, sc, sh):
    ms = jnp.mean(x * x, axis=-1, keepdims=True)
    return (x * lax.rsqrt(ms + EPS) * g) * (1.0 + sc) + sh


def _inproj_kernel(x_ref, g_ref, sc_ref, sh_ref, w_ref, o_ref, h_ref, *, chunk):
    first = pl.program_id(1) == 0

    @pl.when(first)
    def _():
        g = g_ref[...]
        sc = sc_ref[0]
        sh = sh_ref[0]
        for r in range(x_ref.shape[0] // chunk):
            rows = pl.ds(r * chunk, chunk)
            h = _norm_mod(x_ref[rows, :], g, sc, sh).astype(BF16)
            h_ref[rows, :] = h
            o_ref[rows, :] = jnp.dot(h, w_ref[...], preferred_element_type=F32).astype(BF16)

    @pl.when(jnp.logical_not(first))
    def _():
        o_ref[...] = jnp.dot(h_ref[...], w_ref[...], preferred_element_type=F32).astype(BF16)


def _inproj(x2, g, sc, sh, w_bf, seq):
    t, d = x2.shape
    n = w_bf.shape[1]
    tm = 1024
    tn = n // 4 if (n // 4) % (2 * LANES) == 0 else 1024
    per_batch = seq // tm
    return pl.pallas_call(
        functools.partial(_inproj_kernel, chunk=256),
        grid=(t // tm, n // tn),
        in_specs=[
            pl.BlockSpec((tm, d), lambda i, j: (i, 0)),
            pl.BlockSpec((1, d), lambda i, j: (0, 0)),
            pl.BlockSpec((1, 1, d), lambda i, j: (i // per_batch, 0, 0)),
            pl.BlockSpec((1, 1, d), lambda i, j: (i // per_batch, 0, 0)),
            pl.BlockSpec((d, tn), lambda i, j: (0, j)),
        ],
        out_specs=pl.BlockSpec((tm, tn), lambda i, j: (i, j)),
        out_shape=jax.ShapeDtypeStruct((t, n), BF16),
        scratch_shapes=[pltpu.VMEM((tm, d), BF16)],
        compiler_params=_cparams(("arbitrary", "arbitrary")),
        name="inproj",
    )(x2, g.reshape(1, d), sc, sh, w_bf)


def _sb_kernel(q_ref, k_ref, v_ref, o_ref, *, group):
    seq = q_ref.shape[0]
    blk = LANES
    scale = HEAD_DIM ** -0.5 * LOG2_E
    row = lax.broadcasted_iota(I32, (blk, blk), 0)
    col = lax.broadcasted_iota(I32, (blk, blk), 1)
    strict = col < row
    tr = lax.broadcasted_iota(I32, (2 * blk, 2 * blk), 0) % blk
    tc = lax.broadcasted_iota(I32, (2 * blk, 2 * blk), 1)
    tri = jnp.where((tc >= blk) | (tr > tc), 1.0, 0.0).astype(BF16)

    def key_rows(j):
        return pl.ds(pl.multiple_of(j * blk, blk), blk)

    def logits(qb, kb):
        z = lax.dot_general(qb, kb, (((1,), (1,)), ((), ())), preferred_element_type=F32) * scale
        sp = jnp.maximum(z, 0.0) + jnp.log2(1.0 + jnp.exp2(-jnp.abs(z)))
        return -sp, z - sp

    def suffix_sums(log_surv):
        hi = log_surv.astype(BF16)
        lo = (log_surv - hi.astype(F32)).astype(BF16)
        r = jnp.dot(jnp.concatenate([hi, lo], axis=1), tri, preferred_element_type=F32)
        return r[:, :blk], r[:, blk:]

    def tile(qb, j, g, carry, acc):
        log_surv, log_beta = logits(qb, k_ref[key_rows(j), g * blk:(g + 1) * blk])
        after, total = suffix_sums(log_surv)
        a = jnp.exp2(log_beta + after + carry)
        acc = acc + jnp.dot(a.astype(BF16), v_ref[key_rows(j), g * blk:(g + 1) * blk],
                            preferred_element_type=F32)
        return carry + total, acc

    def group_start(qbs, i):
        n_near = 3
        js = [i, jnp.maximum(i - 1, 0), jnp.maximum(i - 2, 0)]
        ivec = jnp.full((blk, blk), i, I32)
        masks = [strict, ivec >= 1, ivec >= 2]
        log_betas, parts = [], []
        for g in range(group):
            cols = slice(g * blk, (g + 1) * blk)
            kcat = jnp.concatenate([k_ref[key_rows(j), cols] for j in js], axis=0)
            log_surv, log_beta = logits(qbs[g], kcat)
            log_betas.append(log_beta)
            for m in range(n_near):
                parts.append(jnp.where(masks[m], log_surv[:, m * blk:(m + 1) * blk], 0.0))
        after, total = suffix_sums(jnp.concatenate(parts, axis=0))
        out = []
        for g in range(group):
            cols = slice(g * blk, (g + 1) * blk)
            carry = jnp.zeros((blk, blk), F32)
            weights = []
            for m in range(n_near):
                rows = slice((g * n_near + m) * blk, (g * n_near + m + 1) * blk)
                a = jnp.exp2(log_betas[g][:, m * blk:(m + 1) * blk] + after[rows] + carry)
                weights.append(jnp.where(masks[m], a, 0.0).astype(BF16))
                carry = carry + total[rows]
            vcat = jnp.concatenate([v_ref[key_rows(j), cols] for j in js], axis=0)
            acc = jnp.dot(jnp.concatenate(weights, axis=1), vcat, preferred_element_type=F32)
            out.append((carry, acc))
        return out

    q_blocks = q_ref.shape[0] // blk

    def store(rows, accs):
        for g in range(group):
            o_ref[rows, g * blk:(g + 1) * blk] = accs[g].astype(BF16)

    tops = []
    for local in range(q_blocks):
        i = pl.program_id(1) * q_blocks + local
        rows = pl.ds(local * blk, blk)
        first = group_start([q_ref[rows, g * blk:(g + 1) * blk] for g in range(group)], i)
        store(rows, [f[1] for f in first])
        tops.append(functools.reduce(jnp.maximum, [f[0] for f in first]))
    need_more = jnp.max(functools.reduce(jnp.maximum, tops)) > SB_SKIP_BOUND

    def q_block(local, c):
        i = pl.program_id(1) * q_blocks + local
        rows = pl.ds(pl.multiple_of(local * blk, blk), blk)
        qbs = [q_ref[rows, g * blk:(g + 1) * blk] for g in range(group)]
        first = group_start(qbs, i)
        carries = tuple(f[0] for f in first)
        accs = tuple(f[1] for f in first)

        def cond(st):
            j, crs, _ = st
            top = functools.reduce(jnp.maximum, crs)
            return jnp.logical_and(j >= 0, jnp.max(top) > SB_SKIP_BOUND)

        def body(st):
            j, crs, acs = st
            nxt = [tile(qbs[g], j, g, crs[g], acs[g]) for g in range(group)]
            return j - 1, tuple(n[0] for n in nxt), tuple(n[1] for n in nxt)

        _, _, accs = lax.while_loop(cond, body, (i - 3, carries, accs))
        store(rows, accs)
        return c

    @pl.when(need_more)
    def _():
        lax.fori_loop(0, q_blocks, q_block, 0)


SB_ROW_TILE = 512


def _sb_attention(u3, n_heads):
    b, s, _ = u3.shape
    w = n_heads * HEAD_DIM
    tq = min(SB_ROW_TILE, s)
    return pl.pallas_call(
        functools.partial(_sb_kernel, group=n_heads),
        grid=(b, s // tq),
        in_specs=[pl.BlockSpec((None, tq, w), lambda bi, qi: (bi, qi, 0)),
                  pl.BlockSpec((None, s, w), lambda bi, qi: (bi, 0, 1)),
                  pl.BlockSpec((None, s, w), lambda bi, qi: (bi, 0, 2))],
        out_specs=pl.BlockSpec((None, tq, w), lambda bi, qi: (bi, qi, 0)),
        out_shape=jax.ShapeDtypeStruct((b, s, w), BF16),
        compiler_params=_cparams(("arbitrary", "arbitrary")),
        name="sb_attn",
    )(u3, u3, u3)


def _ret_kernel(q_ref, k_ref, v_ref, g_ref, cos_ref, sin_ref, gn_ref, o_ref, *, chunk, unroll):
    seq = q_ref.shape[0]
    dh = HEAD_DIM
    head = pl.program_id(1)

    def log_gamma(shape):
        hv = jnp.full(shape, head, I32).astype(F32)
        return jnp.log(1.0 - jnp.exp2(-5.0 - hv))

    diff = (lax.broadcasted_iota(I32, (chunk, chunk), 0)
            - lax.broadcasted_iota(I32, (chunk, chunk), 1)).astype(F32)
    lower = diff >= 0.0
    dmat = jnp.where(lower, jnp.exp(jnp.where(lower, diff, 0.0) * log_gamma((chunk, chunk))), 0.0)
    pos = lax.broadcasted_iota(I32, (chunk, dh), 0).astype(F32)
    k_decay = jnp.exp((chunk - 1.0 - pos) * log_gamma((chunk, dh)))
    q_decay = jnp.exp((pos + 1.0) * log_gamma((chunk, dh)))
    chunk_decay = jnp.exp(chunk * log_gamma((dh, dh)))
    gn = gn_ref[...]
    scale = dh ** -0.5

    def rope(t, cs, sn):
        return t * cs + pltpu.roll(t, dh // 2, 1) * sn

    def body(n, state):
        rows = pl.ds(pl.multiple_of(n * chunk, chunk), chunk)
        cs = cos_ref[rows, :]
        sn = sin_ref[rows, :]
        qc = rope(q_ref[rows, :].astype(F32), cs, sn)
        kc = rope(k_ref[rows, :].astype(F32), cs, sn) * scale
        vc = v_ref[rows, :]
        scores = lax.dot_general(qc.astype(BF16), kc.astype(BF16), (((1,), (1,)), ((), ())),
                                 preferred_element_type=F32) * dmat
        intra = jnp.dot(scores.astype(BF16), vc, preferred_element_type=F32)
        cross = jnp.dot((qc * q_decay).astype(BF16), state.astype(BF16), preferred_element_type=F32)
        out = intra + cross
        kd_t = jnp.transpose(kc * k_decay).astype(BF16)
        state = state * chunk_decay + jnp.dot(kd_t, vc, preferred_element_type=F32)
        ms = jnp.mean(out * out, axis=-1, keepdims=True)
        y = out * lax.rsqrt(ms + EPS) * gn * _silu(g_ref[rows, :].astype(F32))
        o_ref[rows, :] = y.astype(BF16)
        return state

    lax.fori_loop(0, seq // chunk, body, jnp.zeros((dh, dh), F32), unroll=unroll)


RET_CHUNK = 512
RET_UNROLL = 2


def _retention(u3, cos2, sin2, gn, n_heads, col0):
    b, s, _ = u3.shape
    spec = lambda off: pl.BlockSpec((None, s, HEAD_DIM), lambda bi, h: (bi, 0, col0 + off + h))
    return pl.pallas_call(
        functools.partial(_ret_kernel, chunk=RET_CHUNK, unroll=RET_UNROLL),
        grid=(b, n_heads),
        in_specs=[spec(0), spec(n_heads), spec(2 * n_heads), spec(3 * n_heads),
                  pl.BlockSpec((s, HEAD_DIM), lambda bi, h: (0, 0)),
                  pl.BlockSpec((s, HEAD_DIM), lambda bi, h: (0, 0)),
                  pl.BlockSpec((1, HEAD_DIM), lambda bi, h: (0, h))],
        out_specs=pl.BlockSpec((None, s, HEAD_DIM), lambda bi, h: (bi, 0, h)),
        out_shape=jax.ShapeDtypeStruct((b, s, n_heads * HEAD_DIM), BF16),
        compiler_params=_cparams(("arbitrary", "arbitrary")),
        name="retention",
    )(u3, u3, u3, u3, cos2, sin2, gn.reshape(1, -1))


def _outproj_kernel(sb_ref, rt_ref, wa_ref, wb_ref, x_ref, g1_ref, n2_ref, sc_ref, sh_ref, wr_ref,
                    x1_ref, h2_ref, hp_ref, lt_ref, *, chunk):
    d = x_ref.shape[1]
    half = d // 2
    g1 = g1_ref[0]
    n2 = n2_ref[...]
    sc = sc_ref[0]
    sh = sh_ref[0]
    for r in range(x_ref.shape[0] // chunk):
        rows = pl.ds(r * chunk, chunk)
        o = jnp.dot(sb_ref[rows, :], wa_ref[...], preferred_element_type=F32)
        o = o + jnp.dot(rt_ref[rows, :], wb_ref[...], preferred_element_type=F32)
        x1 = x_ref[rows, :] + g1 * o
        x1_ref[rows, :] = x1
        h2 = _norm_mod(x1, n2, sc, sh).astype(BF16)
        h2_ref[rows, :] = h2
        lt_ref[:, rows] = lax.dot_general(wr_ref[...], h2, (((1,), (1,)), ((), ())),
                                          preferred_element_type=F32)
        words = _pack_pair(h2[:, :half].astype(F32), h2[:, half:].astype(F32))
        for j in range(ROW_WORDS):
            hp_ref[pl.ds(r * chunk * ROW_WORDS + j, chunk, stride=ROW_WORDS), :] = (
                words[:, j * LANES:(j + 1) * LANES])


def _outproj(sb2, rt2, wa, wb, x2, g1, n2, sc2, sh2, wr_t, seq):
    t, d = x2.shape
    tm = 512
    per_batch = seq // tm
    mod = pl.BlockSpec((1, 1, d), lambda i: (i // per_batch, 0, 0))
    return pl.pallas_call(
        functools.partial(_outproj_kernel, chunk=512),
        grid=(t // tm,),
        in_specs=[
            pl.BlockSpec((tm, sb2.shape[1]), lambda i: (i, 0)),
            pl.BlockSpec((tm, rt2.shape[1]), lambda i: (i, 0)),
            pl.BlockSpec(wa.shape, lambda i: (0, 0)),
            pl.BlockSpec(wb.shape, lambda i: (0, 0)),
            pl.BlockSpec((tm, d), lambda i: (i, 0)),
            mod,
            pl.BlockSpec((1, d), lambda i: (0, 0)),
            mod, mod,
            pl.BlockSpec(wr_t.shape, lambda i: (0, 0)),
        ],
        out_specs=[
            pl.BlockSpec((tm, d), lambda i: (i, 0)),
            pl.BlockSpec((tm, d), lambda i: (i, 0)),
            pl.BlockSpec((tm * ROW_WORDS, LANES), lambda i: (i, 0)),
            pl.BlockSpec((N_EXPERTS, tm), lambda i: (0, i)),
        ],
        out_shape=[
            jax.ShapeDtypeStruct((t, d), F32),
            jax.ShapeDtypeStruct((t, d), BF16),
            jax.ShapeDtypeStruct((t * ROW_WORDS, LANES), U32),
            jax.ShapeDtypeStruct((N_EXPERTS, t), F32),
        ],
        compiler_params=_cparams(("arbitrary",)),
        name="outproj",
    )(sb2, rt2, wa, wb, x2, g1, n2.reshape(1, d), sc2, sh2, wr_t)


def _beats(row, allv, row_idx, idx):
    return (row > allv) | ((row == allv) & (row_idx < idx))


def _route_kernel(lt_ref, bias_ref, w_ref, rank_ref, cnt_ref, carry_ref):
    tr = lt_ref.shape[1]
    gsz = N_EXPERTS // N_GROUPS
    neg_inf = jnp.float32(-jnp.inf)

    @pl.when(pl.program_id(0) == 0)
    def _():
        carry_ref[...] = jnp.zeros_like(carry_ref)

    scores = jax.nn.sigmoid(lt_ref[...])
    choice = scores + bias_ref[...]
    sub = lax.broadcasted_iota(I32, (gsz, tr), 0)
    gs_rows = []
    for g in range(N_GROUPS):
        cg = choice[g * gsz:(g + 1) * gsz, :]
        m1 = jnp.max(cg, axis=0, keepdims=True)
        first = jnp.min(jnp.where(cg == m1, sub, gsz), axis=0, keepdims=True)
        m2 = jnp.max(jnp.where(sub == first, neg_inf, cg), axis=0, keepdims=True)
        gs_rows.append(m1 + m2)
    gs = jnp.concatenate(gs_rows, axis=0)
    gi = lax.broadcasted_iota(I32, (N_GROUPS, tr), 0)
    grank = jnp.zeros((N_GROUPS, tr), I32)
    for g in range(N_GROUPS):
        grank = grank + _beats(gs[g:g + 1, :], gs, g, gi).astype(I32)
    gmask = grank < TOPK_GROUP
    emask = jnp.concatenate(
        [jnp.broadcast_to(gmask[g:g + 1, :], (gsz, tr)) for g in range(N_GROUPS)], axis=0)
    masked = jnp.where(emask, choice, neg_inf)
    ei = lax.broadcasted_iota(I32, (N_EXPERTS, tr), 0)
    erank = jnp.zeros((N_EXPERTS, tr), I32)
    for e in range(N_EXPERTS):
        erank = erank + _beats(masked[e:e + 1, :], masked, e, ei).astype(I32)
    sel = erank < TOP_K
    ssel = jnp.where(sel, scores, 0.0)
    denom = jnp.sum(ssel, axis=0, keepdims=True)
    w_ref[...] = ssel / denom * ROUTED_SCALE

    self = sel.astype(F32)
    upper = (lax.broadcasted_iota(I32, (tr, tr), 0) < lax.broadcasted_iota(I32, (tr, tr), 1))
    prefix = jnp.dot(self.astype(BF16), upper.astype(BF16), preferred_element_type=F32)
    carry = carry_ref[...]
    rank = prefix + carry[:, 0:1]
    rank_ref[...] = jnp.where(sel, rank, -1.0).astype(I32)
    carry = carry + jnp.sum(self, axis=1, keepdims=True)
    carry_ref[...] = carry
    cnt_ref[...] = carry


def _route(lt, bias):
    e, t = lt.shape
    tr = min(512, t)
    return pl.pallas_call(
        _route_kernel,
        grid=(t // tr,),
        in_specs=[pl.BlockSpec((e, tr), lambda i: (0, i)),
                  pl.BlockSpec((e, 1), lambda i: (0, 0))],
        out_specs=[pl.BlockSpec((e, tr), lambda i: (0, i)),
                   pl.BlockSpec((e, tr), lambda i: (0, i)),
                   pl.BlockSpec((e, LANES), lambda i: (0, 0))],
        out_shape=[jax.ShapeDtypeStruct((e, t), F32),
                   jax.ShapeDtypeStruct((e, t), I32),
                   jax.ShapeDtypeStruct((e, LANES), F32)],
        scratch_shapes=[pltpu.VMEM((e, LANES), F32)],
        compiler_params=_cparams(("arbitrary",)),
        name="route",
    )(lt, bias.reshape(e, 1))


def _plan_kernel(w_ref, rank_ref, cnt_ref, pos_ref, wc_ref, seg_ref):
    cnt = cnt_ref[...]
    nblk = jnp.floor((cnt + (MOE_BLOCK - 1.0)) * (1.0 / MOE_BLOCK))
    lower = (lax.broadcasted_iota(I32, (N_EXPERTS, N_EXPERTS), 1)
             < lax.broadcasted_iota(I32, (N_EXPERTS, N_EXPERTS), 0)).astype(BF16)
    bstart = jnp.dot(lower, nblk.astype(BF16), preferred_element_type=F32)
    bend = bstart + nblk

    @pl.when(pl.program_id(0) == 0)
    def _():
        seg_ref[...] = jnp.concatenate(
            [bstart[:, 0:1] * MOE_BLOCK, cnt[:, 0:1], nblk[:, 0:1] * MOE_BLOCK, bend[:, 0:1]],
            axis=1).astype(I32)

    rank = rank_ref[...]
    sel = rank >= 0
    pos = bstart[:, 0:1] * MOE_BLOCK + rank.astype(F32)
    slot = jnp.dot(lower, sel.astype(BF16), preferred_element_type=F32)
    w = w_ref[...]
    pos_rows, w_rows = [], []
    for k in range(TOP_K):
        m = sel & (slot == float(k))
        pos_rows.append(jnp.sum(jnp.where(m, pos, 0.0), axis=0, keepdims=True))
        w_rows.append(jnp.sum(jnp.where(m, w, 0.0), axis=0, keepdims=True))
    pos_ref[...] = jnp.concatenate(pos_rows, axis=0).astype(I32)
    wc_ref[...] = jnp.concatenate(w_rows, axis=0)


def _plan(w, rank, cnt):
    e, t = w.shape
    tr = min(1024, t)
    return pl.pallas_call(
        _plan_kernel,
        grid=(t // tr,),
        in_specs=[pl.BlockSpec((e, tr), lambda i: (0, i)),
                  pl.BlockSpec((e, tr), lambda i: (0, i)),
                  pl.BlockSpec((e, LANES), lambda i: (0, 0))],
        out_specs=[pl.BlockSpec((TOP_K, tr), lambda i: (0, i)),
                   pl.BlockSpec((TOP_K, tr), lambda i: (0, i)),
                   pl.BlockSpec((e, 4), lambda i: (0, 0))],
        out_shape=[jax.ShapeDtypeStruct((TOP_K, t), I32),
                   jax.ShapeDtypeStruct((TOP_K, t), F32),
                   jax.ShapeDtypeStruct((e, 4), I32)],
        compiler_params=_cparams(("arbitrary",)),
        name="plan",
    )(w, rank, cnt)


def _row_copy(src, src_row, dst, dst_row, sem):
    return pltpu.make_async_copy(src.at[pl.ds(src_row * ROW_WORDS, ROW_WORDS), :],
                                 dst.at[pl.ds(dst_row * ROW_WORDS, ROW_WORDS), :], sem)


def _dispatch_kernel(pos_ref, seg_ref, hp_ref, xs_ref, zero_ref, sem, zsem):
    tq = pos_ref.shape[0] // TOP_K
    blk_words = MOE_BLOCK * ROW_WORDS
    n_blocks = xs_ref.shape[0] // blk_words

    @pl.when(pl.program_id(0) == 0)
    def _():
        zero_ref[...] = jnp.zeros_like(zero_ref)
        zrow = zero_ref.at[pl.ds(0, ROW_WORDS), :]

        def per_expert(e, n):
            start = seg_ref[e, 0] + seg_ref[e, 1]
            npad = seg_ref[e, 2] - seg_ref[e, 1]

            def fill(r, c):
                pltpu.make_async_copy(
                    zrow, xs_ref.at[pl.ds((start + r) * ROW_WORDS, ROW_WORDS), :], zsem).start()
                return c

            lax.fori_loop(0, npad, fill, 0)
            return n + npad

        n_pad_rows = lax.fori_loop(0, N_EXPERTS, per_expert, 0)

        def drain_row(r, c):
            pltpu.make_async_copy(zrow, xs_ref.at[pl.ds(0, ROW_WORDS), :], zsem).wait()
            return c

        lax.fori_loop(0, n_pad_rows, drain_row, 0)
        n_used = seg_ref[N_EXPERTS - 1, 3]

        def fill_block(b, c):
            pltpu.make_async_copy(
                zero_ref, xs_ref.at[pl.ds(b * blk_words, blk_words), :], zsem).start()
            return c

        lax.fori_loop(n_used, n_blocks, fill_block, 0)

        def drain_block(b, c):
            pltpu.make_async_copy(zero_ref, xs_ref.at[pl.ds(0, blk_words), :], zsem).wait()
            return c

        lax.fori_loop(n_used, n_blocks, drain_block, 0)

    def issue(t, c):
        for k in range(TOP_K):
            _row_copy(hp_ref, t, xs_ref, pos_ref[t * TOP_K + k], sem).start(priority=k % 2)
        return c

    lax.fori_loop(0, tq, issue, 0)
    for k in range(TOP_K):
        pltpu.make_async_copy(hp_ref, xs_ref.at[pl.ds(0, tq * ROW_WORDS), :], sem).wait()


def _dispatch(pos_tm, seg, hp, n_rows):
    t = pos_tm.shape[0] // TOP_K
    tq = min(1024, t)
    return pl.pallas_call(
        _dispatch_kernel,
        grid=(t // tq,),
        in_specs=[pl.BlockSpec((tq * TOP_K,), lambda i: (i,), memory_space=pltpu.SMEM),
                  pl.BlockSpec(seg.shape, lambda i: (0, 0), memory_space=pltpu.SMEM),
                  pl.BlockSpec((tq * ROW_WORDS, LANES), lambda i: (i, 0))],
        out_specs=pl.BlockSpec(memory_space=pl.ANY),
        out_shape=jax.ShapeDtypeStruct((n_rows * ROW_WORDS, LANES), U32),
        scratch_shapes=[pltpu.VMEM((MOE_BLOCK * ROW_WORDS, LANES), U32),
                        pltpu.SemaphoreType.DMA(()), pltpu.SemaphoreType.DMA(())],
        compiler_params=_cparams(("arbitrary",)),
        name="dispatch",
    )(pos_tm, seg, hp)


def _expert_kernel(seg_ref, xs_ref, wg_ref, wu_ref, wd_ref, ys_ref,
                   wg32, wu32, wd32, wgb, wub, wdb, xbuf, ybuf, wsem, xsem, ysem):
    e = pl.program_id(0)
    n_exp = pl.num_programs(0)
    last_step = e == n_exp - 1

    def weight_copies(ex):
        slot = ex & 1
        return [pltpu.make_async_copy(src.at[ex], dst.at[slot], wsem.at[slot])
                for src, dst in ((wg_ref, wg32), (wu_ref, wu32), (wd_ref, wd32))]

    @pl.when(e == 0)
    def _():
        for cp in weight_copies(0):
            cp.start()

    for cp in weight_copies(e):
        cp.wait()
    rows = MOE_BLOCK
    blk_words = MOE_BLOCK * ROW_WORDS
    n_blocks = ys_ref.shape[0] // blk_words
    first = lax.div(seg_ref[e * 4], MOE_BLOCK)
    end = seg_ref[e * 4 + 3]
    n_used = seg_ref[(pl.num_programs(0) - 1) * 4 + 3]

    x_slots = xbuf.shape[0]
    x_ahead = x_slots - 2

    def x_copy(g):
        slot = g & (x_slots - 1)
        return pltpu.make_async_copy(xs_ref.at[pl.ds(g * blk_words, blk_words), :],
                                     xbuf.at[slot], xsem.at[slot])

    def y_copy(g):
        slot = g & 1
        return pltpu.make_async_copy(ybuf.at[slot],
                                     ys_ref.at[pl.ds(g * blk_words, blk_words), :], ysem.at[slot])

    for g0 in range(x_ahead):
        @pl.when(jnp.logical_and(e == 0, n_used > g0))
        def _():
            x_copy(g0).start()

    @pl.when(end > first)
    def _():
        wslot = e & 1
        wgb[...] = wg32[wslot].astype(BF16)
        wub[...] = wu32[wslot].astype(BF16)
        wdb[...] = wd32[wslot].astype(BF16)

    @pl.when(e + 1 < n_exp)
    def _():
        for cp in weight_copies(e + 1):
            cp.start(priority=1)

    @pl.when(end > first)
    def _():
        def block(g, c):
            slot = g & 1
            x_copy(g).wait()

            @pl.when(g + x_ahead < n_used)
            def _():
                x_copy(g + x_ahead).start()

            @pl.when(g >= 2)
            def _():
                y_copy(g - 2).wait()

            xin = xbuf.at[g & (x_slots - 1)]
            his, los = [], []
            for j in range(ROW_WORDS):
                hi, lo = _unpack_pair(xin[pl.ds(j, rows, stride=ROW_WORDS), :])
                his.append(hi.astype(BF16))
                los.append(lo.astype(BF16))
            xb = jnp.concatenate(his + los, axis=1)
            a = jnp.dot(xb, wgb[...], preferred_element_type=F32)
            u = jnp.dot(xb, wub[...], preferred_element_type=F32)
            hid = (_silu(a) * u).astype(BF16)
            out = jnp.dot(hid, wdb[...], preferred_element_type=F32).astype(BF16).astype(F32)
            half = out.shape[1] // 2
            words = _pack_pair(out[:, :half], out[:, half:])
            yout = ybuf.at[slot]
            for j in range(ROW_WORDS):
                yout[pl.ds(j, rows, stride=ROW_WORDS), :] = words[:, j * LANES:(j + 1) * LANES]
            y_copy(g).start()
            return c

        lax.fori_loop(first, end, block, 0)

    @pl.when(last_step)
    def _():
        @pl.when(n_used >= 2)
        def _():
            y_copy(n_used - 2).wait()

        @pl.when(n_used >= 1)
        def _():
            y_copy(n_used - 1).wait()

        ybuf[0] = jnp.zeros(ybuf.shape[1:], ybuf.dtype)

        def fill(g, c):
            pltpu.make_async_copy(ybuf.at[0], ys_ref.at[pl.ds(g * blk_words, blk_words), :],
                                  ysem.at[0]).start()
            return c

        lax.fori_loop(n_used, n_blocks, fill, 0)

        def drain(g, c):
            pltpu.make_async_copy(ybuf.at[0], ys_ref.at[pl.ds(0, blk_words), :], ysem.at[0]).wait()
            return c

        lax.fori_loop(n_used, n_blocks, drain, 0)


def _experts(seg_flat, xs, w_gate, w_up, w_down):
    n_exp, d, f = w_gate.shape
    blk_words = MOE_BLOCK * ROW_WORDS
    grid_spec = pltpu.PrefetchScalarGridSpec(
        num_scalar_prefetch=1,
        grid=(n_exp,),
        in_specs=[
            pl.BlockSpec(memory_space=pl.ANY),
            pl.BlockSpec(memory_space=pl.ANY),
            pl.BlockSpec(memory_space=pl.ANY),
            pl.BlockSpec(memory_space=pl.ANY),
        ],
        out_specs=pl.BlockSpec(memory_space=pl.ANY),
        scratch_shapes=[pltpu.VMEM((2, d, f), F32), pltpu.VMEM((2, d, f), F32), pltpu.VMEM((2, f, d), F32),
                        pltpu.VMEM((d, f), BF16), pltpu.VMEM((d, f), BF16), pltpu.VMEM((f, d), BF16),
                        pltpu.VMEM((4, blk_words, LANES), U32), pltpu.VMEM((2, blk_words, LANES), U32),
                        pltpu.SemaphoreType.DMA((2,)), pltpu.SemaphoreType.DMA((4,)),
                        pltpu.SemaphoreType.DMA((2,))],
    )
    return pl.pallas_call(
        _expert_kernel,
        grid_spec=grid_spec,
        out_shape=jax.ShapeDtypeStruct(xs.shape, U32),
        compiler_params=_cparams(("arbitrary",)),
        name="experts",
    )(seg_flat, xs, w_gate, w_up, w_down)


def _combine_kernel(pos_ref, nxt_ref, ys_ref, wc_ref, h2_ref, x1_ref, sg_ref, su_ref, sd_ref,
                    g2_ref, nf_ref, scf_ref, shf_ref, o_ref, buf, sem):
    i = pl.program_id(0)
    tq = wc_ref.shape[1]
    slot = i & 1

    def gather(p_ref, to_slot):
        def issue(t, c):
            for k in range(TOP_K):
                pltpu.make_async_copy(
                    ys_ref.at[pl.ds(p_ref[t * TOP_K + k] * ROW_WORDS, ROW_WORDS), :],
                    buf.at[to_slot, k, pl.ds(t * ROW_WORDS, ROW_WORDS), :],
                    sem.at[to_slot]).start(priority=k % 2)
            return c

        lax.fori_loop(0, tq, issue, 0)

    @pl.when(i == 0)
    def _():
        gather(pos_ref, 0)

    @pl.when(i + 1 < pl.num_programs(0))
    def _():
        gather(nxt_ref, 1 - slot)

    h2 = h2_ref[...]
    a = jnp.dot(h2, sg_ref[...], preferred_element_type=F32)
    u = jnp.dot(h2, su_ref[...], preferred_element_type=F32)
    shared = jnp.dot((_silu(a) * u).astype(BF16), sd_ref[...], preferred_element_type=F32)

    for k in range(TOP_K):
        pltpu.make_async_copy(ys_ref.at[pl.ds(0, tq * ROW_WORDS), :], buf.at[slot, k],
                              sem.at[slot]).wait()

    wt = jnp.transpose(wc_ref[...])
    his, los = [], []
    for j in range(ROW_WORDS):
        acc_hi = jnp.zeros((tq, LANES), F32)
        acc_lo = jnp.zeros((tq, LANES), F32)
        for k in range(TOP_K):
            hi, lo = _unpack_pair(buf.at[slot, k][pl.ds(j, tq, stride=ROW_WORDS), :])
            wk = wt[:, k:k + 1]
            acc_hi = acc_hi + wk * hi
            acc_lo = acc_lo + wk * lo
        his.append(acc_hi)
        los.append(acc_lo)
    y = jnp.concatenate(his + los, axis=1) + shared
    x2 = x1_ref[...] + g2_ref[0] * y
    o_ref[...] = _norm_mod(x2, nf_ref[...], scf_ref[0], shf_ref[0])


def _combine(pos_tm, ys, w_c, h2, x1, sg, su, sd, g2, nf, scf, shf, seq):
    t, d = x1.shape
    tq = 128
    n_tiles = t // tq
    per_batch = seq // tq
    mod = pl.BlockSpec((1, 1, d), lambda i: (i // per_batch, 0, 0))
    full = lambda a: pl.BlockSpec(a.shape, lambda i: (0, 0))
    return pl.pallas_call(
        _combine_kernel,
        grid=(n_tiles,),
        in_specs=[
            pl.BlockSpec((tq * TOP_K,), lambda i: (i,), memory_space=pltpu.SMEM),
            pl.BlockSpec((tq * TOP_K,), lambda i: (jnp.minimum(i + 1, n_tiles - 1),),
                         memory_space=pltpu.SMEM),
            pl.BlockSpec(memory_space=pl.ANY),
            pl.BlockSpec((TOP_K, tq), lambda i: (0, i)),
            pl.BlockSpec((tq, d), lambda i: (i, 0)),
            pl.BlockSpec((tq, d), lambda i: (i, 0)),
            full(sg), full(su), full(sd),
            mod,
            pl.BlockSpec((1, d), lambda i: (0, 0)),
            mod, mod,
        ],
        out_specs=pl.BlockSpec((tq, d), lambda i: (i, 0)),
        out_shape=jax.ShapeDtypeStruct((t, d), F32),
        scratch_shapes=[pltpu.VMEM((2, TOP_K, tq * ROW_WORDS, LANES), U32),
                        pltpu.SemaphoreType.DMA((2,))],
        compiler_params=_cparams(("arbitrary",)),
        name="combine",
    )(pos_tm, pos_tm, ys, w_c, h2, x1, sg, su, sd, g2, nf.reshape(1, d), scf, shf)


def _rope_tables(seq):
    pos = jnp.arange(seq, dtype=F32)
    inv_freq = ROPE_BASE ** (-jnp.arange(0, HEAD_DIM, 2, dtype=F32) / HEAD_DIM)
    ang = pos[:, None] * inv_freq[None, :]
    cos, sin = jnp.cos(ang), jnp.sin(ang)
    return jnp.concatenate([cos, cos], axis=1), jnp.concatenate([-sin, sin], axis=1)


def kernel(x, c, w_ada, b_ada, norm1_g, w_in, ret_gn_g, w_out, norm2_g, w_router, router_bias,
           w_gate, w_up, w_down, ws_gate, ws_up, ws_down, w_ada_final, b_ada_final, norm_f_g):
    bsz, seq, d = x.shape
    t = bsz * seq
    depth = w_ada.shape[0]
    n_heads = d // (2 * HEAD_DIM)
    sb_w = n_heads * HEAD_DIM
    cos2, sin2 = _rope_tables(seq)
    c_pad = jnp.pad(c, ((0, 8 - bsz % 8), (0, 0)))
    n_rows = t * TOP_K + N_EXPERTS * MOE_BLOCK
    as_mod = lambda v: v.reshape(bsz, 1, d)

    x2 = x.reshape(t, d)
    for l in range(depth):
        mod = _ada(c_pad, w_ada[l], b_ada[l])[:bsz]
        sh1, sc1, g1, sh2, sc2, g2 = [as_mod(m) for m in jnp.split(mod, 6, axis=-1)]
        u = _inproj(x2, norm1_g[l], sc1, sh1, w_in[l].astype(BF16), seq)
        u3 = u.reshape(bsz, seq, -1)
        sb = _sb_attention(u3, n_heads)
        rt = _retention(u3, cos2, sin2, ret_gn_g[l], n_heads, 3 * n_heads)
        wo = w_out[l].astype(BF16)
        x1, h2, hp, lt = _outproj(sb.reshape(t, sb_w), rt.reshape(t, -1), wo[:sb_w], wo[sb_w:], x2,
                                  g1, norm2_g[l], sc2, sh2, jnp.transpose(w_router[l]).astype(BF16), seq)
        w_e, rank_e, cnt = _route(lt, router_bias[l])
        pos_c, w_c, seg = _plan(w_e, rank_e, cnt)
        pos_tm = jnp.transpose(pos_c).reshape(-1)
        xs = _dispatch(pos_tm, seg, hp, n_rows)
        ys = _experts(seg.reshape(-1), xs, w_gate[l], w_up[l], w_down[l])
        if l + 1 < depth:
            raise NotImplementedError("only the final layer fuses the output norm")
        modf = _ada(c_pad, w_ada_final, b_ada_final)[:bsz]
        shf, scf = [as_mod(m) for m in jnp.split(modf, 2, axis=-1)]
        x2 = _combine(pos_tm, ys, w_c, h2, x1, ws_gate[l].astype(BF16), ws_up[l].astype(BF16),
                      ws_down[l].astype(BF16), g2, norm_f_g, scf, shf, seq)
    return x2.reshape(bsz, seq, d)
```

```python
import functools

import jax
import jax.numpy as jnp
from jax import lax
from jax.experimental import pallas as pl
from jax.experimental.pallas import tpu as pltpu

F32 = jnp.float32
BF16 = jnp.bfloat16
I32 = jnp.int32
U32 = jnp.uint32

HEAD_DIM = 128
N_EXPERTS = 64
TOP_K = 8
N_GROUPS = 8
TOPK_GROUP = 4
ROUTED_SCALE = 2.5
EPS = 1e-6
ROPE_BASE = 10000.0

LANES = 128
ROW_WORDS = 8
MOE_BLOCK = 256
V7X_VMEM_LIMIT = 56 * 1024 * 1024
LOG2_E = 1.4426950408889634
SB_SKIP_BOUND = -115.0 * LOG2_E


def _cparams(sem, vmem=V7X_VMEM_LIMIT):
    return pltpu.CompilerParams(dimension_semantics=sem, vmem_limit_bytes=vmem)


def _silu(v):
    return v * jax.nn.sigmoid(v)


def _pack_pair(hi_f32, lo_f32):
    hw = pltpu.bitcast(hi_f32, U32) & jnp.uint32(0xFFFF0000)
    lw = pltpu.bitcast(lo_f32, U32) >> 16
    return hw | lw


def _unpack_pair(w):
    hi = pltpu.bitcast(w & jnp.uint32(0xFFFF0000), F32)
    lo = pltpu.bitcast(w << 16, F32)
    return hi, lo


def _ada_kernel(c_ref, w_ref, b_ref, o_ref):
    cs = _silu(c_ref[...]).astype(BF16)
    o_ref[...] = jnp.dot(cs, w_ref[...].astype(BF16), preferred_element_type=F32) + b_ref[...]


def _ada(c_pad, w, b):
    rows, d = c_pad.shape
    n = w.shape[1]
    tn = 1024
    return pl.pallas_call(
        _ada_kernel,
        grid=(n // tn,),
        in_specs=[
            pl.BlockSpec((rows, d), lambda j: (0, 0)),
            pl.BlockSpec((d, tn), lambda j: (0, j)),
            pl.BlockSpec((1, tn), lambda j: (0, j)),
        ],
        out_specs=pl.BlockSpec((rows, tn), lambda j: (0, j)),
        out_shape=jax.ShapeDtypeStruct((rows, n), F32),
        compiler_params=_cparams(("arbitrary",)),
        name="ada",
    )(c_pad, w, b.reshape(1, n))


def _norm_mod(x, g, sc, sh):
    ms = jnp.mean(x * x, axis=-1, keepdims=True)
    return (x * lax.rsqrt(ms + EPS) * g) * (1.0 + sc) + sh


def _inproj_kernel(x_ref, g_ref, sc_ref, sh_ref, w_ref, o_ref, h_ref, *, chunk):
    first = pl.program_id(1) == 0

    @pl.when(first)
    def _():
        g = g_ref[...]
        sc = sc_ref[0]
        sh = sh_ref[0]
        for r in range(x_ref.shape[0] // chunk):
            rows = pl.ds(r * chunk, chunk)
            h = _norm_mod(x_ref[rows, :], g, sc, sh).astype(BF16)
            h_ref[rows, :] = h
            o_ref[rows, :] = jnp.dot(h, w_ref[...], preferred_element_type=F32).astype(BF16)

    @pl.when(jnp.logical_not(first))
    def _():
        o_ref[...] = jnp.dot(h_ref[...], w_ref[...], preferred_element_type=F32).astype(BF16)


def _inproj(x2, g, sc, sh, w_bf, seq):
    t, d = x2.shape
    n = w_bf.shape[1]
    tm = 1024
    tn = n // 4 if (n // 4) % (2 * LANES) == 0 else 1024
    per_batch = seq // tm
    return pl.pallas_call(
        functools.partial(_inproj_kernel, chunk=256),
        grid=(t // tm, n // tn),
        in_specs=[
            pl.BlockSpec((tm, d), lambda i, j: (i, 0)),
            pl.BlockSpec((1, d), lambda i, j: (0, 0)),
            pl.BlockSpec((1, 1, d), lambda i, j: (i // per_batch, 0, 0)),
            pl.BlockSpec((1, 1, d), lambda i, j: (i // per_batch, 0, 0)),
            pl.BlockSpec((d, tn), lambda i, j: (0, j)),
        ],
        out_specs=pl.BlockSpec((tm, tn), lambda i, j: (i, j)),
        out_shape=jax.ShapeDtypeStruct((t, n), BF16),
        scratch_shapes=[pltpu.VMEM((tm, d), BF16)],
        compiler_params=_cparams(("arbitrary", "arbitrary")),
        name="inproj",
    )(x2, g.reshape(1, d), sc, sh, w_bf)


def _sb_kernel(q_ref, k_ref, v_ref, o_ref, *, group):
    seq = q_ref.shape[0]
    blk = LANES
    scale = HEAD_DIM ** -0.5 * LOG2_E
    row = lax.broadcasted_iota(I32, (blk, blk), 0)
    col = lax.broadcasted_iota(I32, (blk, blk), 1)
    strict = col < row
    tr = lax.broadcasted_iota(I32, (2 * blk, 2 * blk), 0) % blk
    tc = lax.broadcasted_iota(I32, (2 * blk, 2 * blk), 1)
    tri = jnp.where((tc >= blk) | (tr > tc), 1.0, 0.0).astype(BF16)

    def key_rows(j):
        return pl.ds(pl.multiple_of(j * blk, blk), blk)

    def logits(qb, kb):
        z = lax.dot_general(qb, kb, (((1,), (1,)), ((), ())), preferred_element_type=F32) * scale
        sp = jnp.maximum(z, 0.0) + jnp.log2(1.0 + jnp.exp2(-jnp.abs(z)))
        return -sp, z - sp

    def suffix_sums(log_surv):
        hi = log_surv.astype(BF16)
        lo = (log_surv - hi.astype(F32)).astype(BF16)
        r = jnp.dot(jnp.concatenate([hi, lo], axis=1), tri, preferred_element_type=F32)
        return r[:, :blk], r[:, blk:]

    def tile(qb, j, g, carry, acc):
        log_surv, log_beta = logits(qb, k_ref[key_rows(j), g * blk:(g + 1) * blk])
        after, total = suffix_sums(log_surv)
        a = jnp.exp2(log_beta + after + carry)
        acc = acc + jnp.dot(a.astype(BF16), v_ref[key_rows(j), g * blk:(g + 1) * blk],
                            preferred_element_type=F32)
        return carry + total, acc

    def group_start(qbs, i):
        n_near = 3
        js = [i, jnp.maximum(i - 1, 0), jnp.maximum(i - 2, 0)]
        ivec = jnp.full((blk, blk), i, I32)
        masks = [strict, ivec >= 1, ivec >= 2]
        log_betas, parts = [], []
        for g in range(group):
            cols = slice(g * blk, (g + 1) * blk)
            kcat = jnp.concatenate([k_ref[key_rows(j), cols] for j in js], axis=0)
            log_surv, log_beta = logits(qbs[g], kcat)
            log_betas.append(log_beta)
            for m in range(n_near):
                parts.append(jnp.where(masks[m], log_surv[:, m * blk:(m + 1) * blk], 0.0))
        after, total = suffix_sums(jnp.concatenate(parts, axis=0))
        out = []
        for g in range(group):
            cols = slice(g * blk, (g + 1) * blk)
            carry = jnp.zeros((blk, blk), F32)
            weights = []
            for m in range(n_near):
                rows = slice((g * n_near + m) * blk, (g * n_near + m + 1) * blk)
                a = jnp.exp2(log_betas[g][:, m * blk:(m + 1) * blk] + after[rows] + carry)
                weights.append(jnp.where(masks[m], a, 0.0).astype(BF16))
                carry = carry + total[rows]
            vcat = jnp.concatenate([v_ref[key_rows(j), cols] for j in js], axis=0)
            acc = jnp.dot(jnp.concatenate(weights, axis=1), vcat, preferred_element_type=F32)
            out.append((carry, acc))
        return out

    q_blocks = q_ref.shape[0] // blk

    def q_block(local, c):
        i = pl.program_id(1) * q_blocks + local
        rows = pl.ds(pl.multiple_of(local * blk, blk), blk)
        qbs = [q_ref[rows, g * blk:(g + 1) * blk] for g in range(group)]
        first = group_start(qbs, i)
        carries = tuple(f[0] for f in first)
        accs = tuple(f[1] for f in first)

        def cond(st):
            j, crs, _ = st
            top = functools.reduce(jnp.maximum, crs)
            return jnp.logical_and(j >= 0, jnp.max(top) > SB_SKIP_BOUND)

        def body(st):
            j, crs, acs = st
            nxt = [tile(qbs[g], j, g, crs[g], acs[g]) for g in range(group)]
            return j - 1, tuple(n[0] for n in nxt), tuple(n[1] for n in nxt)

        _, _, accs = lax.while_loop(cond, body, (i - 3, carries, accs))
        for g in range(group):
            o_ref[rows, g * blk:(g + 1) * blk] = accs[g].astype(BF16)
        return c

    lax.fori_loop(0, q_blocks, q_block, 0)


SB_ROW_TILE = 512


def _sb_attention(u3, n_heads):
    b, s, _ = u3.shape
    w = n_heads * HEAD_DIM
    tq = min(SB_ROW_TILE, s)
    return pl.pallas_call(
        functools.partial(_sb_kernel, group=n_heads),
        grid=(b, s // tq),
        in_specs=[pl.BlockSpec((None, tq, w), lambda bi, qi: (bi, qi, 0)),
                  pl.BlockSpec((None, s, w), lambda bi, qi: (bi, 0, 1)),
                  pl.BlockSpec((None, s, w), lambda bi, qi: (bi, 0, 2))],
        out_specs=pl.BlockSpec((None, tq, w), lambda bi, qi: (bi, qi, 0)),
        out_shape=jax.ShapeDtypeStruct((b, s, w), BF16),
        compiler_params=_cparams(("arbitrary", "arbitrary")),
        name="sb_attn",
    )(u3, u3, u3)


def _ret_kernel(q_ref, k_ref, v_ref, g_ref, cos_ref, sin_ref, gn_ref, o_ref, *, chunk, unroll):
    seq = q_ref.shape[0]
    dh = HEAD_DIM
    head = pl.program_id(1)

    def log_gamma(shape):
        hv = jnp.full(shape, head, I32).astype(F32)
        return jnp.log(1.0 - jnp.exp2(-5.0 - hv))

    diff = (lax.broadcasted_iota(I32, (chunk, chunk), 0)
            - lax.broadcasted_iota(I32, (chunk, chunk), 1)).astype(F32)
    lower = diff >= 0.0
    dmat = jnp.where(lower, jnp.exp(jnp.where(lower, diff, 0.0) * log_gamma((chunk, chunk))), 0.0)
    pos = lax.broadcasted_iota(I32, (chunk, dh), 0).astype(F32)
    k_decay = jnp.exp((chunk - 1.0 - pos) * log_gamma((chunk, dh)))
    q_decay = jnp.exp((pos + 1.0) * log_gamma((chunk, dh)))
    chunk_decay = jnp.exp(chunk * log_gamma((dh, dh)))
    gn = gn_ref[...]
    scale = dh ** -0.5

    def rope(t, cs, sn):
        return t * cs + pltpu.roll(t, dh // 2, 1) * sn

    def body(n, state):
        rows = pl.ds(pl.multiple_of(n * chunk, chunk), chunk)
        cs = cos_ref[rows, :]
        sn = sin_ref[rows, :]
        qc = rope(q_ref[rows, :].astype(F32), cs, sn)
        kc = rope(k_ref[rows, :].astype(F32), cs, sn) * scale
        vc = v_ref[rows, :]
        scores = lax.dot_general(qc.astype(BF16), kc.astype(BF16), (((1,), (1,)), ((), ())),
                                 preferred_element_type=F32) * dmat
        intra = jnp.dot(scores.astype(BF16), vc, preferred_element_type=F32)
        cross = jnp.dot((qc * q_decay).astype(BF16), state.astype(BF16), preferred_element_type=F32)
        out = intra + cross
        kd_t = jnp.transpose(kc * k_decay).astype(BF16)
        state = state * chunk_decay + jnp.dot(kd_t, vc, preferred_element_type=F32)
        ms = jnp.mean(out * out, axis=-1, keepdims=True)
        y = out * lax.rsqrt(ms + EPS) * gn * _silu(g_ref[rows, :].astype(F32))
        o_ref[rows, :] = y.astype(BF16)
        return state

    lax.fori_loop(0, seq // chunk, body, jnp.zeros((dh, dh), F32), unroll=unroll)


RET_CHUNK = 512
RET_UNROLL = 2


def _retention(u3, cos2, sin2, gn, n_heads, col0):
    b, s, _ = u3.shape
    spec = lambda off: pl.BlockSpec((None, s, HEAD_DIM), lambda bi, h: (bi, 0, col0 + off + h))
    return pl.pallas_call(
        functools.partial(_ret_kernel, chunk=RET_CHUNK, unroll=RET_UNROLL),
        grid=(b, n_heads),
        in_specs=[spec(0), spec(n_heads), spec(2 * n_heads), spec(3 * n_heads),
                  pl.BlockSpec((s, HEAD_DIM), lambda bi, h: (0, 0)),
                  pl.BlockSpec((s, HEAD_DIM), lambda bi, h: (0, 0)),
                  pl.BlockSpec((1, HEAD_DIM), lambda bi, h: (0, h))],
        out_specs=pl.BlockSpec((None, s, HEAD_DIM), lambda bi, h: (bi, 0, h)),
        out_shape=jax.ShapeDtypeStruct((b, s, n_heads * HEAD_DIM), BF16),
        compiler_params=_cparams(("arbitrary", "arbitrary")),
        name="retention",
    )(u3, u3, u3, u3, cos2, sin2, gn.reshape(1, -1))


def _outproj_kernel(sb_ref, rt_ref, wa_ref, wb_ref, x_ref, g1_ref, n2_ref, sc_ref, sh_ref, wr_ref,
                    x1_ref, h2_ref, hp_ref, lt_ref, *, chunk):
    d = x_ref.shape[1]
    half = d // 2
    g1 = g1_ref[0]
    n2 = n2_ref[...]
    sc = sc_ref[0]
    sh = sh_ref[0]
    for r in range(x_ref.shape[0] // chunk):
        rows = pl.ds(r * chunk, chunk)
        o = jnp.dot(sb_ref[rows, :], wa_ref[...], preferred_element_type=F32)
        o = o + jnp.dot(rt_ref[rows, :], wb_ref[...], preferred_element_type=F32)
        x1 = x_ref[rows, :] + g1 * o
        x1_ref[rows, :] = x1
        h2 = _norm_mod(x1, n2, sc, sh).astype(BF16)
        h2_ref[rows, :] = h2
        lt_ref[:, rows] = lax.dot_general(wr_ref[...], h2, (((1,), (1,)), ((), ())),
                                          preferred_element_type=F32)
        words = _pack_pair(h2[:, :half].astype(F32), h2[:, half:].astype(F32))
        for j in range(ROW_WORDS):
            hp_ref[pl.ds(r * chunk * ROW_WORDS + j, chunk, stride=ROW_WORDS), :] = (
                words[:, j * LANES:(j + 1) * LANES])


def _outproj(sb2, rt2, wa, wb, x2, g1, n2, sc2, sh2, wr_t, seq):
    t, d = x2.shape
    tm = 512
    per_batch = seq // tm
    mod = pl.BlockSpec((1, 1, d), lambda i: (i // per_batch, 0, 0))
    return pl.pallas_call(
        functools.partial(_outproj_kernel, chunk=512),
        grid=(t // tm,),
        in_specs=[
            pl.BlockSpec((tm, sb2.shape[1]), lambda i: (i, 0)),
            pl.BlockSpec((tm, rt2.shape[1]), lambda i: (i, 0)),
            pl.BlockSpec(wa.shape, lambda i: (0, 0)),
            pl.BlockSpec(wb.shape, lambda i: (0, 0)),
            pl.BlockSpec((tm, d), lambda i: (i, 0)),
            mod,
            pl.BlockSpec((1, d), lambda i: (0, 0)),
            mod, mod,
            pl.BlockSpec(wr_t.shape, lambda i: (0, 0)),
        ],
        out_specs=[
            pl.BlockSpec((tm, d), lambda i: (i, 0)),
            pl.BlockSpec((tm, d), lambda i: (i, 0)),
            pl.BlockSpec((tm * ROW_WORDS, LANES), lambda i: (i, 0)),
            pl.BlockSpec((N_EXPERTS, tm), lambda i: (0, i)),
        ],
        out_shape=[
            jax.ShapeDtypeStruct((t, d), F32),
            jax.ShapeDtypeStruct((t, d), BF16),
            jax.ShapeDtypeStruct((t * ROW_WORDS, LANES), U32),
            jax.ShapeDtypeStruct((N_EXPERTS, t), F32),
        ],
        compiler_params=_cparams(("arbitrary",)),
        name="outproj",
    )(sb2, rt2, wa, wb, x2, g1, n2.reshape(1, d), sc2, sh2, wr_t)


def _beats(row, allv, row_idx, idx):
    return (row > allv) | ((row == allv) & (row_idx < idx))


def _route_kernel(lt_ref, bias_ref, w_ref, rank_ref, cnt_ref, carry_ref):
    tr = lt_ref.shape[1]
    gsz = N_EXPERTS // N_GROUPS
    neg_inf = jnp.float32(-jnp.inf)

    @pl.when(pl.program_id(0) == 0)
    def _():
        carry_ref[...] = jnp.zeros_like(carry_ref)

    scores = jax.nn.sigmoid(lt_ref[...])
    choice = scores + bias_ref[...]
    sub = lax.broadcasted_iota(I32, (gsz, tr), 0)
    gs_rows = []
    for g in range(N_GROUPS):
        cg = choice[g * gsz:(g + 1) * gsz, :]
        m1 = jnp.max(cg, axis=0, keepdims=True)
        first = jnp.min(jnp.where(cg == m1, sub, gsz), axis=0, keepdims=True)
        m2 = jnp.max(jnp.where(sub == first, neg_inf, cg), axis=0, keepdims=True)
        gs_rows.append(m1 + m2)
    gs = jnp.concatenate(gs_rows, axis=0)
    gi = lax.broadcasted_iota(I32, (N_GROUPS, tr), 0)
    grank = jnp.zeros((N_GROUPS, tr), I32)
    for g in range(N_GROUPS):
        grank = grank + _beats(gs[g:g + 1, :], gs, g, gi).astype(I32)
    gmask = grank < TOPK_GROUP
    emask = jnp.concatenate(
        [jnp.broadcast_to(gmask[g:g + 1, :], (gsz, tr)) for g in range(N_GROUPS)], axis=0)
    masked = jnp.where(emask, choice, neg_inf)
    ei = lax.broadcasted_iota(I32, (N_EXPERTS, tr), 0)
    erank = jnp.zeros((N_EXPERTS, tr), I32)
    for e in range(N_EXPERTS):
        erank = erank + _beats(masked[e:e + 1, :], masked, e, ei).astype(I32)
    sel = erank < TOP_K
    ssel = jnp.where(sel, scores, 0.0)
    denom = jnp.sum(ssel, axis=0, keepdims=True)
    w_ref[...] = ssel / denom * ROUTED_SCALE

    self = sel.astype(F32)
    upper = (lax.broadcasted_iota(I32, (tr, tr), 0) < lax.broadcasted_iota(I32, (tr, tr), 1))
    prefix = jnp.dot(self.astype(BF16), upper.astype(BF16), preferred_element_type=F32)
    carry = carry_ref[...]
    rank = prefix + carry[:, 0:1]
    rank_ref[...] = jnp.where(sel, rank, -1.0).astype(I32)
    carry = carry + jnp.sum(self, axis=1, keepdims=True)
    carry_ref[...] = carry
    cnt_ref[...] = carry


def _route(lt, bias):
    e, t = lt.shape
    tr = min(512, t)
    return pl.pallas_call(
        _route_kernel,
        grid=(t // tr,),
        in_specs=[pl.BlockSpec((e, tr), lambda i: (0, i)),
                  pl.BlockSpec((e, 1), lambda i: (0, 0))],
        out_specs=[pl.BlockSpec((e, tr), lambda i: (0, i)),
                   pl.BlockSpec((e, tr), lambda i: (0, i)),
                   pl.BlockSpec((e, LANES), lambda i: (0, 0))],
        out_shape=[jax.ShapeDtypeStruct((e, t), F32),
                   jax.ShapeDtypeStruct((e, t), I32),
                   jax.ShapeDtypeStruct((e, LANES), F32)],
        scratch_shapes=[pltpu.VMEM((e, LANES), F32)],
        compiler_params=_cparams(("arbitrary",)),
        name="route",
    )(lt, bias.reshape(e, 1))


def _plan_kernel(w_ref, rank_ref, cnt_ref, pos_ref, wc_ref, seg_ref):
    cnt = cnt_ref[...]
    nblk = jnp.floor((cnt + (MOE_BLOCK - 1.0)) * (1.0 / MOE_BLOCK))
    lower = (lax.broadcasted_iota(I32, (N_EXPERTS, N_EXPERTS), 1)
             < lax.broadcasted_iota(I32, (N_EXPERTS, N_EXPERTS), 0)).astype(BF16)
    bstart = jnp.dot(lower, nblk.astype(BF16), preferred_element_type=F32)
    bend = bstart + nblk

    @pl.when(pl.program_id(0) == 0)
    def _():
        seg_ref[...] = jnp.concatenate(
            [bstart[:, 0:1] * MOE_BLOCK, cnt[:, 0:1], nblk[:, 0:1] * MOE_BLOCK, bend[:, 0:1]],
            axis=1).astype(I32)

    rank = rank_ref[...]
    sel = rank >= 0
    pos = bstart[:, 0:1] * MOE_BLOCK + rank.astype(F32)
    slot = jnp.dot(lower, sel.astype(BF16), preferred_element_type=F32)
    w = w_ref[...]
    pos_rows, w_rows = [], []
    for k in range(TOP_K):
        m = sel & (slot == float(k))
        pos_rows.append(jnp.sum(jnp.where(m, pos, 0.0), axis=0, keepdims=True))
        w_rows.append(jnp.sum(jnp.where(m, w, 0.0), axis=0, keepdims=True))
    pos_ref[...] = jnp.concatenate(pos_rows, axis=0).astype(I32)
    wc_ref[...] = jnp.concatenate(w_rows, axis=0)


def _plan(w, rank, cnt):
    e, t = w.shape
    tr = min(1024, t)
    return pl.pallas_call(
        _plan_kernel,
        grid=(t // tr,),
        in_specs=[pl.BlockSpec((e, tr), lambda i: (0, i)),
                  pl.BlockSpec((e, tr), lambda i: (0, i)),
                  pl.BlockSpec((e, LANES), lambda i: (0, 0))],
        out_specs=[pl.BlockSpec((TOP_K, tr), lambda i: (0, i)),
                   pl.BlockSpec((TOP_K, tr), lambda i: (0, i)),
                   pl.BlockSpec((e, 4), lambda i: (0, 0))],
        out_shape=[jax.ShapeDtypeStruct((TOP_K, t), I32),
                   jax.ShapeDtypeStruct((TOP_K, t), F32),
                   jax.ShapeDtypeStruct((e, 4), I32)],
        compiler_params=_cparams(("arbitrary",)),
        name="plan",
    )(w, rank, cnt)


def _row_copy(src, src_row, dst, dst_row, sem):
    return pltpu.make_async_copy(src.at[pl.ds(src_row * ROW_WORDS, ROW_WORDS), :],
                                 dst.at[pl.ds(dst_row * ROW_WORDS, ROW_WORDS), :], sem)


def _dispatch_kernel(pos_ref, seg_ref, hp_ref, h2_ref, sg_ref, su_ref, sd_ref, xs_ref, shared_ref,
                     zero_ref, sem, zsem):
    tq = pos_ref.shape[0] // TOP_K
    blk_words = MOE_BLOCK * ROW_WORDS
    n_blocks = xs_ref.shape[0] // blk_words

    @pl.when(pl.program_id(0) == 0)
    def _():
        zero_ref[...] = jnp.zeros_like(zero_ref)
        zrow = zero_ref.at[pl.ds(0, ROW_WORDS), :]

        def per_expert(e, n):
            start = seg_ref[e, 0] + seg_ref[e, 1]
            npad = seg_ref[e, 2] - seg_ref[e, 1]

            def fill(r, c):
                pltpu.make_async_copy(
                    zrow, xs_ref.at[pl.ds((start + r) * ROW_WORDS, ROW_WORDS), :], zsem).start()
                return c

            lax.fori_loop(0, npad, fill, 0)
            return n + npad

        n_pad_rows = lax.fori_loop(0, N_EXPERTS, per_expert, 0)

        def drain_row(r, c):
            pltpu.make_async_copy(zrow, xs_ref.at[pl.ds(0, ROW_WORDS), :], zsem).wait()
            return c

        lax.fori_loop(0, n_pad_rows, drain_row, 0)
        n_used = seg_ref[N_EXPERTS - 1, 3]

        def fill_block(b, c):
            pltpu.make_async_copy(
                zero_ref, xs_ref.at[pl.ds(b * blk_words, blk_words), :], zsem).start()
            return c

        lax.fori_loop(n_used, n_blocks, fill_block, 0)

        def drain_block(b, c):
            pltpu.make_async_copy(zero_ref, xs_ref.at[pl.ds(0, blk_words), :], zsem).wait()
            return c

        lax.fori_loop(n_used, n_blocks, drain_block, 0)

    def issue(t, c):
        for k in range(TOP_K):
            _row_copy(hp_ref, t, xs_ref, pos_ref[t * TOP_K + k], sem).start(priority=k % 2)
        return c

    lax.fori_loop(0, tq, issue, 0)

    chunk = min(256, tq)
    for r in range(tq // chunk):
        rows = pl.ds(r * chunk, chunk)
        h2 = h2_ref[rows, :]
        a = jnp.dot(h2, sg_ref[...], preferred_element_type=F32)
        u = jnp.dot(h2, su_ref[...], preferred_element_type=F32)
        shared_ref[rows, :] = jnp.dot((_silu(a) * u).astype(BF16), sd_ref[...],
                                      preferred_element_type=F32).astype(BF16)

    for k in range(TOP_K):
        pltpu.make_async_copy(hp_ref, xs_ref.at[pl.ds(0, tq * ROW_WORDS), :], sem).wait()


def _dispatch(pos_tm, seg, hp, h2, sg, su, sd, n_rows):
    t, d = h2.shape
    tq = min(1024, t)
    full = lambda a: pl.BlockSpec(a.shape, lambda i: (0, 0))
    return pl.pallas_call(
        _dispatch_kernel,
        grid=(t // tq,),
        in_specs=[pl.BlockSpec((tq * TOP_K,), lambda i: (i,), memory_space=pltpu.SMEM),
                  pl.BlockSpec(seg.shape, lambda i: (0, 0), memory_space=pltpu.SMEM),
                  pl.BlockSpec((tq * ROW_WORDS, LANES), lambda i: (i, 0)),
                  pl.BlockSpec((tq, d), lambda i: (i, 0)),
                  full(sg), full(su), full(sd)],
        out_specs=[pl.BlockSpec(memory_space=pl.ANY),
                   pl.BlockSpec((tq, d), lambda i: (i, 0))],
        out_shape=[jax.ShapeDtypeStruct((n_rows * ROW_WORDS, LANES), U32),
                   jax.ShapeDtypeStruct((t, d), BF16)],
        scratch_shapes=[pltpu.VMEM((MOE_BLOCK * ROW_WORDS, LANES), U32),
                        pltpu.SemaphoreType.DMA(()), pltpu.SemaphoreType.DMA(())],
        compiler_params=_cparams(("arbitrary",)),
        name="dispatch",
    )(pos_tm, seg, hp, h2, sg, su, sd)


def _expert_kernel(seg_ref, xs_ref, wg_ref, wu_ref, wd_ref, ys_ref,
                   wg32, wu32, wd32, wgb, wub, wdb, xbuf, ybuf, wsem, xsem, ysem):
    e = pl.program_id(0)
    n_exp = pl.num_programs(0)
    last_step = e == n_exp - 1

    def weight_copies(ex):
        slot = ex & 1
        return [pltpu.make_async_copy(src.at[ex], dst.at[slot], wsem.at[slot])
                for src, dst in ((wg_ref, wg32), (wu_ref, wu32), (wd_ref, wd32))]

    @pl.when(e == 0)
    def _():
        for cp in weight_copies(0):
            cp.start()

    for cp in weight_copies(e):
        cp.wait()
    rows = MOE_BLOCK
    blk_words = MOE_BLOCK * ROW_WORDS
    n_blocks = ys_ref.shape[0] // blk_words
    first = lax.div(seg_ref[e * 4], MOE_BLOCK)
    end = seg_ref[e * 4 + 3]
    n_used = seg_ref[(pl.num_programs(0) - 1) * 4 + 3]

    x_slots = xbuf.shape[0]
    x_ahead = x_slots - 2

    def x_copy(g):
        slot = g & (x_slots - 1)
        return pltpu.make_async_copy(xs_ref.at[pl.ds(g * blk_words, blk_words), :],
                                     xbuf.at[slot], xsem.at[slot])

    def y_copy(g):
        slot = g & 1
        return pltpu.make_async_copy(ybuf.at[slot],
                                     ys_ref.at[pl.ds(g * blk_words, blk_words), :], ysem.at[slot])

    for g0 in range(x_ahead):
        @pl.when(jnp.logical_and(e == 0, n_used > g0))
        def _():
            x_copy(g0).start()

    @pl.when(end > first)
    def _():
        wslot = e & 1
        wgb[...] = wg32[wslot].astype(BF16)
        wub[...] = wu32[wslot].astype(BF16)
        wdb[...] = wd32[wslot].astype(BF16)

    @pl.when(e + 1 < n_exp)
    def _():
        for cp in weight_copies(e + 1):
            cp.start(priority=1)

    @pl.when(end > first)
    def _():
        def block(g, c):
            slot = g & 1
            x_copy(g).wait()

            @pl.when(g + x_ahead < n_used)
            def _():
                x_copy(g + x_ahead).start()

            @pl.when(g >= 2)
            def _():
                y_copy(g - 2).wait()

            xin = xbuf.at[g & (x_slots - 1)]
            his, los = [], []
            for j in range(ROW_WORDS):
                hi, lo = _unpack_pair(xin[pl.ds(j, rows, stride=ROW_WORDS), :])
                his.append(hi.astype(BF16))
                los.append(lo.astype(BF16))
            xb = jnp.concatenate(his + los, axis=1)
            a = jnp.dot(xb, wgb[...], preferred_element_type=F32)
            u = jnp.dot(xb, wub[...], preferred_element_type=F32)
            hid = (_silu(a) * u).astype(BF16)
            out = jnp.dot(hid, wdb[...], preferred_element_type=F32).astype(BF16).astype(F32)
            half = out.shape[1] // 2
            words = _pack_pair(out[:, :half], out[:, half:])
            yout = ybuf.at[slot]
            for j in range(ROW_WORDS):
                yout[pl.ds(j, rows, stride=ROW_WORDS), :] = words[:, j * LANES:(j + 1) * LANES]
            y_copy(g).start()
            return c

        lax.fori_loop(first, end, block, 0)

    @pl.when(last_step)
    def _():
        @pl.when(n_used >= 2)
        def _():
            y_copy(n_used - 2).wait()

        @pl.when(n_used >= 1)
        def _():
            y_copy(n_used - 1).wait()

        ybuf[0] = jnp.zeros(ybuf.shape[1:], ybuf.dtype)

        def fill(g, c):
            pltpu.make_async_copy(ybuf.at[0], ys_ref.at[pl.ds(g * blk_words, blk_words), :],
                                  ysem.at[0]).start()
            return c

        lax.fori_loop(n_used, n_blocks, fill, 0)

        def drain(g, c):
            pltpu.make_async_copy(ybuf.at[0], ys_ref.at[pl.ds(0, blk_words), :], ysem.at[0]).wait()
            return c

        lax.fori_loop(n_used, n_blocks, drain, 0)


def _experts(seg_flat, xs, w_gate, w_up, w_down):
    n_exp, d, f = w_gate.shape
    blk_words = MOE_BLOCK * ROW_WORDS
    grid_spec = pltpu.PrefetchScalarGridSpec(
        num_scalar_prefetch=1,
        grid=(n_exp,),
        in_specs=[
            pl.BlockSpec(memory_space=pl.ANY),
            pl.BlockSpec(memory_space=pl.ANY),
            pl.BlockSpec(memory_space=pl.ANY),
            pl.BlockSpec(memory_space=pl.ANY),
        ],
        out_specs=pl.BlockSpec(memory_space=pl.ANY),
        scratch_shapes=[pltpu.VMEM((2, d, f), F32), pltpu.VMEM((2, d, f), F32), pltpu.VMEM((2, f, d), F32),
                        pltpu.VMEM((d, f), BF16), pltpu.VMEM((d, f), BF16), pltpu.VMEM((f, d), BF16),
                        pltpu.VMEM((4, blk_words, LANES), U32), pltpu.VMEM((2, blk_words, LANES), U32),
                        pltpu.SemaphoreType.DMA((2,)), pltpu.SemaphoreType.DMA((4,)),
                        pltpu.SemaphoreType.DMA((2,))],
    )
    return pl.pallas_call(
        _expert_kernel,
        grid_spec=grid_spec,
        out_shape=jax.ShapeDtypeStruct(xs.shape, U32),
        compiler_params=_cparams(("arbitrary",)),
        name="experts",
    )(seg_flat, xs, w_gate, w_up, w_down)


def _combine_kernel(pos_ref, nxt_ref, ys_ref, wc_ref, shared_ref, x1_ref,
                    g2_ref, nf_ref, scf_ref, shf_ref, o_ref, buf, sem):
    i = pl.program_id(0)
    tq = wc_ref.shape[1]
    slot = i & 1

    def gather(p_ref, to_slot):
        def issue(t, c):
            for k in range(TOP_K):
                pltpu.make_async_copy(
                    ys_ref.at[pl.ds(p_ref[t * TOP_K + k] * ROW_WORDS, ROW_WORDS), :],
                    buf.at[to_slot, k, pl.ds(t * ROW_WORDS, ROW_WORDS), :],
                    sem.at[to_slot]).start(priority=k % 2)
            return c

        lax.fori_loop(0, tq, issue, 0)

    @pl.when(i == 0)
    def _():
        gather(pos_ref, 0)

    @pl.when(i + 1 < pl.num_programs(0))
    def _():
        gather(nxt_ref, 1 - slot)

    for k in range(TOP_K):
        pltpu.make_async_copy(ys_ref.at[pl.ds(0, tq * ROW_WORDS), :], buf.at[slot, k],
                              sem.at[slot]).wait()

    wt = jnp.transpose(wc_ref[...])
    his, los = [], []
    for j in range(ROW_WORDS):
        acc_hi = jnp.zeros((tq, LANES), F32)
        acc_lo = jnp.zeros((tq, LANES), F32)
        for k in range(TOP_K):
            hi, lo = _unpack_pair(buf.at[slot, k][pl.ds(j, tq, stride=ROW_WORDS), :])
            wk = wt[:, k:k + 1]
            acc_hi = acc_hi + wk * hi
            acc_lo = acc_lo + wk * lo
        his.append(acc_hi)
        los.append(acc_lo)
    y = jnp.concatenate(his + los, axis=1) + shared_ref[...].astype(F32)
    x2 = x1_ref[...] + g2_ref[0] * y
    o_ref[...] = _norm_mod(x2, nf_ref[...], scf_ref[0], shf_ref[0])


def _combine(pos_tm, ys, w_c, shared, x1, g2, nf, scf, shf, seq):
    t, d = x1.shape
    tq = 128
    n_tiles = t // tq
    per_batch = seq // tq
    mod = pl.BlockSpec((1, 1, d), lambda i: (i // per_batch, 0, 0))
    return pl.pallas_call(
        _combine_kernel,
        grid=(n_tiles,),
        in_specs=[
            pl.BlockSpec((tq * TOP_K,), lambda i: (i,), memory_space=pltpu.SMEM),
            pl.BlockSpec((tq * TOP_K,), lambda i: (jnp.minimum(i + 1, n_tiles - 1),),
                         memory_space=pltpu.SMEM),
            pl.BlockSpec(memory_space=pl.ANY),
            pl.BlockSpec((TOP_K, tq), lambda i: (0, i)),
            pl.BlockSpec((tq, d), lambda i: (i, 0)),
            pl.BlockSpec((tq, d), lambda i: (i, 0)),
            mod,
            pl.BlockSpec((1, d), lambda i: (0, 0)),
            mod, mod,
        ],
        out_specs=pl.BlockSpec((tq, d), lambda i: (i, 0)),
        out_shape=jax.ShapeDtypeStruct((t, d), F32),
        scratch_shapes=[pltpu.VMEM((2, TOP_K, tq * ROW_WORDS, LANES), U32),
                        pltpu.SemaphoreType.DMA((2,))],
        compiler_params=_cparams(("arbitrary",)),
        name="combine",
    )(pos_tm, pos_tm, ys, w_c, shared, x1, g2, nf.reshape(1, d), scf, shf)


def _rope_tables(seq):
    pos = jnp.arange(seq, dtype=F32)
    inv_freq = ROPE_BASE ** (-jnp.arange(0, HEAD_DIM, 2, dtype=F32) / HEAD_DIM)
    ang = pos[:, None] * inv_freq[None, :]
    cos, sin = jnp.cos(ang), jnp.sin(ang)
    return jnp.concatenate([cos, cos], axis=1), jnp.concatenate([-sin, sin], axis=1)


def kernel(x, c, w_ada, b_ada, norm1_g, w_in, ret_gn_g, w_out, norm2_g, w_router, router_bias,
           w_gate, w_up, w_down, ws_gate, ws_up, ws_down, w_ada_final, b_ada_final, norm_f_g):
    bsz, seq, d = x.shape
    t = bsz * seq
    depth = w_ada.shape[0]
    n_heads = d // (2 * HEAD_DIM)
    sb_w = n_heads * HEAD_DIM
    cos2, sin2 = _rope_tables(seq)
    c_pad = jnp.pad(c, ((0, 8 - bsz % 8), (0, 0)))
    n_rows = t * TOP_K + N_EXPERTS * MOE_BLOCK
    as_mod = lambda v: v.reshape(bsz, 1, d)

    x2 = x.reshape(t, d)
    for l in range(depth):
        mod = _ada(c_pad, w_ada[l], b_ada[l])[:bsz]
        sh1, sc1, g1, sh2, sc2, g2 = [as_mod(m) for m in jnp.split(mod, 6, axis=-1)]
        u = _inproj(x2, norm1_g[l], sc1, sh1, w_in[l].astype(BF16), seq)
        u3 = u.reshape(bsz, seq, -1)
        sb = _sb_attention(u3, n_heads)
        rt = _retention(u3, cos2, sin2, ret_gn_g[l], n_heads, 3 * n_heads)
        wo = w_out[l].astype(BF16)
        x1, h2, hp, lt = _outproj(sb.reshape(t, sb_w), rt.reshape(t, -1), wo[:sb_w], wo[sb_w:], x2,
                                  g1, norm2_g[l], sc2, sh2, jnp.transpose(w_router[l]).astype(BF16), seq)
        w_e, rank_e, cnt = _route(lt, router_bias[l])
        pos_c, w_c, seg = _plan(w_e, rank_e, cnt)
        pos_tm = jnp.transpose(pos_c).reshape(-1)
        xs, shared = _dispatch(pos_tm, seg, hp, h2, ws_gate[l].astype(BF16), ws_up[l].astype(BF16),
                               ws_down[l].astype(BF16), n_rows)
        ys = _experts(seg.reshape(-1), xs, w_gate[l], w_up[l], w_down[l])
        if l + 1 < depth:
            raise NotImplementedError("only the final layer fuses the output norm")
        modf = _ada(c_pad, w_ada_final, b_ada_final)[:bsz]
        shf, scf = [as_mod(m) for m in jnp.split(modf, 2, axis=-1)]
        x2 = _combine(pos_tm, ys, w_c, shared, x1, g2, norm_f_g, scf, shf, seq)
    return x2.reshape(bsz, seq, d)
```

```python
import functools

import jax
import jax.numpy as jnp
from jax import lax
from jax.experimental import pallas as pl
from jax.experimental.pallas import tpu as pltpu

F32 = jnp.float32
BF16 = jnp.bfloat16
I32 = jnp.int32
U32 = jnp.uint32

HEAD_DIM = 128
N_EXPERTS = 64
TOP_K = 8
N_GROUPS = 8
TOPK_GROUP = 4
ROUTED_SCALE = 2.5
EPS = 1e-6
ROPE_BASE = 10000.0

LANES = 128
ROW_WORDS = 8
MOE_BLOCK = 256
V7X_VMEM_LIMIT = 56 * 1024 * 1024
LOG2_E = 1.4426950408889634
SB_SKIP_BOUND = -115.0 * LOG2_E


def _cparams(sem, vmem=V7X_VMEM_LIMIT):
    return pltpu.CompilerParams(dimension_semantics=sem, vmem_limit_bytes=vmem)


def _silu(v):
    return v * jax.nn.sigmoid(v)


def _pack_pair(hi_f32, lo_f32):
    hw = pltpu.bitcast(hi_f32, U32) & jnp.uint32(0xFFFF0000)
    lw = pltpu.bitcast(lo_f32, U32) >> 16
    return hw | lw


def _unpack_pair(w):
    hi = pltpu.bitcast(w & jnp.uint32(0xFFFF0000), F32)
    lo = pltpu.bitcast(w << 16, F32)
    return hi, lo


def _ada_kernel(c_ref, w_ref, b_ref, o_ref):
    cs = _silu(c_ref[...]).astype(BF16)
    o_ref[...] = jnp.dot(cs, w_ref[...].astype(BF16), preferred_element_type=F32) + b_ref[...]


def _ada(c_pad, w, b):
    rows, d = c_pad.shape
    n = w.shape[1]
    tn = 1024
    return pl.pallas_call(
        _ada_kernel,
        grid=(n // tn,),
        in_specs=[
            pl.BlockSpec((rows, d), lambda j: (0, 0)),
            pl.BlockSpec((d, tn), lambda j: (0, j)),
            pl.BlockSpec((1, tn), lambda j: (0, j)),
        ],
        out_specs=pl.BlockSpec((rows, tn), lambda j: (0, j)),
        out_shape=jax.ShapeDtypeStruct((rows, n), F32),
        compiler_params=_cparams(("arbitrary",)),
        name="ada",
    )(c_pad, w, b.reshape(1, n))


def _norm_mod(x, g, sc, sh):
    ms = jnp.mean(x * x, axis=-1, keepdims=True)
    return (x * lax.rsqrt(ms + EPS) * g) * (1.0 + sc) + sh


def _inproj_kernel(x_ref, g_ref, sc_ref, sh_ref, w_ref, o_ref, h_ref, *, chunk):
    first = pl.program_id(1) == 0

    @pl.when(first)
    def _():
        g = g_ref[...]
        sc = sc_ref[0]
        sh = sh_ref[0]
        for r in range(x_ref.shape[0] // chunk):
            rows = pl.ds(r * chunk, chunk)
            h = _norm_mod(x_ref[rows, :], g, sc, sh).astype(BF16)
            h_ref[rows, :] = h
            o_ref[rows, :] = jnp.dot(h, w_ref[...], preferred_element_type=F32).astype(BF16)

    @pl.when(jnp.logical_not(first))
    def _():
        o_ref[...] = jnp.dot(h_ref[...], w_ref[...], preferred_element_type=F32).astype(BF16)


def _inproj(x2, g, sc, sh, w_bf, seq):
    t, d = x2.shape
    n = w_bf.shape[1]
    tm = 1024
    tn = n // 4 if (n // 4) % (2 * LANES) == 0 else 1024
    per_batch = seq // tm
    return pl.pallas_call(
        functools.partial(_inproj_kernel, chunk=256),
        grid=(t // tm, n // tn),
        in_specs=[
            pl.BlockSpec((tm, d), lambda i, j: (i, 0)),
            pl.BlockSpec((1, d), lambda i, j: (0, 0)),
            pl.BlockSpec((1, 1, d), lambda i, j: (i // per_batch, 0, 0)),
            pl.BlockSpec((1, 1, d), lambda i, j: (i // per_batch, 0, 0)),
            pl.BlockSpec((d, tn), lambda i, j: (0, j)),
        ],
        out_specs=pl.BlockSpec((tm, tn), lambda i, j: (i, j)),
        out_shape=jax.ShapeDtypeStruct((t, n), BF16),
        scratch_shapes=[pltpu.VMEM((tm, d), BF16)],
        compiler_params=_cparams(("arbitrary", "arbitrary")),
        name="inproj",
    )(x2, g.reshape(1, d), sc, sh, w_bf)


def _sb_kernel(q_ref, k_ref, v_ref, o_ref, *, group):
    seq = q_ref.shape[0]
    blk = LANES
    scale = HEAD_DIM ** -0.5 * LOG2_E
    row = lax.broadcasted_iota(I32, (blk, blk), 0)
    col = lax.broadcasted_iota(I32, (blk, blk), 1)
    strict = col < row
    tr = lax.broadcasted_iota(I32, (2 * blk, 2 * blk), 0) % blk
    tc = lax.broadcasted_iota(I32, (2 * blk, 2 * blk), 1)
    tri = jnp.where((tc >= blk) | (tr > tc), 1.0, 0.0).astype(BF16)

    def key_rows(j):
        return pl.ds(pl.multiple_of(j * blk, blk), blk)

    def logits(qb, kb):
        z = lax.dot_general(qb, kb, (((1,), (1,)), ((), ())), preferred_element_type=F32) * scale
        sp = jnp.maximum(z, 0.0) + jnp.log2(1.0 + jnp.exp2(-jnp.abs(z)))
        return -sp, z - sp

    def suffix_sums(log_surv):
        hi = log_surv.astype(BF16)
        lo = (log_surv - hi.astype(F32)).astype(BF16)
        r = jnp.dot(jnp.concatenate([hi, lo], axis=1), tri, preferred_element_type=F32)
        return r[:, :blk], r[:, blk:]

    def tile(qb, j, g, carry, acc):
        log_surv, log_beta = logits(qb, k_ref[key_rows(j), g * blk:(g + 1) * blk])
        after, total = suffix_sums(log_surv)
        a = jnp.exp2(log_beta + after + carry)
        acc = acc + jnp.dot(a.astype(BF16), v_ref[key_rows(j), g * blk:(g + 1) * blk],
                            preferred_element_type=F32)
        return carry + total, acc

    def group_start(qbs, i):
        n_near = 3
        js = [i, jnp.maximum(i - 1, 0), jnp.maximum(i - 2, 0)]
        ivec = jnp.full((blk, blk), i, I32)
        masks = [strict, ivec >= 1, ivec >= 2]
        log_betas, parts = [], []
        for g in range(group):
            cols = slice(g * blk, (g + 1) * blk)
            kcat = jnp.concatenate([k_ref[key_rows(j), cols] for j in js], axis=0)
            log_surv, log_beta = logits(qbs[g], kcat)
            log_betas.append(log_beta)
            for m in range(n_near):
                parts.append(jnp.where(masks[m], log_surv[:, m * blk:(m + 1) * blk], 0.0))
        after, total = suffix_sums(jnp.concatenate(parts, axis=0))
        out = []
        for g in range(group):
            cols = slice(g * blk, (g + 1) * blk)
            carry = jnp.zeros((blk, blk), F32)
            weights = []
            for m in range(n_near):
                rows = slice((g * n_near + m) * blk, (g * n_near + m + 1) * blk)
                a = jnp.exp2(log_betas[g][:, m * blk:(m + 1) * blk] + after[rows] + carry)
                weights.append(jnp.where(masks[m], a, 0.0).astype(BF16))
                carry = carry + total[rows]
            vcat = jnp.concatenate([v_ref[key_rows(j), cols] for j in js], axis=0)
            acc = jnp.dot(jnp.concatenate(weights, axis=1), vcat, preferred_element_type=F32)
            out.append((carry, acc))
        return out

    q_blocks = q_ref.shape[0] // blk

    def q_block(local, c):
        i = pl.program_id(1) * q_blocks + local
        rows = pl.ds(pl.multiple_of(local * blk, blk), blk)
        qbs = [q_ref[rows, g * blk:(g + 1) * blk] for g in range(group)]
        first = group_start(qbs, i)
        carries = tuple(f[0] for f in first)
        accs = tuple(f[1] for f in first)

        def cond(st):
            j, crs, _ = st
            top = functools.reduce(jnp.maximum, crs)
            return jnp.logical_and(j >= 0, jnp.max(top) > SB_SKIP_BOUND)

        def body(st):
            j, crs, acs = st
            nxt = [tile(qbs[g], j, g, crs[g], acs[g]) for g in range(group)]
            return j - 1, tuple(n[0] for n in nxt), tuple(n[1] for n in nxt)

        _, _, accs = lax.while_loop(cond, body, (i - 3, carries, accs))
        for g in range(group):
            o_ref[rows, g * blk:(g + 1) * blk] = accs[g].astype(BF16)
        return c

    lax.fori_loop(0, q_blocks, q_block, 0)


SB_ROW_TILE = 512


def _sb_attention(u3, n_heads):
    b, s, _ = u3.shape
    w = n_heads * HEAD_DIM
    tq = min(SB_ROW_TILE, s)
    return pl.pallas_call(
        functools.partial(_sb_kernel, group=n_heads),
        grid=(b, s // tq),
        in_specs=[pl.BlockSpec((None, tq, w), lambda bi, qi: (bi, qi, 0)),
                  pl.BlockSpec((None, s, w), lambda bi, qi: (bi, 0, 1)),
                  pl.BlockSpec((None, s, w), lambda bi, qi: (bi, 0, 2))],
        out_specs=pl.BlockSpec((None, tq, w), lambda bi, qi: (bi, qi, 0)),
        out_shape=jax.ShapeDtypeStruct((b, s, w), BF16),
        compiler_params=_cparams(("arbitrary", "arbitrary")),
        name="sb_attn",
    )(u3, u3, u3)


def _ret_kernel(q_ref, k_ref, v_ref, g_ref, cos_ref, sin_ref, gn_ref, o_ref, *, chunk, unroll):
    seq = q_ref.shape[0]
    dh = HEAD_DIM
    head = pl.program_id(1)

    def log_gamma(shape):
        hv = jnp.full(shape, head, I32).astype(F32)
        return jnp.log(1.0 - jnp.exp2(-5.0 - hv))

    diff = (lax.broadcasted_iota(I32, (chunk, chunk), 0)
            - lax.broadcasted_iota(I32, (chunk, chunk), 1)).astype(F32)
    lower = diff >= 0.0
    dmat = jnp.where(lower, jnp.exp(jnp.where(lower, diff, 0.0) * log_gamma((chunk, chunk))), 0.0)
    pos = lax.broadcasted_iota(I32, (chunk, dh), 0).astype(F32)
    k_decay = jnp.exp((chunk - 1.0 - pos) * log_gamma((chunk, dh)))
    q_decay = jnp.exp((pos + 1.0) * log_gamma((chunk, dh)))
    chunk_decay = jnp.exp(chunk * log_gamma((dh, dh)))
    gn = gn_ref[...]
    scale = dh ** -0.5

    def rope(t, cs, sn):
        return t * cs + pltpu.roll(t, dh // 2, 1) * sn

    def body(n, state):
        rows = pl.ds(pl.multiple_of(n * chunk, chunk), chunk)
        cs = cos_ref[rows, :]
        sn = sin_ref[rows, :]
        qc = rope(q_ref[rows, :].astype(F32), cs, sn)
        kc = rope(k_ref[rows, :].astype(F32), cs, sn) * scale
        vc = v_ref[rows, :]
        scores = lax.dot_general(qc.astype(BF16), kc.astype(BF16), (((1,), (1,)), ((), ())),
                                 preferred_element_type=F32) * dmat
        intra = jnp.dot(scores.astype(BF16), vc, preferred_element_type=F32)
        cross = jnp.dot((qc * q_decay).astype(BF16), state.astype(BF16), preferred_element_type=F32)
        out = intra + cross
        kd_t = jnp.transpose(kc * k_decay).astype(BF16)
        state = state * chunk_decay + jnp.dot(kd_t, vc, preferred_element_type=F32)
        ms = jnp.mean(out * out, axis=-1, keepdims=True)
        y = out * lax.rsqrt(ms + EPS) * gn * _silu(g_ref[rows, :].astype(F32))
        o_ref[rows, :] = y.astype(BF16)
        return state

    lax.fori_loop(0, seq // chunk, body, jnp.zeros((dh, dh), F32), unroll=unroll)


RET_CHUNK = 512
RET_UNROLL = 2


def _retention(u3, cos2, sin2, gn, n_heads, col0):
    b, s, _ = u3.shape
    spec = lambda off: pl.BlockSpec((None, s, HEAD_DIM), lambda bi, h: (bi, 0, col0 + off + h))
    return pl.pallas_call(
        functools.partial(_ret_kernel, chunk=RET_CHUNK, unroll=RET_UNROLL),
        grid=(b, n_heads),
        in_specs=[spec(0), spec(n_heads), spec(2 * n_heads), spec(3 * n_heads),
                  pl.BlockSpec((s, HEAD_DIM), lambda bi, h: (0, 0)),
                  pl.BlockSpec((s, HEAD_DIM), lambda bi, h: (0, 0)),
                  pl.BlockSpec((1, HEAD_DIM), lambda bi, h: (0, h))],
        out_specs=pl.BlockSpec((None, s, HEAD_DIM), lambda bi, h: (bi, 0, h)),
        out_shape=jax.ShapeDtypeStruct((b, s, n_heads * HEAD_DIM), BF16),
        compiler_params=_cparams(("arbitrary", "arbitrary")),
        name="retention",
    )(u3, u3, u3, u3, cos2, sin2, gn.reshape(1, -1))


def _outproj_kernel(sb_ref, rt_ref, wa_ref, wb_ref, x_ref, g1_ref, n2_ref, sc_ref, sh_ref, wr_ref,
                    x1_ref, h2_ref, hp_ref, lt_ref, *, chunk):
    d = x_ref.shape[1]
    half = d // 2
    g1 = g1_ref[0]
    n2 = n2_ref[...]
    sc = sc_ref[0]
    sh = sh_ref[0]
    for r in range(x_ref.shape[0] // chunk):
        rows = pl.ds(r * chunk, chunk)
        o = jnp.dot(sb_ref[rows, :], wa_ref[...], preferred_element_type=F32)
        o = o + jnp.dot(rt_ref[rows, :], wb_ref[...], preferred_element_type=F32)
        x1 = x_ref[rows, :] + g1 * o
        x1_ref[rows, :] = x1
        h2 = _norm_mod(x1, n2, sc, sh).astype(BF16)
        h2_ref[rows, :] = h2
        lt_ref[:, rows] = lax.dot_general(wr_ref[...], h2, (((1,), (1,)), ((), ())),
                                          preferred_element_type=F32)
        words = _pack_pair(h2[:, :half].astype(F32), h2[:, half:].astype(F32))
        for j in range(ROW_WORDS):
            hp_ref[pl.ds(r * chunk * ROW_WORDS + j, chunk, stride=ROW_WORDS), :] = (
                words[:, j * LANES:(j + 1) * LANES])


def _outproj(sb2, rt2, wa, wb, x2, g1, n2, sc2, sh2, wr_t, seq):
    t, d = x2.shape
    tm = 512
    per_batch = seq // tm
    mod = pl.BlockSpec((1, 1, d), lambda i: (i // per_batch, 0, 0))
    return pl.pallas_call(
        functools.partial(_outproj_kernel, chunk=512),
        grid=(t // tm,),
        in_specs=[
            pl.BlockSpec((tm, sb2.shape[1]), lambda i: (i, 0)),
            pl.BlockSpec((tm, rt2.shape[1]), lambda i: (i, 0)),
            pl.BlockSpec(wa.shape, lambda i: (0, 0)),
            pl.BlockSpec(wb.shape, lambda i: (0, 0)),
            pl.BlockSpec((tm, d), lambda i: (i, 0)),
            mod,
            pl.BlockSpec((1, d), lambda i: (0, 0)),
            mod, mod,
            pl.BlockSpec(wr_t.shape, lambda i: (0, 0)),
        ],
        out_specs=[
            pl.BlockSpec((tm, d), lambda i: (i, 0)),
            pl.BlockSpec((tm, d), lambda i: (i, 0)),
            pl.BlockSpec((tm * ROW_WORDS, LANES), lambda i: (i, 0)),
            pl.BlockSpec((N_EXPERTS, tm), lambda i: (0, i)),
        ],
        out_shape=[
            jax.ShapeDtypeStruct((t, d), F32),
            jax.ShapeDtypeStruct((t, d), BF16),
            jax.ShapeDtypeStruct((t * ROW_WORDS, LANES), U32),
            jax.ShapeDtypeStruct((N_EXPERTS, t), F32),
        ],
        compiler_params=_cparams(("arbitrary",)),
        name="outproj",
    )(sb2, rt2, wa, wb, x2, g1, n2.reshape(1, d), sc2, sh2, wr_t)


def _beats(row, allv, row_idx, idx):
    return (row > allv) | ((row == allv) & (row_idx < idx))


def _route_kernel(lt_ref, bias_ref, h2_ref, sg_ref, su_ref, sd_ref,
                  w_ref, rank_ref, cnt_ref, shared_ref, carry_ref):
    tr = lt_ref.shape[1]
    gsz = N_EXPERTS // N_GROUPS
    neg_inf = jnp.float32(-jnp.inf)

    @pl.when(pl.program_id(0) == 0)
    def _():
        carry_ref[...] = jnp.zeros_like(carry_ref)

    h2 = h2_ref[...]
    sa = jnp.dot(h2, sg_ref[...], preferred_element_type=F32)
    su = jnp.dot(h2, su_ref[...], preferred_element_type=F32)
    shared_ref[...] = jnp.dot((_silu(sa) * su).astype(BF16), sd_ref[...],
                              preferred_element_type=F32).astype(BF16)

    scores = jax.nn.sigmoid(lt_ref[...])
    choice = scores + bias_ref[...]
    sub = lax.broadcasted_iota(I32, (gsz, tr), 0)
    gs_rows = []
    for g in range(N_GROUPS):
        cg = choice[g * gsz:(g + 1) * gsz, :]
        m1 = jnp.max(cg, axis=0, keepdims=True)
        first = jnp.min(jnp.where(cg == m1, sub, gsz), axis=0, keepdims=True)
        m2 = jnp.max(jnp.where(sub == first, neg_inf, cg), axis=0, keepdims=True)
        gs_rows.append(m1 + m2)
    gs = jnp.concatenate(gs_rows, axis=0)
    gi = lax.broadcasted_iota(I32, (N_GROUPS, tr), 0)
    grank = jnp.zeros((N_GROUPS, tr), I32)
    for g in range(N_GROUPS):
        grank = grank + _beats(gs[g:g + 1, :], gs, g, gi).astype(I32)
    gmask = grank < TOPK_GROUP
    emask = jnp.concatenate(
        [jnp.broadcast_to(gmask[g:g + 1, :], (gsz, tr)) for g in range(N_GROUPS)], axis=0)
    masked = jnp.where(emask, choice, neg_inf)
    ei = lax.broadcasted_iota(I32, (N_EXPERTS, tr), 0)
    erank = jnp.zeros((N_EXPERTS, tr), I32)
    for e in range(N_EXPERTS):
        erank = erank + _beats(masked[e:e + 1, :], masked, e, ei).astype(I32)
    sel = erank < TOP_K
    ssel = jnp.where(sel, scores, 0.0)
    denom = jnp.sum(ssel, axis=0, keepdims=True)
    w_ref[...] = ssel / denom * ROUTED_SCALE

    self = sel.astype(F32)
    upper = (lax.broadcasted_iota(I32, (tr, tr), 0) < lax.broadcasted_iota(I32, (tr, tr), 1))
    prefix = jnp.dot(self.astype(BF16), upper.astype(BF16), preferred_element_type=F32)
    carry = carry_ref[...]
    rank = prefix + carry[:, 0:1]
    rank_ref[...] = jnp.where(sel, rank, -1.0).astype(I32)
    carry = carry + jnp.sum(self, axis=1, keepdims=True)
    carry_ref[...] = carry
    cnt_ref[...] = carry


def _route(lt, bias, h2, sg, su, sd):
    e, t = lt.shape
    d = h2.shape[1]
    tr = min(512, t)
    full = lambda a: pl.BlockSpec(a.shape, lambda i: (0, 0))
    return pl.pallas_call(
        _route_kernel,
        grid=(t // tr,),
        in_specs=[pl.BlockSpec((e, tr), lambda i: (0, i)),
                  pl.BlockSpec((e, 1), lambda i: (0, 0)),
                  pl.BlockSpec((tr, d), lambda i: (i, 0)),
                  full(sg), full(su), full(sd)],
        out_specs=[pl.BlockSpec((e, tr), lambda i: (0, i)),
                   pl.BlockSpec((e, tr), lambda i: (0, i)),
                   pl.BlockSpec((e, LANES), lambda i: (0, 0)),
                   pl.BlockSpec((tr, d), lambda i: (i, 0))],
        out_shape=[jax.ShapeDtypeStruct((e, t), F32),
                   jax.ShapeDtypeStruct((e, t), I32),
                   jax.ShapeDtypeStruct((e, LANES), F32),
                   jax.ShapeDtypeStruct((t, d), BF16)],
        scratch_shapes=[pltpu.VMEM((e, LANES), F32)],
        compiler_params=_cparams(("arbitrary",)),
        name="route",
    )(lt, bias.reshape(e, 1), h2, sg, su, sd)


def _plan_kernel(w_ref, rank_ref, cnt_ref, pos_ref, wc_ref, seg_ref):
    cnt = cnt_ref[...]
    nblk = jnp.floor((cnt + (MOE_BLOCK - 1.0)) * (1.0 / MOE_BLOCK))
    lower = (lax.broadcasted_iota(I32, (N_EXPERTS, N_EXPERTS), 1)
             < lax.broadcasted_iota(I32, (N_EXPERTS, N_EXPERTS), 0)).astype(BF16)
    bstart = jnp.dot(lower, nblk.astype(BF16), preferred_element_type=F32)
    bend = bstart + nblk

    @pl.when(pl.program_id(0) == 0)
    def _():
        seg_ref[...] = jnp.concatenate(
            [bstart[:, 0:1] * MOE_BLOCK, cnt[:, 0:1], nblk[:, 0:1] * MOE_BLOCK, bend[:, 0:1]],
            axis=1).astype(I32)

    rank = rank_ref[...]
    sel = rank >= 0
    pos = bstart[:, 0:1] * MOE_BLOCK + rank.astype(F32)
    slot = jnp.dot(lower, sel.astype(BF16), preferred_element_type=F32)
    w = w_ref[...]
    pos_rows, w_rows = [], []
    for k in range(TOP_K):
        m = sel & (slot == float(k))
        pos_rows.append(jnp.sum(jnp.where(m, pos, 0.0), axis=0, keepdims=True))
        w_rows.append(jnp.sum(jnp.where(m, w, 0.0), axis=0, keepdims=True))
    pos_ref[...] = jnp.concatenate(pos_rows, axis=0).astype(I32)
    wc_ref[...] = jnp.concatenate(w_rows, axis=0)


def _plan(w, rank, cnt):
    e, t = w.shape
    tr = min(1024, t)
    return pl.pallas_call(
        _plan_kernel,
        grid=(t // tr,),
        in_specs=[pl.BlockSpec((e, tr), lambda i: (0, i)),
                  pl.BlockSpec((e, tr), lambda i: (0, i)),
                  pl.BlockSpec((e, LANES), lambda i: (0, 0))],
        out_specs=[pl.BlockSpec((TOP_K, tr), lambda i: (0, i)),
                   pl.BlockSpec((TOP_K, tr), lambda i: (0, i)),
                   pl.BlockSpec((e, 4), lambda i: (0, 0))],
        out_shape=[jax.ShapeDtypeStruct((TOP_K, t), I32),
                   jax.ShapeDtypeStruct((TOP_K, t), F32),
                   jax.ShapeDtypeStruct((e, 4), I32)],
        compiler_params=_cparams(("arbitrary",)),
        name="plan",
    )(w, rank, cnt)


def _row_copy(src, src_row, dst, dst_row, sem):
    return pltpu.make_async_copy(src.at[pl.ds(src_row * ROW_WORDS, ROW_WORDS), :],
                                 dst.at[pl.ds(dst_row * ROW_WORDS, ROW_WORDS), :], sem)


def _dispatch_kernel(pos_ref, seg_ref, hp_ref, xs_ref, zero_ref, sem, zsem):
    tq = pos_ref.shape[0] // TOP_K
    blk_words = MOE_BLOCK * ROW_WORDS
    n_blocks = xs_ref.shape[0] // blk_words

    @pl.when(pl.program_id(0) == 0)
    def _():
        zero_ref[...] = jnp.zeros_like(zero_ref)
        zrow = zero_ref.at[pl.ds(0, ROW_WORDS), :]

        def per_expert(e, n):
            start = seg_ref[e, 0] + seg_ref[e, 1]
            npad = seg_ref[e, 2] - seg_ref[e, 1]

            def fill(r, c):
                pltpu.make_async_copy(
                    zrow, xs_ref.at[pl.ds((start + r) * ROW_WORDS, ROW_WORDS), :], zsem).start()
                return c

            lax.fori_loop(0, npad, fill, 0)
            return n + npad

        n_pad_rows = lax.fori_loop(0, N_EXPERTS, per_expert, 0)

        def drain_row(r, c):
            pltpu.make_async_copy(zrow, xs_ref.at[pl.ds(0, ROW_WORDS), :], zsem).wait()
            return c

        lax.fori_loop(0, n_pad_rows, drain_row, 0)
        n_used = seg_ref[N_EXPERTS - 1, 3]

        def fill_block(b, c):
            pltpu.make_async_copy(
                zero_ref, xs_ref.at[pl.ds(b * blk_words, blk_words), :], zsem).start()
            return c

        lax.fori_loop(n_used, n_blocks, fill_block, 0)

        def drain_block(b, c):
            pltpu.make_async_copy(zero_ref, xs_ref.at[pl.ds(0, blk_words), :], zsem).wait()
            return c

        lax.fori_loop(n_used, n_blocks, drain_block, 0)

    def issue(t, c):
        for k in range(TOP_K):
            _row_copy(hp_ref, t, xs_ref, pos_ref[t * TOP_K + k], sem).start(priority=k % 2)
        return c

    lax.fori_loop(0, tq, issue, 0)
    for k in range(TOP_K):
        pltpu.make_async_copy(hp_ref, xs_ref.at[pl.ds(0, tq * ROW_WORDS), :], sem).wait()


def _dispatch(pos_tm, seg, hp, n_rows):
    t = pos_tm.shape[0] // TOP_K
    tq = min(1024, t)
    return pl.pallas_call(
        _dispatch_kernel,
        grid=(t // tq,),
        in_specs=[pl.BlockSpec((tq * TOP_K,), lambda i: (i,), memory_space=pltpu.SMEM),
                  pl.BlockSpec(seg.shape, lambda i: (0, 0), memory_space=pltpu.SMEM),
                  pl.BlockSpec((tq * ROW_WORDS, LANES), lambda i: (i, 0))],
        out_specs=pl.BlockSpec(memory_space=pl.ANY),
        out_shape=jax.ShapeDtypeStruct((n_rows * ROW_WORDS, LANES), U32),
        scratch_shapes=[pltpu.VMEM((MOE_BLOCK * ROW_WORDS, LANES), U32),
                        pltpu.SemaphoreType.DMA(()), pltpu.SemaphoreType.DMA(())],
        compiler_params=_cparams(("arbitrary",)),
        name="dispatch",
    )(pos_tm, seg, hp)


def _expert_kernel(seg_ref, xs_ref, wg_ref, wu_ref, wd_ref, ys_ref,
                   wg32, wu32, wd32, wgb, wub, wdb, xbuf, ybuf, wsem, xsem, ysem):
    e = pl.program_id(0)
    n_exp = pl.num_programs(0)
    last_step = e == n_exp - 1

    def weight_copies(ex):
        slot = ex & 1
        return [pltpu.make_async_copy(src.at[ex], dst.at[slot], wsem.at[slot])
                for src, dst in ((wg_ref, wg32), (wu_ref, wu32), (wd_ref, wd32))]

    @pl.when(e == 0)
    def _():
        for cp in weight_copies(0):
            cp.start()

    for cp in weight_copies(e):
        cp.wait()
    rows = MOE_BLOCK
    blk_words = MOE_BLOCK * ROW_WORDS
    n_blocks = ys_ref.shape[0] // blk_words
    first = lax.div(seg_ref[e * 4], MOE_BLOCK)
    end = seg_ref[e * 4 + 3]
    n_used = seg_ref[(pl.num_programs(0) - 1) * 4 + 3]

    x_slots = xbuf.shape[0]
    x_ahead = x_slots - 2

    def x_copy(g):
        slot = g & (x_slots - 1)
        return pltpu.make_async_copy(xs_ref.at[pl.ds(g * blk_words, blk_words), :],
                                     xbuf.at[slot], xsem.at[slot])

    def y_copy(g):
        slot = g & 1
        return pltpu.make_async_copy(ybuf.at[slot],
                                     ys_ref.at[pl.ds(g * blk_words, blk_words), :], ysem.at[slot])

    for g0 in range(x_ahead):
        @pl.when(jnp.logical_and(e == 0, n_used > g0))
        def _():
            x_copy(g0).start()

    @pl.when(end > first)
    def _():
        wslot = e & 1
        wgb[...] = wg32[wslot].astype(BF16)
        wub[...] = wu32[wslot].astype(BF16)
        wdb[...] = wd32[wslot].astype(BF16)

    @pl.when(e + 1 < n_exp)
    def _():
        for cp in weight_copies(e + 1):
            cp.start(priority=1)

    @pl.when(end > first)
    def _():
        def block(g, c):
            slot = g & 1
            x_copy(g).wait()

            @pl.when(g + x_ahead < n_used)
            def _():
                x_copy(g + x_ahead).start()

            @pl.when(g >= 2)
            def _():
                y_copy(g - 2).wait()

            xin = xbuf.at[g & (x_slots - 1)]
            his, los = [], []
            for j in range(ROW_WORDS):
                hi, lo = _unpack_pair(xin[pl.ds(j, rows, stride=ROW_WORDS), :])
                his.append(hi.astype(BF16))
                los.append(lo.astype(BF16))
            xb = jnp.concatenate(his + los, axis=1)
            a = jnp.dot(xb, wgb[...], preferred_element_type=F32)
            u = jnp.dot(xb, wub[...], preferred_element_type=F32)
            hid = (_silu(a) * u).astype(BF16)
            out = jnp.dot(hid, wdb[...], preferred_element_type=F32).astype(BF16).astype(F32)
            half = out.shape[1] // 2
            words = _pack_pair(out[:, :half], out[:, half:])
            yout = ybuf.at[slot]
            for j in range(ROW_WORDS):
                yout[pl.ds(j, rows, stride=ROW_WORDS), :] = words[:, j * LANES:(j + 1) * LANES]
            y_copy(g).start()
            return c

        lax.fori_loop(first, end, block, 0)

    @pl.when(last_step)
    def _():
        @pl.when(n_used >= 2)
        def _():
            y_copy(n_used - 2).wait()

        @pl.when(n_used >= 1)
        def _():
            y_copy(n_used - 1).wait()

        ybuf[0] = jnp.zeros(ybuf.shape[1:], ybuf.dtype)

        def fill(g, c):
            pltpu.make_async_copy(ybuf.at[0], ys_ref.at[pl.ds(g * blk_words, blk_words), :],
                                  ysem.at[0]).start()
            return c

        lax.fori_loop(n_used, n_blocks, fill, 0)

        def drain(g, c):
            pltpu.make_async_copy(ybuf.at[0], ys_ref.at[pl.ds(0, blk_words), :], ysem.at[0]).wait()
            return c

        lax.fori_loop(n_used, n_blocks, drain, 0)


def _experts(seg_flat, xs, w_gate, w_up, w_down):
    n_exp, d, f = w_gate.shape
    blk_words = MOE_BLOCK * ROW_WORDS
    grid_spec = pltpu.PrefetchScalarGridSpec(
        num_scalar_prefetch=1,
        grid=(n_exp,),
        in_specs=[
            pl.BlockSpec(memory_space=pl.ANY),
            pl.BlockSpec(memory_space=pl.ANY),
            pl.BlockSpec(memory_space=pl.ANY),
            pl.BlockSpec(memory_space=pl.ANY),
        ],
        out_specs=pl.BlockSpec(memory_space=pl.ANY),
        scratch_shapes=[pltpu.VMEM((2, d, f), F32), pltpu.VMEM((2, d, f), F32), pltpu.VMEM((2, f, d), F32),
                        pltpu.VMEM((d, f), BF16), pltpu.VMEM((d, f), BF16), pltpu.VMEM((f, d), BF16),
                        pltpu.VMEM((4, blk_words, LANES), U32), pltpu.VMEM((2, blk_words, LANES), U32),
                        pltpu.SemaphoreType.DMA((2,)), pltpu.SemaphoreType.DMA((4,)),
                        pltpu.SemaphoreType.DMA((2,))],
    )
    return pl.pallas_call(
        _expert_kernel,
        grid_spec=grid_spec,
        out_shape=jax.ShapeDtypeStruct(xs.shape, U32),
        compiler_params=_cparams(("arbitrary",)),
        name="experts",
    )(seg_flat, xs, w_gate, w_up, w_down)


def _combine_kernel(pos_ref, nxt_ref, ys_ref, wc_ref, shared_ref, x1_ref,
                    g2_ref, nf_ref, scf_ref, shf_ref, o_ref, buf, sem):
    i = pl.program_id(0)
    tq = wc_ref.shape[1]
    slot = i & 1

    def gather(p_ref, to_slot):
        def issue(t, c):
            for k in range(TOP_K):
                pltpu.make_async_copy(
                    ys_ref.at[pl.ds(p_ref[t * TOP_K + k] * ROW_WORDS, ROW_WORDS), :],
                    buf.at[to_slot, k, pl.ds(t * ROW_WORDS, ROW_WORDS), :],
                    sem.at[to_slot]).start(priority=k % 2)
            return c

        lax.fori_loop(0, tq, issue, 0)

    @pl.when(i == 0)
    def _():
        gather(pos_ref, 0)

    @pl.when(i + 1 < pl.num_programs(0))
    def _():
        gather(nxt_ref, 1 - slot)

    for k in range(TOP_K):
        pltpu.make_async_copy(ys_ref.at[pl.ds(0, tq * ROW_WORDS), :], buf.at[slot, k],
                              sem.at[slot]).wait()

    wt = jnp.transpose(wc_ref[...])
    his, los = [], []
    for j in range(ROW_WORDS):
        acc_hi = jnp.zeros((tq, LANES), F32)
        acc_lo = jnp.zeros((tq, LANES), F32)
        for k in range(TOP_K):
            hi, lo = _unpack_pair(buf.at[slot, k][pl.ds(j, tq, stride=ROW_WORDS), :])
            wk = wt[:, k:k + 1]
            acc_hi = acc_hi + wk * hi
            acc_lo = acc_lo + wk * lo
        his.append(acc_hi)
        los.append(acc_lo)
    y = jnp.concatenate(his + los, axis=1) + shared_ref[...].astype(F32)
    x2 = x1_ref[...] + g2_ref[0] * y
    o_ref[...] = _norm_mod(x2, nf_ref[...], scf_ref[0], shf_ref[0])


def _combine(pos_tm, ys, w_c, shared, x1, g2, nf, scf, shf, seq):
    t, d = x1.shape
    tq = 128
    n_tiles = t // tq
    per_batch = seq // tq
    mod = pl.BlockSpec((1, 1, d), lambda i: (i // per_batch, 0, 0))
    return pl.pallas_call(
        _combine_kernel,
        grid=(n_tiles,),
        in_specs=[
            pl.BlockSpec((tq * TOP_K,), lambda i: (i,), memory_space=pltpu.SMEM),
            pl.BlockSpec((tq * TOP_K,), lambda i: (jnp.minimum(i + 1, n_tiles - 1),),
                         memory_space=pltpu.SMEM),
            pl.BlockSpec(memory_space=pl.ANY),
            pl.BlockSpec((TOP_K, tq), lambda i: (0, i)),
            pl.BlockSpec((tq, d), lambda i: (i, 0)),
            pl.BlockSpec((tq, d), lambda i: (i, 0)),
            mod,
            pl.BlockSpec((1, d), lambda i: (0, 0)),
            mod, mod,
        ],
        out_specs=pl.BlockSpec((tq, d), lambda i: (i, 0)),
        out_shape=jax.ShapeDtypeStruct((t, d), F32),
        scratch_shapes=[pltpu.VMEM((2, TOP_K, tq * ROW_WORDS, LANES), U32),
                        pltpu.SemaphoreType.DMA((2,))],
        compiler_params=_cparams(("arbitrary",)),
        name="combine",
    )(pos_tm, pos_tm, ys, w_c, shared, x1, g2, nf.reshape(1, d), scf, shf)


def _rope_tables(seq):
    pos = jnp.arange(seq, dtype=F32)
    inv_freq = ROPE_BASE ** (-jnp.arange(0, HEAD_DIM, 2, dtype=F32) / HEAD_DIM)
    ang = pos[:, None] * inv_freq[None, :]
    cos, sin = jnp.cos(ang), jnp.sin(ang)
    return jnp.concatenate([cos, cos], axis=1), jnp.concatenate([-sin, sin], axis=1)


def kernel(x, c, w_ada, b_ada, norm1_g, w_in, ret_gn_g, w_out, norm2_g, w_router, router_bias,
           w_gate, w_up, w_down, ws_gate, ws_up, ws_down, w_ada_final, b_ada_final, norm_f_g):
    bsz, seq, d = x.shape
    t = bsz * seq
    depth = w_ada.shape[0]
    n_heads = d // (2 * HEAD_DIM)
    sb_w = n_heads * HEAD_DIM
    cos2, sin2 = _rope_tables(seq)
    c_pad = jnp.pad(c, ((0, 8 - bsz % 8), (0, 0)))
    n_rows = t * TOP_K + N_EXPERTS * MOE_BLOCK
    as_mod = lambda v: v.reshape(bsz, 1, d)

    x2 = x.reshape(t, d)
    for l in range(depth):
        mod = _ada(c_pad, w_ada[l], b_ada[l])[:bsz]
        sh1, sc1, g1, sh2, sc2, g2 = [as_mod(m) for m in jnp.split(mod, 6, axis=-1)]
        u = _inproj(x2, norm1_g[l], sc1, sh1, w_in[l].astype(BF16), seq)
        u3 = u.reshape(bsz, seq, -1)
        sb = _sb_attention(u3, n_heads)
        rt = _retention(u3, cos2, sin2, ret_gn_g[l], n_heads, 3 * n_heads)
        wo = w_out[l].astype(BF16)
        x1, h2, hp, lt = _outproj(sb.reshape(t, sb_w), rt.reshape(t, -1), wo[:sb_w], wo[sb_w:], x2,
                                  g1, norm2_g[l], sc2, sh2, jnp.transpose(w_router[l]).astype(BF16), seq)
        w_e, rank_e, cnt, shared = _route(lt, router_bias[l], h2, ws_gate[l].astype(BF16),
                                          ws_up[l].astype(BF16), ws_down[l].astype(BF16))
        pos_c, w_c, seg = _plan(w_e, rank_e, cnt)
        pos_tm = jnp.transpose(pos_c).reshape(-1)
        xs = _dispatch(pos_tm, seg, hp, n_rows)
        ys = _experts(seg.reshape(-1), xs, w_gate[l], w_up[l], w_down[l])
        if l + 1 < depth:
            raise NotImplementedError("only the final layer fuses the output norm")
        modf = _ada(c_pad, w_ada_final, b_ada_final)[:bsz]
        shf, scf = [as_mod(m) for m in jnp.split(modf, 2, axis=-1)]
        x2 = _combine(pos_tm, ys, w_c, shared, x1, g2, norm_f_g, scf, shf, seq)
    return x2.reshape(bsz, seq, d)
```

```python
import functools

import jax
import jax.numpy as jnp
from jax import lax
from jax.experimental import pallas as pl
from jax.experimental.pallas import tpu as pltpu

F32 = jnp.float32
BF16 = jnp.bfloat16
I32 = jnp.int32
U32 = jnp.uint32

HEAD_DIM = 128
N_EXPERTS = 64
TOP_K = 8
N_GROUPS = 8
TOPK_GROUP = 4
ROUTED_SCALE = 2.5
EPS = 1e-6
ROPE_BASE = 10000.0

LANES = 128
ROW_WORDS = 8
MOE_BLOCK = 256
V7X_VMEM_LIMIT = 56 * 1024 * 1024
LOG2_E = 1.4426950408889634
SB_SKIP_BOUND = -115.0 * LOG2_E


def _cparams(sem, vmem=V7X_VMEM_LIMIT):
    return pltpu.CompilerParams(dimension_semantics=sem, vmem_limit_bytes=vmem)


def _silu(v):
    return v * jax.nn.sigmoid(v)


def _pack_pair(hi_f32, lo_f32):
    hw = pltpu.bitcast(hi_f32, U32) & jnp.uint32(0xFFFF0000)
    lw = pltpu.bitcast(lo_f32, U32) >> 16
    return hw | lw


def _unpack_pair(w):
    hi = pltpu.bitcast(w & jnp.uint32(0xFFFF0000), F32)
    lo = pltpu.bitcast(w << 16, F32)
    return hi, lo


def _ada_kernel(c_ref, w_ref, b_ref, o_ref):
    cs = _silu(c_ref[...]).astype(BF16)
    o_ref[...] = jnp.dot(cs, w_ref[...].astype(BF16), preferred_element_type=F32) + b_ref[...]


def _ada(c_pad, w, b):
    rows, d = c_pad.shape
    n = w.shape[1]
    tn = 1024
    return pl.pallas_call(
        _ada_kernel,
        grid=(n // tn,),
        in_specs=[
            pl.BlockSpec((rows, d), lambda j: (0, 0)),
            pl.BlockSpec((d, tn), lambda j: (0, j)),
            pl.BlockSpec((1, tn), lambda j: (0, j)),
        ],
        out_specs=pl.BlockSpec((rows, tn), lambda j: (0, j)),
        out_shape=jax.ShapeDtypeStruct((rows, n), F32),
        compiler_params=_cparams(("arbitrary",)),
        name="ada",
    )(c_pad, w, b.reshape(1, n))


def _norm_mod(x, g, sc, sh):
    ms = jnp.mean(x * x, axis=-1, keepdims=True)
    return (x * lax.rsqrt(ms + EPS) * g) * (1.0 + sc) + sh


def _inproj_kernel(x_ref, g_ref, sc_ref, sh_ref, w_ref, o_ref, h_ref, *, chunk):
    first = pl.program_id(1) == 0

    @pl.when(first)
    def _():
        g = g_ref[...]
        sc = sc_ref[0]
        sh = sh_ref[0]
        for r in range(x_ref.shape[0] // chunk):
            rows = pl.ds(r * chunk, chunk)
            h = _norm_mod(x_ref[rows, :], g, sc, sh).astype(BF16)
            h_ref[rows, :] = h
            o_ref[rows, :] = jnp.dot(h, w_ref[...], preferred_element_type=F32).astype(BF16)

    @pl.when(jnp.logical_not(first))
    def _():
        o_ref[...] = jnp.dot(h_ref[...], w_ref[...], preferred_element_type=F32).astype(BF16)


def _inproj(x2, g, sc, sh, w_bf, seq):
    t, d = x2.shape
    n = w_bf.shape[1]
    tm = 1024
    tn = n // 4 if (n // 4) % (2 * LANES) == 0 else 1024
    per_batch = seq // tm
    return pl.pallas_call(
        functools.partial(_inproj_kernel, chunk=256),
        grid=(t // tm, n // tn),
        in_specs=[
            pl.BlockSpec((tm, d), lambda i, j: (i, 0)),
            pl.BlockSpec((1, d), lambda i, j: (0, 0)),
            pl.BlockSpec((1, 1, d), lambda i, j: (i // per_batch, 0, 0)),
            pl.BlockSpec((1, 1, d), lambda i, j: (i // per_batch, 0, 0)),
            pl.BlockSpec((d, tn), lambda i, j: (0, j)),
        ],
        out_specs=pl.BlockSpec((tm, tn), lambda i, j: (i, j)),
        out_shape=jax.ShapeDtypeStruct((t, n), BF16),
        scratch_shapes=[pltpu.VMEM((tm, d), BF16)],
        compiler_params=_cparams(("arbitrary", "arbitrary")),
        name="inproj",
    )(x2, g.reshape(1, d), sc, sh, w_bf)


def _sb_kernel(q_ref, k_ref, v_ref, o_ref, *, group):
    seq = q_ref.shape[0]
    blk = LANES
    scale = HEAD_DIM ** -0.5 * LOG2_E
    row = lax.broadcasted_iota(I32, (blk, blk), 0)
    col = lax.broadcasted_iota(I32, (blk, blk), 1)
    strict = col < row
    tr = lax.broadcasted_iota(I32, (2 * blk, 2 * blk), 0) % blk
    tc = lax.broadcasted_iota(I32, (2 * blk, 2 * blk), 1)
    tri = jnp.where((tc >= blk) | (tr > tc), 1.0, 0.0).astype(BF16)

    def key_rows(j):
        return pl.ds(pl.multiple_of(j * blk, blk), blk)

    def logits(qb, kb):
        z = lax.dot_general(qb, kb, (((1,), (1,)), ((), ())), preferred_element_type=F32) * scale
        sp = jnp.maximum(z, 0.0) + jnp.log2(1.0 + jnp.exp2(-jnp.abs(z)))
        return -sp, z - sp

    def suffix_sums(log_surv):
        hi = log_surv.astype(BF16)
        lo = (log_surv - hi.astype(F32)).astype(BF16)
        r = jnp.dot(jnp.concatenate([hi, lo], axis=1), tri, preferred_element_type=F32)
        return r[:, :blk], r[:, blk:]

    def tile(qb, j, g, carry, acc):
        log_surv, log_beta = logits(qb, k_ref[key_rows(j), g * blk:(g + 1) * blk])
        after, total = suffix_sums(log_surv)
        a = jnp.exp2(log_beta + after + carry)
        acc = acc + jnp.dot(a.astype(BF16), v_ref[key_rows(j), g * blk:(g + 1) * blk],
                            preferred_element_type=F32)
        return carry + total, acc

    def group_start(qbs, i):
        n_near = 3
        js = [i, jnp.maximum(i - 1, 0), jnp.maximum(i - 2, 0)]
        ivec = jnp.full((blk, blk), i, I32)
        masks = [strict, ivec >= 1, ivec >= 2]
        log_betas, parts = [], []
        for g in range(group):
            cols = slice(g * blk, (g + 1) * blk)
            kcat = jnp.concatenate([k_ref[key_rows(j), cols] for j in js], axis=0)
            log_surv, log_beta = logits(qbs[g], kcat)
            log_betas.append(log_beta)
            for m in range(n_near):
                parts.append(jnp.where(masks[m], log_surv[:, m * blk:(m + 1) * blk], 0.0))
        after, total = suffix_sums(jnp.concatenate(parts, axis=0))
        out = []
        for g in range(group):
            cols = slice(g * blk, (g + 1) * blk)
            carry = jnp.zeros((blk, blk), F32)
            weights = []
            for m in range(n_near):
                rows = slice((g * n_near + m) * blk, (g * n_near + m + 1) * blk)
                a = jnp.exp2(log_betas[g][:, m * blk:(m + 1) * blk] + after[rows] + carry)
                weights.append(jnp.where(masks[m], a, 0.0).astype(BF16))
                carry = carry + total[rows]
            vcat = jnp.concatenate([v_ref[key_rows(j), cols] for j in js], axis=0)
            acc = jnp.dot(jnp.concatenate(weights, axis=1), vcat, preferred_element_type=F32)
            out.append((carry, acc))
        return out

    q_blocks = q_ref.shape[0] // blk

    def q_block(local, c):
        i = pl.program_id(1) * q_blocks + local
        rows = pl.ds(pl.multiple_of(local * blk, blk), blk)
        qbs = [q_ref[rows, g * blk:(g + 1) * blk] for g in range(group)]
        first = group_start(qbs, i)
        carries = tuple(f[0] for f in first)
        accs = tuple(f[1] for f in first)

        def cond(st):
            j, crs, _ = st
            top = functools.reduce(jnp.maximum, crs)
            return jnp.logical_and(j >= 0, jnp.max(top) > SB_SKIP_BOUND)

        def body(st):
            j, crs, acs = st
            nxt = [tile(qbs[g], j, g, crs[g], acs[g]) for g in range(group)]
            return j - 1, tuple(n[0] for n in nxt), tuple(n[1] for n in nxt)

        _, _, accs = lax.while_loop(cond, body, (i - 3, carries, accs))
        for g in range(group):
            o_ref[rows, g * blk:(g + 1) * blk] = accs[g].astype(BF16)
        return c

    lax.fori_loop(0, q_blocks, q_block, 0)


SB_ROW_TILE = 512


def _sb_attention(u3, n_heads):
    b, s, _ = u3.shape
    w = n_heads * HEAD_DIM
    tq = min(SB_ROW_TILE, s)
    return pl.pallas_call(
        functools.partial(_sb_kernel, group=n_heads),
        grid=(b, s // tq),
        in_specs=[pl.BlockSpec((None, tq, w), lambda bi, qi: (bi, qi, 0)),
                  pl.BlockSpec((None, s, w), lambda bi, qi: (bi, 0, 1)),
                  pl.BlockSpec((None, s, w), lambda bi, qi: (bi, 0, 2))],
        out_specs=pl.BlockSpec((None, tq, w), lambda bi, qi: (bi, qi, 0)),
        out_shape=jax.ShapeDtypeStruct((b, s, w), BF16),
        compiler_params=_cparams(("arbitrary", "arbitrary")),
        name="sb_attn",
    )(u3, u3, u3)


def _ret_kernel(q_ref, k_ref, v_ref, g_ref, cos_ref, sin_ref, gn_ref, o_ref, *, chunk, unroll):
    seq = q_ref.shape[0]
    dh = HEAD_DIM
    head = pl.program_id(1)

    def log_gamma(shape):
        hv = jnp.full(shape, head, I32).astype(F32)
        return jnp.log(1.0 - jnp.exp2(-5.0 - hv))

    diff = (lax.broadcasted_iota(I32, (chunk, chunk), 0)
            - lax.broadcasted_iota(I32, (chunk, chunk), 1)).astype(F32)
    lower = diff >= 0.0
    dmat = jnp.where(lower, jnp.exp(jnp.where(lower, diff, 0.0) * log_gamma((chunk, chunk))), 0.0)
    pos = lax.broadcasted_iota(I32, (chunk, dh), 0).astype(F32)
    k_decay = jnp.exp((chunk - 1.0 - pos) * log_gamma((chunk, dh)))
    q_decay = jnp.exp((pos + 1.0) * log_gamma((chunk, dh)))
    chunk_decay = jnp.exp(chunk * log_gamma((dh, dh)))
    gn = gn_ref[...]
    scale = dh ** -0.5

    def rope(t, cs, sn):
        return t * cs + pltpu.roll(t, dh // 2, 1) * sn

    def body(n, state):
        rows = pl.ds(pl.multiple_of(n * chunk, chunk), chunk)
        cs = cos_ref[rows, :]
        sn = sin_ref[rows, :]
        qc = rope(q_ref[rows, :].astype(F32), cs, sn)
        kc = rope(k_ref[rows, :].astype(F32), cs, sn) * scale
        vc = v_ref[rows, :]
        scores = lax.dot_general(qc.astype(BF16), kc.astype(BF16), (((1,), (1,)), ((), ())),
                                 preferred_element_type=F32) * dmat
        intra = jnp.dot(scores.astype(BF16), vc, preferred_element_type=F32)
        cross = jnp.dot((qc * q_decay).astype(BF16), state.astype(BF16), preferred_element_type=F32)
        out = intra + cross
        kd_t = jnp.transpose(kc * k_decay).astype(BF16)
        state = state * chunk_decay + jnp.dot(kd_t, vc, preferred_element_type=F32)
        ms = jnp.mean(out * out, axis=-1, keepdims=True)
        y = out * lax.rsqrt(ms + EPS) * gn * _silu(g_ref[rows, :].astype(F32))
        o_ref[rows, :] = y.astype(BF16)
        return state

    lax.fori_loop(0, seq // chunk, body, jnp.zeros((dh, dh), F32), unroll=unroll)


RET_CHUNK = 512
RET_UNROLL = 2


def _retention(u3, cos2, sin2, gn, n_heads, col0):
    b, s, _ = u3.shape
    spec = lambda off: pl.BlockSpec((None, s, HEAD_DIM), lambda bi, h: (bi, 0, col0 + off + h))
    return pl.pallas_call(
        functools.partial(_ret_kernel, chunk=RET_CHUNK, unroll=RET_UNROLL),
        grid=(b, n_heads),
        in_specs=[spec(0), spec(n_heads), spec(2 * n_heads), spec(3 * n_heads),
                  pl.BlockSpec((s, HEAD_DIM), lambda bi, h: (0, 0)),
                  pl.BlockSpec((s, HEAD_DIM), lambda bi, h: (0, 0)),
                  pl.BlockSpec((1, HEAD_DIM), lambda bi, h: (0, h))],
        out_specs=pl.BlockSpec((None, s, HEAD_DIM), lambda bi, h: (bi, 0, h)),
        out_shape=jax.ShapeDtypeStruct((b, s, n_heads * HEAD_DIM), BF16),
        compiler_params=_cparams(("arbitrary", "arbitrary")),
        name="retention",
    )(u3, u3, u3, u3, cos2, sin2, gn.reshape(1, -1))


def _outproj_kernel(sb_ref, rt_ref, wa_ref, wb_ref, x_ref, g1_ref, n2_ref, sc_ref, sh_ref, wr_ref,
                    x1_ref, h2_ref, hp_ref, lt_ref, *, chunk):
    d = x_ref.shape[1]
    half = d // 2
    g1 = g1_ref[0]
    n2 = n2_ref[...]
    sc = sc_ref[0]
    sh = sh_ref[0]
    for r in range(x_ref.shape[0] // chunk):
        rows = pl.ds(r * chunk, chunk)
        o = jnp.dot(sb_ref[rows, :], wa_ref[...], preferred_element_type=F32)
        o = o + jnp.dot(rt_ref[rows, :], wb_ref[...], preferred_element_type=F32)
        x1 = x_ref[rows, :] + g1 * o
        x1_ref[rows, :] = x1
        h2 = _norm_mod(x1, n2, sc, sh).astype(BF16)
        h2_ref[rows, :] = h2
        lt_ref[:, rows] = lax.dot_general(wr_ref[...], h2, (((1,), (1,)), ((), ())),
                                          preferred_element_type=F32)
        words = _pack_pair(h2[:, :half].astype(F32), h2[:, half:].astype(F32))
        for j in range(ROW_WORDS):
            hp_ref[pl.ds(r * chunk * ROW_WORDS + j, chunk, stride=ROW_WORDS), :] = (
                words[:, j * LANES:(j + 1) * LANES])


def _outproj(sb2, rt2, wa, wb, x2, g1, n2, sc2, sh2, wr_t, seq):
    t, d = x2.shape
    tm = 512
    per_batch = seq // tm
    mod = pl.BlockSpec((1, 1, d), lambda i: (i // per_batch, 0, 0))
    return pl.pallas_call(
        functools.partial(_outproj_kernel, chunk=512),
        grid=(t // tm,),
        in_specs=[
            pl.BlockSpec((tm, sb2.shape[1]), lambda i: (i, 0)),
            pl.BlockSpec((tm, rt2.shape[1]), lambda i: (i, 0)),
            pl.BlockSpec(wa.shape, lambda i: (0, 0)),
            pl.BlockSpec(wb.shape, lambda i: (0, 0)),
            pl.BlockSpec((tm, d), lambda i: (i, 0)),
            mod,
            pl.BlockSpec((1, d), lambda i: (0, 0)),
            mod, mod,
            pl.BlockSpec(wr_t.shape, lambda i: (0, 0)),
        ],
        out_specs=[
            pl.BlockSpec((tm, d), lambda i: (i, 0)),
            pl.BlockSpec((tm, d), lambda i: (i, 0)),
            pl.BlockSpec((tm * ROW_WORDS, LANES), lambda i: (i, 0)),
            pl.BlockSpec((N_EXPERTS, tm), lambda i: (0, i)),
        ],
        out_shape=[
            jax.ShapeDtypeStruct((t, d), F32),
            jax.ShapeDtypeStruct((t, d), BF16),
            jax.ShapeDtypeStruct((t * ROW_WORDS, LANES), U32),
            jax.ShapeDtypeStruct((N_EXPERTS, t), F32),
        ],
        compiler_params=_cparams(("arbitrary",)),
        name="outproj",
    )(sb2, rt2, wa, wb, x2, g1, n2.reshape(1, d), sc2, sh2, wr_t)


def _beats(row, allv, row_idx, idx):
    return (row > allv) | ((row == allv) & (row_idx < idx))


def _route_kernel(lt_ref, bias_ref, h2_ref, sg_ref, su_ref, sd_ref,
                  w_ref, rank_ref, cnt_ref, shared_ref, carry_ref):
    tr = lt_ref.shape[1]
    gsz = N_EXPERTS // N_GROUPS
    neg_inf = jnp.float32(-jnp.inf)

    @pl.when(pl.program_id(0) == 0)
    def _():
        carry_ref[...] = jnp.zeros_like(carry_ref)

    h2 = h2_ref[...]
    sa = jnp.dot(h2, sg_ref[...], preferred_element_type=F32)
    su = jnp.dot(h2, su_ref[...], preferred_element_type=F32)
    shared_ref[...] = jnp.dot((_silu(sa) * su).astype(BF16), sd_ref[...],
                              preferred_element_type=F32).astype(BF16)

    scores = jax.nn.sigmoid(lt_ref[...])
    choice = scores + bias_ref[...]
    sub = lax.broadcasted_iota(I32, (gsz, tr), 0)
    gs_rows = []
    for g in range(N_GROUPS):
        cg = choice[g * gsz:(g + 1) * gsz, :]
        m1 = jnp.max(cg, axis=0, keepdims=True)
        first = jnp.min(jnp.where(cg == m1, sub, gsz), axis=0, keepdims=True)
        m2 = jnp.max(jnp.where(sub == first, neg_inf, cg), axis=0, keepdims=True)
        gs_rows.append(m1 + m2)
    gs = jnp.concatenate(gs_rows, axis=0)
    gi = lax.broadcasted_iota(I32, (N_GROUPS, tr), 0)
    grank = jnp.zeros((N_GROUPS, tr), I32)
    for g in range(N_GROUPS):
        grank = grank + _beats(gs[g:g + 1, :], gs, g, gi).astype(I32)
    gmask = grank < TOPK_GROUP
    emask = jnp.concatenate(
        [jnp.broadcast_to(gmask[g:g + 1, :], (gsz, tr)) for g in range(N_GROUPS)], axis=0)
    masked = jnp.where(emask, choice, neg_inf)
    ei = lax.broadcasted_iota(I32, (N_EXPERTS, tr), 0)
    erank = jnp.zeros((N_EXPERTS, tr), I32)
    for e in range(N_EXPERTS):
        erank = erank + _beats(masked[e:e + 1, :], masked, e, ei).astype(I32)
    sel = erank < TOP_K
    ssel = jnp.where(sel, scores, 0.0)
    denom = jnp.sum(ssel, axis=0, keepdims=True)
    w_ref[...] = ssel / denom * ROUTED_SCALE

    self = sel.astype(F32)
    upper = (lax.broadcasted_iota(I32, (tr, tr), 0) < lax.broadcasted_iota(I32, (tr, tr), 1))
    prefix = jnp.dot(self.astype(BF16), upper.astype(BF16), preferred_element_type=F32)
    carry = carry_ref[...]
    rank = prefix + carry[:, 0:1]
    rank_ref[...] = jnp.where(sel, rank, -1.0).astype(I32)
    carry = carry + jnp.sum(self, axis=1, keepdims=True)
    carry_ref[...] = carry
    cnt_ref[...] = carry


def _route(lt, bias, h2, sg, su, sd):
    e, t = lt.shape
    d = h2.shape[1]
    tr = min(512, t)
    full = lambda a: pl.BlockSpec(a.shape, lambda i: (0, 0))
    return pl.pallas_call(
        _route_kernel,
        grid=(t // tr,),
        in_specs=[pl.BlockSpec((e, tr), lambda i: (0, i)),
                  pl.BlockSpec((e, 1), lambda i: (0, 0)),
                  pl.BlockSpec((tr, d), lambda i: (i, 0)),
                  full(sg), full(su), full(sd)],
        out_specs=[pl.BlockSpec((e, tr), lambda i: (0, i)),
                   pl.BlockSpec((e, tr), lambda i: (0, i)),
                   pl.BlockSpec((e, LANES), lambda i: (0, 0)),
                   pl.BlockSpec((tr, d), lambda i: (i, 0))],
        out_shape=[jax.ShapeDtypeStruct((e, t), F32),
                   jax.ShapeDtypeStruct((e, t), I32),
                   jax.ShapeDtypeStruct((e, LANES), F32),
                   jax.ShapeDtypeStruct((t, d), BF16)],
        scratch_shapes=[pltpu.VMEM((e, LANES), F32)],
        compiler_params=_cparams(("arbitrary",)),
        name="route",
    )(lt, bias.reshape(e, 1), h2, sg, su, sd)


def _plan_kernel(w_ref, rank_ref, cnt_ref, pos_ref, wc_ref, seg_ref):
    cnt = cnt_ref[...]
    nblk = jnp.floor((cnt + (MOE_BLOCK - 1.0)) * (1.0 / MOE_BLOCK))
    lower = (lax.broadcasted_iota(I32, (N_EXPERTS, N_EXPERTS), 1)
             < lax.broadcasted_iota(I32, (N_EXPERTS, N_EXPERTS), 0)).astype(BF16)
    bstart = jnp.dot(lower, nblk.astype(BF16), preferred_element_type=F32)
    bend = bstart + nblk

    @pl.when(pl.program_id(0) == 0)
    def _():
        seg_ref[...] = jnp.concatenate(
            [bstart[:, 0:1] * MOE_BLOCK, cnt[:, 0:1], nblk[:, 0:1] * MOE_BLOCK, bend[:, 0:1]],
            axis=1).astype(I32)

    rank = rank_ref[...]
    sel = rank >= 0
    pos = bstart[:, 0:1] * MOE_BLOCK + rank.astype(F32)
    slot = jnp.dot(lower, sel.astype(BF16), preferred_element_type=F32)
    w = w_ref[...]
    pos_rows, w_rows = [], []
    for k in range(TOP_K):
        m = sel & (slot == float(k))
        pos_rows.append(jnp.sum(jnp.where(m, pos, 0.0), axis=0, keepdims=True))
        w_rows.append(jnp.sum(jnp.where(m, w, 0.0), axis=0, keepdims=True))
    pos_ref[...] = jnp.concatenate(pos_rows, axis=0).astype(I32)
    wc_ref[...] = jnp.concatenate(w_rows, axis=0)


def _plan(w, rank, cnt):
    e, t = w.shape
    tr = min(1024, t)
    return pl.pallas_call(
        _plan_kernel,
        grid=(t // tr,),
        in_specs=[pl.BlockSpec((e, tr), lambda i: (0, i)),
                  pl.BlockSpec((e, tr), lambda i: (0, i)),
                  pl.BlockSpec((e, LANES), lambda i: (0, 0))],
        out_specs=[pl.BlockSpec((TOP_K, tr), lambda i: (0, i)),
                   pl.BlockSpec((TOP_K, tr), lambda i: (0, i)),
                   pl.BlockSpec((e, 4), lambda i: (0, 0))],
        out_shape=[jax.ShapeDtypeStruct((TOP_K, t), I32),
                   jax.ShapeDtypeStruct((TOP_K, t), F32),
                   jax.ShapeDtypeStruct((e, 4), I32)],
        compiler_params=_cparams(("arbitrary",)),
        name="plan",
    )(w, rank, cnt)


def _row_copy(src, src_row, dst, dst_row, sem):
    return pltpu.make_async_copy(src.at[pl.ds(src_row * ROW_WORDS, ROW_WORDS), :],
                                 dst.at[pl.ds(dst_row * ROW_WORDS, ROW_WORDS), :], sem)


def _dispatch_kernel(pos_ref, seg_ref, hp_ref, xs_ref, zero_ref, sem, zsem):
    tq = pos_ref.shape[0] // TOP_K
    blk_words = MOE_BLOCK * ROW_WORDS
    n_blocks = xs_ref.shape[0] // blk_words

    @pl.when(pl.program_id(0) == 0)
    def _():
        zero_ref[...] = jnp.zeros_like(zero_ref)
        zrow = zero_ref.at[pl.ds(0, ROW_WORDS), :]

        def per_expert(e, n):
            start = seg_ref[e, 0] + seg_ref[e, 1]
            npad = seg_ref[e, 2] - seg_ref[e, 1]

            def fill(r, c):
                pltpu.make_async_copy(
                    zrow, xs_ref.at[pl.ds((start + r) * ROW_WORDS, ROW_WORDS), :], zsem).start()
                return c

            lax.fori_loop(0, npad, fill, 0)
            return n + npad

        n_pad_rows = lax.fori_loop(0, N_EXPERTS, per_expert, 0)

        def drain_row(r, c):
            pltpu.make_async_copy(zrow, xs_ref.at[pl.ds(0, ROW_WORDS), :], zsem).wait()
            return c

        lax.fori_loop(0, n_pad_rows, drain_row, 0)
        n_used = seg_ref[N_EXPERTS - 1, 3]

        def fill_block(b, c):
            pltpu.make_async_copy(
                zero_ref, xs_ref.at[pl.ds(b * blk_words, blk_words), :], zsem).start()
            return c

        lax.fori_loop(n_used, n_blocks, fill_block, 0)

        def drain_block(b, c):
            pltpu.make_async_copy(zero_ref, xs_ref.at[pl.ds(0, blk_words), :], zsem).wait()
            return c

        lax.fori_loop(n_used, n_blocks, drain_block, 0)

    def issue(t, c):
        for k in range(TOP_K):
            _row_copy(hp_ref, t, xs_ref, pos_ref[t * TOP_K + k], sem).start(priority=k % 2)
        return c

    lax.fori_loop(0, tq, issue, 0)
    for k in range(TOP_K):
        pltpu.make_async_copy(hp_ref, xs_ref.at[pl.ds(0, tq * ROW_WORDS), :], sem).wait()


def _dispatch(pos_tm, seg, hp, n_rows):
    t = pos_tm.shape[0] // TOP_K
    tq = min(1024, t)
    return pl.pallas_call(
        _dispatch_kernel,
        grid=(t // tq,),
        in_specs=[pl.BlockSpec((tq * TOP_K,), lambda i: (i,), memory_space=pltpu.SMEM),
                  pl.BlockSpec(seg.shape, lambda i: (0, 0), memory_space=pltpu.SMEM),
                  pl.BlockSpec((tq * ROW_WORDS, LANES), lambda i: (i, 0))],
        out_specs=pl.BlockSpec(memory_space=pl.ANY),
        out_shape=jax.ShapeDtypeStruct((n_rows * ROW_WORDS, LANES), U32),
        scratch_shapes=[pltpu.VMEM((MOE_BLOCK * ROW_WORDS, LANES), U32),
                        pltpu.SemaphoreType.DMA(()), pltpu.SemaphoreType.DMA(())],
        compiler_params=_cparams(("arbitrary",)),
        name="dispatch",
    )(pos_tm, seg, hp)


def _expert_kernel(seg_ref, xs_ref, wg_ref, wu_ref, wd_ref, ys_ref,
                   wg32, wu32, wd32, wgb, wub, wdb, xbuf, ybuf, wsem, xsem, ysem):
    e = pl.program_id(0)
    n_exp = pl.num_programs(0)
    last_step = e == n_exp - 1

    def weight_copies(ex):
        slot = ex & 1
        return [pltpu.make_async_copy(src.at[ex], dst.at[slot], wsem.at[slot])
                for src, dst in ((wg_ref, wg32), (wu_ref, wu32), (wd_ref, wd32))]

    @pl.when(e == 0)
    def _():
        for cp in weight_copies(0):
            cp.start()

    for cp in weight_copies(e):
        cp.wait()
    rows = MOE_BLOCK
    blk_words = MOE_BLOCK * ROW_WORDS
    n_blocks = ys_ref.shape[0] // blk_words
    first = lax.div(seg_ref[e * 4], MOE_BLOCK)
    end = seg_ref[e * 4 + 3]
    n_used = seg_ref[(pl.num_programs(0) - 1) * 4 + 3]

    x_slots = xbuf.shape[0]
    x_ahead = x_slots - 2

    def x_copy(g):
        slot = g & (x_slots - 1)
        return pltpu.make_async_copy(xs_ref.at[pl.ds(g * blk_words, blk_words), :],
                                     xbuf.at[slot], xsem.at[slot])

    def y_copy(g):
        slot = g & 1
        return pltpu.make_async_copy(ybuf.at[slot],
                                     ys_ref.at[pl.ds(g * blk_words, blk_words), :], ysem.at[slot])

    for g0 in range(x_ahead):
        @pl.when(jnp.logical_and(e == 0, n_used > g0))
        def _():
            x_copy(g0).start()

    @pl.when(end > first)
    def _():
        wslot = e & 1
        wgb[...] = wg32[wslot].astype(BF16)
        wub[...] = wu32[wslot].astype(BF16)
        wdb[...] = wd32[wslot].astype(BF16)

    @pl.when(e + 1 < n_exp)
    def _():
        for cp in weight_copies(e + 1):
            cp.start(priority=1)

    @pl.when(end > first)
    def _():
        def block(g, c):
            slot = g & 1
            x_copy(g).wait()

            @pl.when(g + x_ahead < n_used)
            def _():
                x_copy(g + x_ahead).start()

            @pl.when(g >= 2)
            def _():
                y_copy(g - 2).wait()

            xin = xbuf.at[g & (x_slots - 1)]
            his, los = [], []
            for j in range(ROW_WORDS):
                hi, lo = _unpack_pair(xin[pl.ds(j, rows, stride=ROW_WORDS), :])
                his.append(hi.astype(BF16))
                los.append(lo.astype(BF16))
            xb = jnp.concatenate(his + los, axis=1)
            a = jnp.dot(xb, wgb[...], preferred_element_type=F32)
            u = jnp.dot(xb, wub[...], preferred_element_type=F32)
            hid = (_silu(a) * u).astype(BF16)
            out = jnp.dot(hid, wdb[...], preferred_element_type=F32).astype(BF16).astype(F32)
            half = out.shape[1] // 2
            words = _pack_pair(out[:, :half], out[:, half:])
            yout = ybuf.at[slot]
            for j in range(ROW_WORDS):
                yout[pl.ds(j, rows, stride=ROW_WORDS), :] = words[:, j * LANES:(j + 1) * LANES]
            y_copy(g).start()
            return c

        lax.fori_loop(first, end, block, 0)

    @pl.when(last_step)
    def _():
        @pl.when(n_used >= 2)
        def _():
            y_copy(n_used - 2).wait()

        @pl.when(n_used >= 1)
        def _():
            y_copy(n_used - 1).wait()

        ybuf[0] = jnp.zeros(ybuf.shape[1:], ybuf.dtype)

        def fill(g, c):
            pltpu.make_async_copy(ybuf.at[0], ys_ref.at[pl.ds(g * blk_words, blk_words), :],
                                  ysem.at[0]).start()
            return c

        lax.fori_loop(n_used, n_blocks, fill, 0)

        def drain(g, c):
            pltpu.make_async_copy(ybuf.at[0], ys_ref.at[pl.ds(0, blk_words), :], ysem.at[0]).wait()
            return c

        lax.fori_loop(n_used, n_blocks, drain, 0)


def _experts(seg_flat, xs, w_gate, w_up, w_down):
    n_exp, d, f = w_gate.shape
    blk_words = MOE_BLOCK * ROW_WORDS
    grid_spec = pltpu.PrefetchScalarGridSpec(
        num_scalar_prefetch=1,
        grid=(n_exp,),
        in_specs=[
            pl.BlockSpec(memory_space=pl.ANY),
            pl.BlockSpec(memory_space=pl.ANY),
            pl.BlockSpec(memory_space=pl.ANY),
            pl.BlockSpec(memory_space=pl.ANY),
        ],
        out_specs=pl.BlockSpec(memory_space=pl.ANY),
        scratch_shapes=[pltpu.VMEM((2, d, f), F32), pltpu.VMEM((2, d, f), F32), pltpu.VMEM((2, f, d), F32),
                        pltpu.VMEM((d, f), BF16), pltpu.VMEM((d, f), BF16), pltpu.VMEM((f, d), BF16),
                        pltpu.VMEM((4, blk_words, LANES), U32), pltpu.VMEM((2, blk_words, LANES), U32),
                        pltpu.SemaphoreType.DMA((2,)), pltpu.SemaphoreType.DMA((4,)),
                        pltpu.SemaphoreType.DMA((2,))],
    )
    return pl.pallas_call(
        _expert_kernel,
        grid_spec=grid_spec,
        out_shape=jax.ShapeDtypeStruct(xs.shape, U32),
        compiler_params=_cparams(("arbitrary",)),
        name="experts",
    )(seg_flat, xs, w_gate, w_up, w_down)


def _combine_kernel(pos_ref, nxt_ref, ys_ref, wc_ref, shared_ref, x1_ref,
                    g2_ref, nf_ref, scf_ref, shf_ref, o_ref, buf, sem):
    i = pl.program_id(0)
    tq = wc_ref.shape[1]
    slot = i & 1

    def gather(p_ref, to_slot):
        def issue(t, c):
            for k in range(TOP_K):
                pltpu.make_async_copy(
                    ys_ref.at[pl.ds(p_ref[t * TOP_K + k] * ROW_WORDS, ROW_WORDS), :],
                    buf.at[to_slot, k, pl.ds(t * ROW_WORDS, ROW_WORDS), :],
                    sem.at[to_slot]).start(priority=k % 2)
            return c

        lax.fori_loop(0, tq, issue, 0)

    @pl.when(i == 0)
    def _():
        gather(pos_ref, 0)

    @pl.when(i + 1 < pl.num_programs(0))
    def _():
        gather(nxt_ref, 1 - slot)

    for k in range(TOP_K):
        pltpu.make_async_copy(ys_ref.at[pl.ds(0, tq * ROW_WORDS), :], buf.at[slot, k],
                              sem.at[slot]).wait()

    wt = jnp.transpose(wc_ref[...])
    his, los = [], []
    for j in range(ROW_WORDS):
        acc_hi = jnp.zeros((tq, LANES), F32)
        acc_lo = jnp.zeros((tq, LANES), F32)
        for k in range(TOP_K):
            hi, lo = _unpack_pair(buf.at[slot, k][pl.ds(j, tq, stride=ROW_WORDS), :])
            wk = wt[:, k:k + 1]
            acc_hi = acc_hi + wk * hi
            acc_lo = acc_lo + wk * lo
        his.append(acc_hi)
        los.append(acc_lo)
    y = jnp.concatenate(his + los, axis=1) + shared_ref[...].astype(F32)
    x2 = x1_ref[...] + g2_ref[0] * y
    o_ref[...] = _norm_mod(x2, nf_ref[...], scf_ref[0], shf_ref[0])


def _combine(pos_tm, ys, w_c, shared, x1, g2, nf, scf, shf, seq):
    t, d = x1.shape
    tq = min(256, t)
    n_tiles = t // tq
    per_batch = seq // tq
    mod = pl.BlockSpec((1, 1, d), lambda i: (i // per_batch, 0, 0))
    return pl.pallas_call(
        _combine_kernel,
        grid=(n_tiles,),
        in_specs=[
            pl.BlockSpec((tq * TOP_K,), lambda i: (i,), memory_space=pltpu.SMEM),
            pl.BlockSpec((tq * TOP_K,), lambda i: (jnp.minimum(i + 1, n_tiles - 1),),
                         memory_space=pltpu.SMEM),
            pl.BlockSpec(memory_space=pl.ANY),
            pl.BlockSpec((TOP_K, tq), lambda i: (0, i)),
            pl.BlockSpec((tq, d), lambda i: (i, 0)),
            pl.BlockSpec((tq, d), lambda i: (i, 0)),
            mod,
            pl.BlockSpec((1, d), lambda i: (0, 0)),
            mod, mod,
        ],
        out_specs=pl.BlockSpec((tq, d), lambda i: (i, 0)),
        out_shape=jax.ShapeDtypeStruct((t, d), F32),
        scratch_shapes=[pltpu.VMEM((2, TOP_K, tq * ROW_WORDS, LANES), U32),
                        pltpu.SemaphoreType.DMA((2,))],
        compiler_params=_cparams(("arbitrary",)),
        name="combine",
    )(pos_tm, pos_tm, ys, w_c, shared, x1, g2, nf.reshape(1, d), scf, shf)


def _rope_tables(seq):
    pos = jnp.arange(seq, dtype=F32)
    inv_freq = ROPE_BASE ** (-jnp.arange(0, HEAD_DIM, 2, dtype=F32) / HEAD_DIM)
    ang = pos[:, None] * inv_freq[None, :]
    cos, sin = jnp.cos(ang), jnp.sin(ang)
    return jnp.concatenate([cos, cos], axis=1), jnp.concatenate([-sin, sin], axis=1)


def kernel(x, c, w_ada, b_ada, norm1_g, w_in, ret_gn_g, w_out, norm2_g, w_router, router_bias,
           w_gate, w_up, w_down, ws_gate, ws_up, ws_down, w_ada_final, b_ada_final, norm_f_g):
    bsz, seq, d = x.shape
    t = bsz * seq
    depth = w_ada.shape[0]
    n_heads = d // (2 * HEAD_DIM)
    sb_w = n_heads * HEAD_DIM
    cos2, sin2 = _rope_tables(seq)
    c_pad = jnp.pad(c, ((0, 8 - bsz % 8), (0, 0)))
    n_rows = t * TOP_K + N_EXPERTS * MOE_BLOCK
    as_mod = lambda v: v.reshape(bsz, 1, d)

    x2 = x.reshape(t, d)
    for l in range(depth):
        mod = _ada(c_pad, w_ada[l], b_ada[l])[:bsz]
        sh1, sc1, g1, sh2, sc2, g2 = [as_mod(m) for m in jnp.split(mod, 6, axis=-1)]
        u = _inproj(x2, norm1_g[l], sc1, sh1, w_in[l].astype(BF16), seq)
        u3 = u.reshape(bsz, seq, -1)
        sb = _sb_attention(u3, n_heads)
        rt = _retention(u3, cos2, sin2, ret_gn_g[l], n_heads, 3 * n_heads)
        wo = w_out[l].astype(BF16)
        x1, h2, hp, lt = _outproj(sb.reshape(t, sb_w), rt.reshape(t, -1), wo[:sb_w], wo[sb_w:], x2,
                                  g1, norm2_g[l], sc2, sh2, jnp.transpose(w_router[l]).astype(BF16), seq)
        w_e, rank_e, cnt, shared = _route(lt, router_bias[l], h2, ws_gate[l].astype(BF16),
                                          ws_up[l].astype(BF16), ws_down[l].astype(BF16))
        pos_c, w_c, seg = _plan(w_e, rank_e, cnt)
        pos_tm = jnp.transpose(pos_c).reshape(-1)
        xs = _dispatch(pos_tm, seg, hp, n_rows)
        ys = _experts(seg.reshape(-1), xs, w_gate[l], w_up[l], w_down[l])
        if l + 1 < depth:
            raise NotImplementedError("only the final layer fuses the output norm")
        modf = _ada(c_pad, w_ada_final, b_ada_final)[:bsz]
        shf, scf = [as_mod(m) for m in jnp.split(modf, 2, axis=-1)]
        x2 = _combine(pos_tm, ys, w_c, shared, x1, g2, norm_f_g, scf, shf, seq)
    return x2.reshape(bsz, seq, d)
```

```python
import functools

import jax
import jax.numpy as jnp
from jax import lax
from jax.experimental import pallas as pl
from jax.experimental.pallas import tpu as pltpu

F32 = jnp.float32
BF16 = jnp.bfloat16
I32 = jnp.int32
U32 = jnp.uint32

HEAD_DIM = 128
N_EXPERTS = 64
TOP_K = 8
N_GROUPS = 8
TOPK_GROUP = 4
ROUTED_SCALE = 2.5
EPS = 1e-6
ROPE_BASE = 10000.0

LANES = 128
ROW_WORDS = 8
MOE_BLOCK = 256
V7X_VMEM_LIMIT = 56 * 1024 * 1024
LOG2_E = 1.4426950408889634
SB_SKIP_BOUND = -115.0 * LOG2_E


def _cparams(sem, vmem=V7X_VMEM_LIMIT):
    return pltpu.CompilerParams(dimension_semantics=sem, vmem_limit_bytes=vmem)


def _silu(v):
    return v * jax.nn.sigmoid(v)


def _pack_pair(hi_f32, lo_f32):
    hw = pltpu.bitcast(hi_f32, U32) & jnp.uint32(0xFFFF0000)
    lw = pltpu.bitcast(lo_f32, U32) >> 16
    return hw | lw


def _unpack_pair(w):
    hi = pltpu.bitcast(w & jnp.uint32(0xFFFF0000), F32)
    lo = pltpu.bitcast(w << 16, F32)
    return hi, lo


def _ada_kernel(c_ref, w_ref, b_ref, o_ref):
    cs = _silu(c_ref[...]).astype(BF16)
    o_ref[...] = jnp.dot(cs, w_ref[...].astype(BF16), preferred_element_type=F32) + b_ref[...]


def _ada(c_pad, w, b):
    rows, d = c_pad.shape
    n = w.shape[1]
    tn = 1024
    return pl.pallas_call(
        _ada_kernel,
        grid=(n // tn,),
        in_specs=[
            pl.BlockSpec((rows, d), lambda j: (0, 0)),
            pl.BlockSpec((d, tn), lambda j: (0, j)),
            pl.BlockSpec((1, tn), lambda j: (0, j)),
        ],
        out_specs=pl.BlockSpec((rows, tn), lambda j: (0, j)),
        out_shape=jax.ShapeDtypeStruct((rows, n), F32),
        compiler_params=_cparams(("arbitrary",)),
        name="ada",
    )(c_pad, w, b.reshape(1, n))


def _norm_mod(x, g, sc, sh):
    ms = jnp.mean(x * x, axis=-1, keepdims=True)
    return (x * lax.rsqrt(ms + EPS) * g) * (1.0 + sc) + sh


def _inproj_kernel(x_ref, g_ref, sc_ref, sh_ref, w_ref, o_ref, h_ref, *, chunk):
    first = pl.program_id(1) == 0

    @pl.when(first)
    def _():
        g = g_ref[...]
        sc = sc_ref[0]
        sh = sh_ref[0]
        for r in range(x_ref.shape[0] // chunk):
            rows = pl.ds(r * chunk, chunk)
            h = _norm_mod(x_ref[rows, :], g, sc, sh).astype(BF16)
            h_ref[rows, :] = h
            o_ref[rows, :] = jnp.dot(h, w_ref[...], preferred_element_type=F32).astype(BF16)

    @pl.when(jnp.logical_not(first))
    def _():
        o_ref[...] = jnp.dot(h_ref[...], w_ref[...], preferred_element_type=F32).astype(BF16)


def _inproj(x2, g, sc, sh, w_bf, seq):
    t, d = x2.shape
    n = w_bf.shape[1]
    tm = 1024
    tn = n // 4 if (n // 4) % (2 * LANES) == 0 else 1024
    per_batch = seq // tm
    return pl.pallas_call(
        functools.partial(_inproj_kernel, chunk=256),
        grid=(t // tm, n // tn),
        in_specs=[
            pl.BlockSpec((tm, d), lambda i, j: (i, 0)),
            pl.BlockSpec((1, d), lambda i, j: (0, 0)),
            pl.BlockSpec((1, 1, d), lambda i, j: (i // per_batch, 0, 0)),
            pl.BlockSpec((1, 1, d), lambda i, j: (i // per_batch, 0, 0)),
            pl.BlockSpec((d, tn), lambda i, j: (0, j)),
        ],
        out_specs=pl.BlockSpec((tm, tn), lambda i, j: (i, j)),
        out_shape=jax.ShapeDtypeStruct((t, n), BF16),
        scratch_shapes=[pltpu.VMEM((tm, d), BF16)],
        compiler_params=_cparams(("arbitrary", "arbitrary")),
        name="inproj",
    )(x2, g.reshape(1, d), sc, sh, w_bf)


def _sb_kernel(q_ref, k_ref, v_ref, o_ref, *, group):
    seq = q_ref.shape[0]
    blk = LANES
    scale = HEAD_DIM ** -0.5 * LOG2_E
    row = lax.broadcasted_iota(I32, (blk, blk), 0)
    col = lax.broadcasted_iota(I32, (blk, blk), 1)
    strict = col < row
    tr = lax.broadcasted_iota(I32, (2 * blk, 2 * blk), 0) % blk
    tc = lax.broadcasted_iota(I32, (2 * blk, 2 * blk), 1)
    tri = jnp.where((tc >= blk) | (tr > tc), 1.0, 0.0).astype(BF16)

    def key_rows(j):
        return pl.ds(pl.multiple_of(j * blk, blk), blk)

    def logits(qb, kb):
        z = lax.dot_general(qb, kb, (((1,), (1,)), ((), ())), preferred_element_type=F32) * scale
        sp = jnp.maximum(z, 0.0) + jnp.log2(1.0 + jnp.exp2(-jnp.abs(z)))
        return -sp, z - sp

    def suffix_sums(log_surv):
        hi = log_surv.astype(BF16)
        lo = (log_surv - hi.astype(F32)).astype(BF16)
        r = jnp.dot(jnp.concatenate([hi, lo], axis=1), tri, preferred_element_type=F32)
        return r[:, :blk], r[:, blk:]

    def tile(qb, j, g, carry, acc):
        log_surv, log_beta = logits(qb, k_ref[key_rows(j), g * blk:(g + 1) * blk])
        after, total = suffix_sums(log_surv)
        a = jnp.exp2(log_beta + after + carry)
        acc = acc + jnp.dot(a.astype(BF16), v_ref[key_rows(j), g * blk:(g + 1) * blk],
                            preferred_element_type=F32)
        return carry + total, acc

    def group_start(qbs, i):
        n_near = 3
        js = [i, jnp.maximum(i - 1, 0), jnp.maximum(i - 2, 0)]
        ivec = jnp.full((blk, blk), i, I32)
        masks = [strict, ivec >= 1, ivec >= 2]
        log_betas, parts = [], []
        for g in range(group):
            cols = slice(g * blk, (g + 1) * blk)
            kcat = jnp.concatenate([k_ref[key_rows(j), cols] for j in js], axis=0)
            log_surv, log_beta = logits(qbs[g], kcat)
            log_betas.append(log_beta)
            for m in range(n_near):
                parts.append(jnp.where(masks[m], log_surv[:, m * blk:(m + 1) * blk], 0.0))
        after, total = suffix_sums(jnp.concatenate(parts, axis=0))
        out = []
        for g in range(group):
            cols = slice(g * blk, (g + 1) * blk)
            carry = jnp.zeros((blk, blk), F32)
            weights = []
            for m in range(n_near):
                rows = slice((g * n_near + m) * blk, (g * n_near + m + 1) * blk)
                a = jnp.exp2(log_betas[g][:, m * blk:(m + 1) * blk] + after[rows] + carry)
                weights.append(jnp.where(masks[m], a, 0.0).astype(BF16))
                carry = carry + total[rows]
            vcat = jnp.concatenate([v_ref[key_rows(j), cols] for j in js], axis=0)
            acc = jnp.dot(jnp.concatenate(weights, axis=1), vcat, preferred_element_type=F32)
            out.append((carry, acc))
        return out

    q_blocks = q_ref.shape[0] // blk

    def q_block(local, c):
        i = pl.program_id(1) * q_blocks + local
        rows = pl.ds(pl.multiple_of(local * blk, blk), blk)
        qbs = [q_ref[rows, g * blk:(g + 1) * blk] for g in range(group)]
        first = group_start(qbs, i)
        carries = tuple(f[0] for f in first)
        accs = tuple(f[1] for f in first)

        def cond(st):
            j, crs, _ = st
            top = functools.reduce(jnp.maximum, crs)
            return jnp.logical_and(j >= 0, jnp.max(top) > SB_SKIP_BOUND)

        def body(st):
            j, crs, acs = st
            nxt = [tile(qbs[g], j, g, crs[g], acs[g]) for g in range(group)]
            return j - 1, tuple(n[0] for n in nxt), tuple(n[1] for n in nxt)

        _, _, accs = lax.while_loop(cond, body, (i - 3, carries, accs))
        for g in range(group):
            o_ref[rows, g * blk:(g + 1) * blk] = accs[g].astype(BF16)
        return c

    def near_only(local, top):
        i = pl.program_id(1) * q_blocks + local
        rows = pl.ds(pl.multiple_of(local * blk, blk), blk)
        first = group_start([q_ref[rows, g * blk:(g + 1) * blk] for g in range(group)], i)
        for g in range(group):
            o_ref[rows, g * blk:(g + 1) * blk] = first[g][1].astype(BF16)
        left = functools.reduce(jnp.maximum, [f[0] for f in first])
        left = jnp.where(jnp.full((blk, blk), i, I32) >= 3, left, -jnp.inf)
        return jnp.maximum(top, left)

    top = lax.fori_loop(0, q_blocks, near_only, jnp.full((blk, blk), -jnp.inf, F32))

    @pl.when(jnp.max(top) > SB_SKIP_BOUND)
    def _():
        lax.fori_loop(0, q_blocks, q_block, 0)


SB_ROW_TILE = 512


def _sb_attention(u3, n_heads):
    b, s, _ = u3.shape
    w = n_heads * HEAD_DIM
    tq = min(SB_ROW_TILE, s)
    return pl.pallas_call(
        functools.partial(_sb_kernel, group=n_heads),
        grid=(b, s // tq),
        in_specs=[pl.BlockSpec((None, tq, w), lambda bi, qi: (bi, qi, 0)),
                  pl.BlockSpec((None, s, w), lambda bi, qi: (bi, 0, 1)),
                  pl.BlockSpec((None, s, w), lambda bi, qi: (bi, 0, 2))],
        out_specs=pl.BlockSpec((None, tq, w), lambda bi, qi: (bi, qi, 0)),
        out_shape=jax.ShapeDtypeStruct((b, s, w), BF16),
        compiler_params=_cparams(("arbitrary", "arbitrary")),
        name="sb_attn",
    )(u3, u3, u3)


def _ret_kernel(q_ref, k_ref, v_ref, g_ref, cos_ref, sin_ref, gn_ref, o_ref, *, chunk, unroll):
    seq = q_ref.shape[0]
    dh = HEAD_DIM
    head = pl.program_id(1)

    def log_gamma(shape):
        hv = jnp.full(shape, head, I32).astype(F32)
        return jnp.log(1.0 - jnp.exp2(-5.0 - hv))

    diff = (lax.broadcasted_iota(I32, (chunk, chunk), 0)
            - lax.broadcasted_iota(I32, (chunk, chunk), 1)).astype(F32)
    lower = diff >= 0.0
    dmat = jnp.where(lower, jnp.exp(jnp.where(lower, diff, 0.0) * log_gamma((chunk, chunk))), 0.0)
    pos = lax.broadcasted_iota(I32, (chunk, dh), 0).astype(F32)
    k_decay = jnp.exp((chunk - 1.0 - pos) * log_gamma((chunk, dh)))
    q_decay = jnp.exp((pos + 1.0) * log_gamma((chunk, dh)))
    chunk_decay = jnp.exp(chunk * log_gamma((dh, dh)))
    gn = gn_ref[...]
    scale = dh ** -0.5

    def rope(t, cs, sn):
        return t * cs + pltpu.roll(t, dh // 2, 1) * sn

    def body(n, state):
        rows = pl.ds(pl.multiple_of(n * chunk, chunk), chunk)
        cs = cos_ref[rows, :]
        sn = sin_ref[rows, :]
        qc = rope(q_ref[rows, :].astype(F32), cs, sn)
        kc = rope(k_ref[rows, :].astype(F32), cs, sn) * scale
        vc = v_ref[rows, :]
        scores = lax.dot_general(qc.astype(BF16), kc.astype(BF16), (((1,), (1,)), ((), ())),
                                 preferred_element_type=F32) * dmat
        intra = jnp.dot(scores.astype(BF16), vc, preferred_element_type=F32)
        cross = jnp.dot((qc * q_decay).astype(BF16), state.astype(BF16), preferred_element_type=F32)
        out = intra + cross
        kd_t = jnp.transpose(kc * k_decay).astype(BF16)
        state = state * chunk_decay + jnp.dot(kd_t, vc, preferred_element_type=F32)
        ms = jnp.mean(out * out, axis=-1, keepdims=True)
        y = out * lax.rsqrt(ms + EPS) * gn * _silu(g_ref[rows, :].astype(F32))
        o_ref[rows, :] = y.astype(BF16)
        return state

    lax.fori_loop(0, seq // chunk, body, jnp.zeros((dh, dh), F32), unroll=unroll)


RET_CHUNK = 512
RET_UNROLL = 2


def _retention(u3, cos2, sin2, gn, n_heads, col0):
    b, s, _ = u3.shape
    spec = lambda off: pl.BlockSpec((None, s, HEAD_DIM), lambda bi, h: (bi, 0, col0 + off + h))
    return pl.pallas_call(
        functools.partial(_ret_kernel, chunk=RET_CHUNK, unroll=RET_UNROLL),
        grid=(b, n_heads),
        in_specs=[spec(0), spec(n_heads), spec(2 * n_heads), spec(3 * n_heads),
                  pl.BlockSpec((s, HEAD_DIM), lambda bi, h: (0, 0)),
                  pl.BlockSpec((s, HEAD_DIM), lambda bi, h: (0, 0)),
                  pl.BlockSpec((1, HEAD_DIM), lambda bi, h: (0, h))],
        out_specs=pl.BlockSpec((None, s, HEAD_DIM), lambda bi, h: (bi, 0, h)),
        out_shape=jax.ShapeDtypeStruct((b, s, n_heads * HEAD_DIM), BF16),
        compiler_params=_cparams(("arbitrary", "arbitrary")),
        name="retention",
    )(u3, u3, u3, u3, cos2, sin2, gn.reshape(1, -1))


def _outproj_kernel(sb_ref, rt_ref, wa_ref, wb_ref, x_ref, g1_ref, n2_ref, sc_ref, sh_ref, wr_ref,
                    x1_ref, h2_ref, hp_ref, lt_ref, *, chunk):
    d = x_ref.shape[1]
    half = d // 2
    g1 = g1_ref[0]
    n2 = n2_ref[...]
    sc = sc_ref[0]
    sh = sh_ref[0]
    for r in range(x_ref.shape[0] // chunk):
        rows = pl.ds(r * chunk, chunk)
        o = jnp.dot(sb_ref[rows, :], wa_ref[...], preferred_element_type=F32)
        o = o + jnp.dot(rt_ref[rows, :], wb_ref[...], preferred_element_type=F32)
        x1 = x_ref[rows, :] + g1 * o
        x1_ref[rows, :] = x1
        h2 = _norm_mod(x1, n2, sc, sh).astype(BF16)
        h2_ref[rows, :] = h2
        lt_ref[:, rows] = lax.dot_general(wr_ref[...], h2, (((1,), (1,)), ((), ())),
                                          preferred_element_type=F32)
        words = _pack_pair(h2[:, :half].astype(F32), h2[:, half:].astype(F32))
        for j in range(ROW_WORDS):
            hp_ref[pl.ds(r * chunk * ROW_WORDS + j, chunk, stride=ROW_WORDS), :] = (
                words[:, j * LANES:(j + 1) * LANES])


def _outproj(sb2, rt2, wa, wb, x2, g1, n2, sc2, sh2, wr_t, seq):
    t, d = x2.shape
    tm = 512
    per_batch = seq // tm
    mod = pl.BlockSpec((1, 1, d), lambda i: (i // per_batch, 0, 0))
    return pl.pallas_call(
        functools.partial(_outproj_kernel, chunk=512),
        grid=(t // tm,),
        in_specs=[
            pl.BlockSpec((tm, sb2.shape[1]), lambda i: (i, 0)),
            pl.BlockSpec((tm, rt2.shape[1]), lambda i: (i, 0)),
            pl.BlockSpec(wa.shape, lambda i: (0, 0)),
            pl.BlockSpec(wb.shape, lambda i: (0, 0)),
            pl.BlockSpec((tm, d), lambda i: (i, 0)),
            mod,
            pl.BlockSpec((1, d), lambda i: (0, 0)),
            mod, mod,
            pl.BlockSpec(wr_t.shape, lambda i: (0, 0)),
        ],
        out_specs=[
            pl.BlockSpec((tm, d), lambda i: (i, 0)),
            pl.BlockSpec((tm, d), lambda i: (i, 0)),
            pl.BlockSpec((tm * ROW_WORDS, LANES), lambda i: (i, 0)),
            pl.BlockSpec((N_EXPERTS, tm), lambda i: (0, i)),
        ],
        out_shape=[
            jax.ShapeDtypeStruct((t, d), F32),
            jax.ShapeDtypeStruct((t, d), BF16),
            jax.ShapeDtypeStruct((t * ROW_WORDS, LANES), U32),
            jax.ShapeDtypeStruct((N_EXPERTS, t), F32),
        ],
        compiler_params=_cparams(("arbitrary",)),
        name="outproj",
    )(sb2, rt2, wa, wb, x2, g1, n2.reshape(1, d), sc2, sh2, wr_t)


def _beats(row, allv, row_idx, idx):
    return (row > allv) | ((row == allv) & (row_idx < idx))


def _route_kernel(lt_ref, bias_ref, h2_ref, sg_ref, su_ref, sd_ref,
                  w_ref, rank_ref, cnt_ref, shared_ref, carry_ref):
    tr = lt_ref.shape[1]
    gsz = N_EXPERTS // N_GROUPS
    neg_inf = jnp.float32(-jnp.inf)

    @pl.when(pl.program_id(0) == 0)
    def _():
        carry_ref[...] = jnp.zeros_like(carry_ref)

    h2 = h2_ref[...]
    sa = jnp.dot(h2, sg_ref[...], preferred_element_type=F32)
    su = jnp.dot(h2, su_ref[...], preferred_element_type=F32)
    shared_ref[...] = jnp.dot((_silu(sa) * su).astype(BF16), sd_ref[...],
                              preferred_element_type=F32).astype(BF16)

    scores = jax.nn.sigmoid(lt_ref[...])
    choice = scores + bias_ref[...]
    sub = lax.broadcasted_iota(I32, (gsz, tr), 0)
    gs_rows = []
    for g in range(N_GROUPS):
        cg = choice[g * gsz:(g + 1) * gsz, :]
        m1 = jnp.max(cg, axis=0, keepdims=True)
        first = jnp.min(jnp.where(cg == m1, sub, gsz), axis=0, keepdims=True)
        m2 = jnp.max(jnp.where(sub == first, neg_inf, cg), axis=0, keepdims=True)
        gs_rows.append(m1 + m2)
    gs = jnp.concatenate(gs_rows, axis=0)
    gi = lax.broadcasted_iota(I32, (N_GROUPS, tr), 0)
    grank = jnp.zeros((N_GROUPS, tr), I32)
    for g in range(N_GROUPS):
        grank = grank + _beats(gs[g:g + 1, :], gs, g, gi).astype(I32)
    gmask = grank < TOPK_GROUP
    emask = jnp.concatenate(
        [jnp.broadcast_to(gmask[g:g + 1, :], (gsz, tr)) for g in range(N_GROUPS)], axis=0)
    masked = jnp.where(emask, choice, neg_inf)
    ei = lax.broadcasted_iota(I32, (N_EXPERTS, tr), 0)
    erank = jnp.zeros((N_EXPERTS, tr), I32)
    for e in range(N_EXPERTS):
        erank = erank + _beats(masked[e:e + 1, :], masked, e, ei).astype(I32)
    sel = erank < TOP_K
    ssel = jnp.where(sel, scores, 0.0)
    denom = jnp.sum(ssel, axis=0, keepdims=True)
    w_ref[...] = ssel / denom * ROUTED_SCALE

    self = sel.astype(F32)
    upper = (lax.broadcasted_iota(I32, (tr, tr), 0) < lax.broadcasted_iota(I32, (tr, tr), 1))
    prefix = jnp.dot(self.astype(BF16), upper.astype(BF16), preferred_element_type=F32)
    carry = carry_ref[...]
    rank = prefix + carry[:, 0:1]
    rank_ref[...] = jnp.where(sel, rank, -1.0).astype(I32)
    carry = carry + jnp.sum(self, axis=1, keepdims=True)
    carry_ref[...] = carry
    cnt_ref[...] = carry


def _route(lt, bias, h2, sg, su, sd):
    e, t = lt.shape
    d = h2.shape[1]
    tr = min(512, t)
    full = lambda a: pl.BlockSpec(a.shape, lambda i: (0, 0))
    return pl.pallas_call(
        _route_kernel,
        grid=(t // tr,),
        in_specs=[pl.BlockSpec((e, tr), lambda i: (0, i)),
                  pl.BlockSpec((e, 1), lambda i: (0, 0)),
                  pl.BlockSpec((tr, d), lambda i: (i, 0)),
                  full(sg), full(su), full(sd)],
        out_specs=[pl.BlockSpec((e, tr), lambda i: (0, i)),
                   pl.BlockSpec((e, tr), lambda i: (0, i)),
                   pl.BlockSpec((e, LANES), lambda i: (0, 0)),
                   pl.BlockSpec((tr, d), lambda i: (i, 0))],
        out_shape=[jax.ShapeDtypeStruct((e, t), F32),
                   jax.ShapeDtypeStruct((e, t), I32),
                   jax.ShapeDtypeStruct((e, LANES), F32),
                   jax.ShapeDtypeStruct((t, d), BF16)],
        scratch_shapes=[pltpu.VMEM((e, LANES), F32)],
        compiler_params=_cparams(("arbitrary",)),
        name="route",
    )(lt, bias.reshape(e, 1), h2, sg, su, sd)


def _plan_kernel(w_ref, rank_ref, cnt_ref, pos_ref, wc_ref, seg_ref):
    cnt = cnt_ref[...]
    nblk = jnp.floor((cnt + (MOE_BLOCK - 1.0)) * (1.0 / MOE_BLOCK))
    lower = (lax.broadcasted_iota(I32, (N_EXPERTS, N_EXPERTS), 1)
             < lax.broadcasted_iota(I32, (N_EXPERTS, N_EXPERTS), 0)).astype(BF16)
    bstart = jnp.dot(lower, nblk.astype(BF16), preferred_element_type=F32)
    bend = bstart + nblk

    @pl.when(pl.program_id(0) == 0)
    def _():
        seg_ref[...] = jnp.concatenate(
            [bstart[:, 0:1] * MOE_BLOCK, cnt[:, 0:1], nblk[:, 0:1] * MOE_BLOCK, bend[:, 0:1]],
            axis=1).astype(I32)

    rank = rank_ref[...]
    sel = rank >= 0
    pos = bstart[:, 0:1] * MOE_BLOCK + rank.astype(F32)
    slot = jnp.dot(lower, sel.astype(BF16), preferred_element_type=F32)
    w = w_ref[...]
    pos_rows, w_rows = [], []
    for k in range(TOP_K):
        m = sel & (slot == float(k))
        pos_rows.append(jnp.sum(jnp.where(m, pos, 0.0), axis=0, keepdims=True))
        w_rows.append(jnp.sum(jnp.where(m, w, 0.0), axis=0, keepdims=True))
    pos_ref[...] = jnp.concatenate(pos_rows, axis=0).astype(I32)
    wc_ref[...] = jnp.concatenate(w_rows, axis=0)


def _plan(w, rank, cnt):
    e, t = w.shape
    tr = min(1024, t)
    return pl.pallas_call(
        _plan_kernel,
        grid=(t // tr,),
        in_specs=[pl.BlockSpec((e, tr), lambda i: (0, i)),
                  pl.BlockSpec((e, tr), lambda i: (0, i)),
                  pl.BlockSpec((e, LANES), lambda i: (0, 0))],
        out_specs=[pl.BlockSpec((TOP_K, tr), lambda i: (0, i)),
                   pl.BlockSpec((TOP_K, tr), lambda i: (0, i)),
                   pl.BlockSpec((e, 4), lambda i: (0, 0))],
        out_shape=[jax.ShapeDtypeStruct((TOP_K, t), I32),
                   jax.ShapeDtypeStruct((TOP_K, t), F32),
                   jax.ShapeDtypeStruct((e, 4), I32)],
        compiler_params=_cparams(("arbitrary",)),
        name="plan",
    )(w, rank, cnt)


def _row_copy(src, src_row, dst, dst_row, sem):
    return pltpu.make_async_copy(src.at[pl.ds(src_row * ROW_WORDS, ROW_WORDS), :],
                                 dst.at[pl.ds(dst_row * ROW_WORDS, ROW_WORDS), :], sem)


def _dispatch_kernel(pos_ref, seg_ref, hp_ref, xs_ref, zero_ref, sem, zsem):
    tq = pos_ref.shape[0] // TOP_K
    blk_words = MOE_BLOCK * ROW_WORDS
    n_blocks = xs_ref.shape[0] // blk_words

    @pl.when(pl.program_id(0) == 0)
    def _():
        zero_ref[...] = jnp.zeros_like(zero_ref)
        zrow = zero_ref.at[pl.ds(0, ROW_WORDS), :]

        def per_expert(e, n):
            start = seg_ref[e, 0] + seg_ref[e, 1]
            npad = seg_ref[e, 2] - seg_ref[e, 1]

            def fill(r, c):
                pltpu.make_async_copy(
                    zrow, xs_ref.at[pl.ds((start + r) * ROW_WORDS, ROW_WORDS), :], zsem).start()
                return c

            lax.fori_loop(0, npad, fill, 0)
            return n + npad

        n_pad_rows = lax.fori_loop(0, N_EXPERTS, per_expert, 0)

        def drain_row(r, c):
            pltpu.make_async_copy(zrow, xs_ref.at[pl.ds(0, ROW_WORDS), :], zsem).wait()
            return c

        lax.fori_loop(0, n_pad_rows, drain_row, 0)
        n_used = seg_ref[N_EXPERTS - 1, 3]

        def fill_block(b, c):
            pltpu.make_async_copy(
                zero_ref, xs_ref.at[pl.ds(b * blk_words, blk_words), :], zsem).start()
            return c

        lax.fori_loop(n_used, n_blocks, fill_block, 0)

        def drain_block(b, c):
            pltpu.make_async_copy(zero_ref, xs_ref.at[pl.ds(0, blk_words), :], zsem).wait()
            return c

        lax.fori_loop(n_used, n_blocks, drain_block, 0)

    def issue(t, c):
        for k in range(TOP_K):
            _row_copy(hp_ref, t, xs_ref, pos_ref[t * TOP_K + k], sem).start(priority=k % 2)
        return c

    lax.fori_loop(0, tq, issue, 0)
    for k in range(TOP_K):
        pltpu.make_async_copy(hp_ref, xs_ref.at[pl.ds(0, tq * ROW_WORDS), :], sem).wait()


def _dispatch(pos_tm, seg, hp, n_rows):
    t = pos_tm.shape[0] // TOP_K
    tq = min(1024, t)
    return pl.pallas_call(
        _dispatch_kernel,
        grid=(t // tq,),
        in_specs=[pl.BlockSpec((tq * TOP_K,), lambda i: (i,), memory_space=pltpu.SMEM),
                  pl.BlockSpec(seg.shape, lambda i: (0, 0), memory_space=pltpu.SMEM),
                  pl.BlockSpec((tq * ROW_WORDS, LANES), lambda i: (i, 0))],
        out_specs=pl.BlockSpec(memory_space=pl.ANY),
        out_shape=jax.ShapeDtypeStruct((n_rows * ROW_WORDS, LANES), U32),
        scratch_shapes=[pltpu.VMEM((MOE_BLOCK * ROW_WORDS, LANES), U32),
                        pltpu.SemaphoreType.DMA(()), pltpu.SemaphoreType.DMA(())],
        compiler_params=_cparams(("arbitrary",)),
        name="dispatch",
    )(pos_tm, seg, hp)


def _expert_kernel(seg_ref, xs_ref, wg_ref, wu_ref, wd_ref, ys_ref,
                   wg32, wu32, wd32, wgb, wub, wdb, xbuf, ybuf, wsem, xsem, ysem):
    e = pl.program_id(0)
    n_exp = pl.num_programs(0)
    last_step = e == n_exp - 1

    def weight_copies(ex):
        slot = ex & 1
        return [pltpu.make_async_copy(src.at[ex], dst.at[slot], wsem.at[slot])
                for src, dst in ((wg_ref, wg32), (wu_ref, wu32), (wd_ref, wd32))]

    @pl.when(e == 0)
    def _():
        for cp in weight_copies(0):
            cp.start()

    for cp in weight_copies(e):
        cp.wait()
    rows = MOE_BLOCK
    blk_words = MOE_BLOCK * ROW_WORDS
    n_blocks = ys_ref.shape[0] // blk_words
    first = lax.div(seg_ref[e * 4], MOE_BLOCK)
    end = seg_ref[e * 4 + 3]
    n_used = seg_ref[(pl.num_programs(0) - 1) * 4 + 3]

    x_slots = xbuf.shape[0]
    x_ahead = x_slots - 2

    def x_copy(g):
        slot = g & (x_slots - 1)
        return pltpu.make_async_copy(xs_ref.at[pl.ds(g * blk_words, blk_words), :],
                                     xbuf.at[slot], xsem.at[slot])

    def y_copy(g):
        slot = g & 1
        return pltpu.make_async_copy(ybuf.at[slot],
                                     ys_ref.at[pl.ds(g * blk_words, blk_words), :], ysem.at[slot])

    for g0 in range(x_ahead):
        @pl.when(jnp.logical_and(e == 0, n_used > g0))
        def _():
            x_copy(g0).start()

    @pl.when(end > first)
    def _():
        wslot = e & 1
        wgb[...] = wg32[wslot].astype(BF16)
        wub[...] = wu32[wslot].astype(BF16)
        wdb[...] = wd32[wslot].astype(BF16)

    @pl.when(e + 1 < n_exp)
    def _():
        for cp in weight_copies(e + 1):
            cp.start(priority=1)

    @pl.when(end > first)
    def _():
        def block(g, c):
            slot = g & 1
            x_copy(g).wait()

            @pl.when(g + x_ahead < n_used)
            def _():
                x_copy(g + x_ahead).start()

            @pl.when(g >= 2)
            def _():
                y_copy(g - 2).wait()

            xin = xbuf.at[g & (x_slots - 1)]
            his, los = [], []
            for j in range(ROW_WORDS):
                hi, lo = _unpack_pair(xin[pl.ds(j, rows, stride=ROW_WORDS), :])
                his.append(hi.astype(BF16))
                los.append(lo.astype(BF16))
            xb = jnp.concatenate(his + los, axis=1)
            a = jnp.dot(xb, wgb[...], preferred_element_type=F32)
            u = jnp.dot(xb, wub[...], preferred_element_type=F32)
            hid = (_silu(a) * u).astype(BF16)
            out = jnp.dot(hid, wdb[...], preferred_element_type=F32).astype(BF16).astype(F32)
            half = out.shape[1] // 2
            words = _pack_pair(out[:, :half], out[:, half:])
            yout = ybuf.at[slot]
            for j in range(ROW_WORDS):
                yout[pl.ds(j, rows, stride=ROW_WORDS), :] = words[:, j * LANES:(j + 1) * LANES]
            y_copy(g).start()
            return c

        lax.fori_loop(first, end, block, 0)

    @pl.when(last_step)
    def _():
        @pl.when(n_used >= 2)
        def _():
            y_copy(n_used - 2).wait()

        @pl.when(n_used >= 1)
        def _():
            y_copy(n_used - 1).wait()

        ybuf[0] = jnp.zeros(ybuf.shape[1:], ybuf.dtype)

        def fill(g, c):
            pltpu.make_async_copy(ybuf.at[0], ys_ref.at[pl.ds(g * blk_words, blk_words), :],
                                  ysem.at[0]).start()
            return c

        lax.fori_loop(n_used, n_blocks, fill, 0)

        def drain(g, c):
            pltpu.make_async_copy(ybuf.at[0], ys_ref.at[pl.ds(0, blk_words), :], ysem.at[0]).wait()
            return c

        lax.fori_loop(n_used, n_blocks, drain, 0)


def _experts(seg_flat, xs, w_gate, w_up, w_down):
    n_exp, d, f = w_gate.shape
    blk_words = MOE_BLOCK * ROW_WORDS
    grid_spec = pltpu.PrefetchScalarGridSpec(
        num_scalar_prefetch=1,
        grid=(n_exp,),
        in_specs=[
            pl.BlockSpec(memory_space=pl.ANY),
            pl.BlockSpec(memory_space=pl.ANY),
            pl.BlockSpec(memory_space=pl.ANY),
            pl.BlockSpec(memory_space=pl.ANY),
        ],
        out_specs=pl.BlockSpec(memory_space=pl.ANY),
        scratch_shapes=[pltpu.VMEM((2, d, f), F32), pltpu.VMEM((2, d, f), F32), pltpu.VMEM((2, f, d), F32),
                        pltpu.VMEM((d, f), BF16), pltpu.VMEM((d, f), BF16), pltpu.VMEM((f, d), BF16),
                        pltpu.VMEM((4, blk_words, LANES), U32), pltpu.VMEM((2, blk_words, LANES), U32),
                        pltpu.SemaphoreType.DMA((2,)), pltpu.SemaphoreType.DMA((4,)),
                        pltpu.SemaphoreType.DMA((2,))],
    )
    return pl.pallas_call(
        _expert_kernel,
        grid_spec=grid_spec,
        out_shape=jax.ShapeDtypeStruct(xs.shape, U32),
        compiler_params=_cparams(("arbitrary",)),
        name="experts",
    )(seg_flat, xs, w_gate, w_up, w_down)


def _combine_kernel(pos_ref, nxt_ref, ys_ref, wc_ref, shared_ref, x1_ref,
                    g2_ref, nf_ref, scf_ref, shf_ref, o_ref, buf, sem):
    i = pl.program_id(0)
    tq = wc_ref.shape[1]
    slot = i & 1

    def gather(p_ref, to_slot):
        def issue(t, c):
            for k in range(TOP_K):
                pltpu.make_async_copy(
                    ys_ref.at[pl.ds(p_ref[t * TOP_K + k] * ROW_WORDS, ROW_WORDS), :],
                    buf.at[to_slot, k, pl.ds(t * ROW_WORDS, ROW_WORDS), :],
                    sem.at[to_slot]).start(priority=k % 2)
            return c

        lax.fori_loop(0, tq, issue, 0)

    @pl.when(i == 0)
    def _():
        gather(pos_ref, 0)

    @pl.when(i + 1 < pl.num_programs(0))
    def _():
        gather(nxt_ref, 1 - slot)

    for k in range(TOP_K):
        pltpu.make_async_copy(ys_ref.at[pl.ds(0, tq * ROW_WORDS), :], buf.at[slot, k],
                              sem.at[slot]).wait()

    wt = jnp.transpose(wc_ref[...])
    his, los = [], []
    for j in range(ROW_WORDS):
        acc_hi = jnp.zeros((tq, LANES), F32)
        acc_lo = jnp.zeros((tq, LANES), F32)
        for k in range(TOP_K):
            hi, lo = _unpack_pair(buf.at[slot, k][pl.ds(j, tq, stride=ROW_WORDS), :])
            wk = wt[:, k:k + 1]
            acc_hi = acc_hi + wk * hi
            acc_lo = acc_lo + wk * lo
        his.append(acc_hi)
        los.append(acc_lo)
    y = jnp.concatenate(his + los, axis=1) + shared_ref[...].astype(F32)
    x2 = x1_ref[...] + g2_ref[0] * y
    o_ref[...] = _norm_mod(x2, nf_ref[...], scf_ref[0], shf_ref[0])


def _combine(pos_tm, ys, w_c, shared, x1, g2, nf, scf, shf, seq):
    t, d = x1.shape
    tq = 128
    n_tiles = t // tq
    per_batch = seq // tq
    mod = pl.BlockSpec((1, 1, d), lambda i: (i // per_batch, 0, 0))
    return pl.pallas_call(
        _combine_kernel,
        grid=(n_tiles,),
        in_specs=[
            pl.BlockSpec((tq * TOP_K,), lambda i: (i,), memory_space=pltpu.SMEM),
            pl.BlockSpec((tq * TOP_K,), lambda i: (jnp.minimum(i + 1, n_tiles - 1),),
                         memory_space=pltpu.SMEM),
            pl.BlockSpec(memory_space=pl.ANY),
            pl.BlockSpec((TOP_K, tq), lambda i: (0, i)),
            pl.BlockSpec((tq, d), lambda i: (i, 0)),
            pl.BlockSpec((tq, d), lambda i: (i, 0)),
            mod,
            pl.BlockSpec((1, d), lambda i: (0, 0)),
            mod, mod,
        ],
        out_specs=pl.BlockSpec((tq, d), lambda i: (i, 0)),
        out_shape=jax.ShapeDtypeStruct((t, d), F32),
        scratch_shapes=[pltpu.VMEM((2, TOP_K, tq * ROW_WORDS, LANES), U32),
                        pltpu.SemaphoreType.DMA((2,))],
        compiler_params=_cparams(("arbitrary",)),
        name="combine",
    )(pos_tm, pos_tm, ys, w_c, shared, x1, g2, nf.reshape(1, d), scf, shf)


def _rope_tables(seq):
    pos = jnp.arange(seq, dtype=F32)
    inv_freq = ROPE_BASE ** (-jnp.arange(0, HEAD_DIM, 2, dtype=F32) / HEAD_DIM)
    ang = pos[:, None] * inv_freq[None, :]
    cos, sin = jnp.cos(ang), jnp.sin(ang)
    return jnp.concatenate([cos, cos], axis=1), jnp.concatenate([-sin, sin], axis=1)


def kernel(x, c, w_ada, b_ada, norm1_g, w_in, ret_gn_g, w_out, norm2_g, w_router, router_bias,
           w_gate, w_up, w_down, ws_gate, ws_up, ws_down, w_ada_final, b_ada_final, norm_f_g):
    bsz, seq, d = x.shape
    t = bsz * seq
    depth = w_ada.shape[0]
    n_heads = d // (2 * HEAD_DIM)
    sb_w = n_heads * HEAD_DIM
    cos2, sin2 = _rope_tables(seq)
    c_pad = jnp.pad(c, ((0, 8 - bsz % 8), (0, 0)))
    n_rows = t * TOP_K + N_EXPERTS * MOE_BLOCK
    as_mod = lambda v: v.reshape(bsz, 1, d)

    x2 = x.reshape(t, d)
    for l in range(depth):
        mod = _ada(c_pad, w_ada[l], b_ada[l])[:bsz]
        sh1, sc1, g1, sh2, sc2, g2 = [as_mod(m) for m in jnp.split(mod, 6, axis=-1)]
        u = _inproj(x2, norm1_g[l], sc1, sh1, w_in[l].astype(BF16), seq)
        u3 = u.reshape(bsz, seq, -1)
        sb = _sb_attention(u3, n_heads)
        rt = _retention(u3, cos2, sin2, ret_gn_g[l], n_heads, 3 * n_heads)
        wo = w_out[l].astype(BF16)
        x1, h2, hp, lt = _outproj(sb.reshape(t, sb_w), rt.reshape(t, -1), wo[:sb_w], wo[sb_w:], x2,
                                  g1, norm2_g[l], sc2, sh2, jnp.transpose(w_router[l]).astype(BF16), seq)
        w_e, rank_e, cnt, shared = _route(lt, router_bias[l], h2, ws_gate[l].astype(BF16),
                                          ws_up[l].astype(BF16), ws_down[l].astype(BF16))
        pos_c, w_c, seg = _plan(w_e, rank_e, cnt)
        pos_tm = jnp.transpose(pos_c).reshape(-1)
        xs = _dispatch(pos_tm, seg, hp, n_rows)
        ys = _experts(seg.reshape(-1), xs, w_gate[l], w_up[l], w_down[l])
        if l + 1 < depth:
            raise NotImplementedError("only the final layer fuses the output norm")
        modf = _ada(c_pad, w_ada_final, b_ada_final)[:bsz]
        shf, scf = [as_mod(m) for m in jnp.split(modf, 2, axis=-1)]
        x2 = _combine(pos_tm, ys, w_c, shared, x1, g2, norm_f_g, scf, shf, seq)
    return x2.reshape(bsz, seq, d)
```

```python
import functools

import jax
import jax.numpy as jnp
from jax import lax
from jax.experimental import pallas as pl
from jax.experimental.pallas import tpu as pltpu

F32 = jnp.float32
BF16 = jnp.bfloat16
I32 = jnp.int32
U32 = jnp.uint32

HEAD_DIM = 128
N_EXPERTS = 64
TOP_K = 8
N_GROUPS = 8
TOPK_GROUP = 4
ROUTED_SCALE = 2.5
EPS = 1e-6
ROPE_BASE = 10000.0

LANES = 128
ROW_WORDS = 8
MOE_BLOCK = 256
V7X_VMEM_LIMIT = 56 * 1024 * 1024
LOG2_E = 1.4426950408889634
SB_SKIP_BOUND = -115.0 * LOG2_E


def _cparams(sem, vmem=V7X_VMEM_LIMIT):
    return pltpu.CompilerParams(dimension_semantics=sem, vmem_limit_bytes=vmem)


def _silu(v):
    return v * jax.nn.sigmoid(v)


def _pack_pair(hi_f32, lo_f32):
    hw = pltpu.bitcast(hi_f32, U32) & jnp.uint32(0xFFFF0000)
    lw = pltpu.bitcast(lo_f32, U32) >> 16
    return hw | lw


def _unpack_pair(w):
    hi = pltpu.bitcast(w & jnp.uint32(0xFFFF0000), F32)
    lo = pltpu.bitcast(w << 16, F32)
    return hi, lo


def _ada_kernel(c_ref, w_ref, b_ref, o_ref):
    cs = _silu(c_ref[...]).astype(BF16)
    o_ref[...] = jnp.dot(cs, w_ref[...].astype(BF16), preferred_element_type=F32) + b_ref[...]


def _ada(c_pad, w, b):
    rows, d = c_pad.shape
    n = w.shape[1]
    tn = 1024
    return pl.pallas_call(
        _ada_kernel,
        grid=(n // tn,),
        in_specs=[
            pl.BlockSpec((rows, d), lambda j: (0, 0)),
            pl.BlockSpec((d, tn), lambda j: (0, j)),
            pl.BlockSpec((1, tn), lambda j: (0, j)),
        ],
        out_specs=pl.BlockSpec((rows, tn), lambda j: (0, j)),
        out_shape=jax.ShapeDtypeStruct((rows, n), F32),
        compiler_params=_cparams(("arbitrary",)),
        name="ada",
    )(c_pad, w, b.reshape(1, n))


def _norm_mod(x, g, sc, sh):
    ms = jnp.mean(x * x, axis=-1, keepdims=True)
    return (x * lax.rsqrt(ms + EPS) * g) * (1.0 + sc) + sh


def _inproj_kernel(x_ref, g_ref, sc_ref, sh_ref, w_ref, o_ref, h_ref, *, chunk):
    first = pl.program_id(1) == 0

    @pl.when(first)
    def _():
        g = g_ref[...]
        sc = sc_ref[0]
        sh = sh_ref[0]
        for r in range(x_ref.shape[0] // chunk):
            rows = pl.ds(r * chunk, chunk)
            h = _norm_mod(x_ref[rows, :], g, sc, sh).astype(BF16)
            h_ref[rows, :] = h
            o_ref[rows, :] = jnp.dot(h, w_ref[...], preferred_element_type=F32).astype(BF16)

    @pl.when(jnp.logical_not(first))
    def _():
        o_ref[...] = jnp.dot(h_ref[...], w_ref[...], preferred_element_type=F32).astype(BF16)


def _inproj(x2, g, sc, sh, w_bf, seq):
    t, d = x2.shape
    n = w_bf.shape[1]
    tm = 1024
    tn = n // 4 if (n // 4) % (2 * LANES) == 0 else 1024
    per_batch = seq // tm
    return pl.pallas_call(
        functools.partial(_inproj_kernel, chunk=256),
        grid=(t // tm, n // tn),
        in_specs=[
            pl.BlockSpec((tm, d), lambda i, j: (i, 0)),
            pl.BlockSpec((1, d), lambda i, j: (0, 0)),
            pl.BlockSpec((1, 1, d), lambda i, j: (i // per_batch, 0, 0)),
            pl.BlockSpec((1, 1, d), lambda i, j: (i // per_batch, 0, 0)),
            pl.BlockSpec((d, tn), lambda i, j: (0, j)),
        ],
        out_specs=pl.BlockSpec((tm, tn), lambda i, j: (i, j)),
        out_shape=jax.ShapeDtypeStruct((t, n), BF16),
        scratch_shapes=[pltpu.VMEM((tm, d), BF16)],
        compiler_params=_cparams(("arbitrary", "arbitrary")),
        name="inproj",
    )(x2, g.reshape(1, d), sc, sh, w_bf)


def _sb_kernel(q_ref, k_ref, v_ref, o_ref, *, group):
    seq = q_ref.shape[0]
    blk = LANES
    scale = HEAD_DIM ** -0.5 * LOG2_E
    row = lax.broadcasted_iota(I32, (blk, blk), 0)
    col = lax.broadcasted_iota(I32, (blk, blk), 1)
    strict = col < row
    tr = lax.broadcasted_iota(I32, (2 * blk, 2 * blk), 0) % blk
    tc = lax.broadcasted_iota(I32, (2 * blk, 2 * blk), 1)
    tri = jnp.where((tc >= blk) | (tr > tc), 1.0, 0.0).astype(BF16)

    def key_rows(j):
        return pl.ds(pl.multiple_of(j * blk, blk), blk)

    def logits(qb, kb):
        z = lax.dot_general(qb, kb, (((1,), (1,)), ((), ())), preferred_element_type=F32) * scale
        sp = jnp.maximum(z, 0.0) + jnp.log2(1.0 + jnp.exp2(-jnp.abs(z)))
        return -sp, z - sp

    def suffix_sums(log_surv):
        hi = log_surv.astype(BF16)
        lo = (log_surv - hi.astype(F32)).astype(BF16)
        r = jnp.dot(jnp.concatenate([hi, lo], axis=1), tri, preferred_element_type=F32)
        return r[:, :blk], r[:, blk:]

    def tile(qb, j, g, carry, acc):
        log_surv, log_beta = logits(qb, k_ref[key_rows(j), g * blk:(g + 1) * blk])
        after, total = suffix_sums(log_surv)
        a = jnp.exp2(log_beta + after + carry)
        acc = acc + jnp.dot(a.astype(BF16), v_ref[key_rows(j), g * blk:(g + 1) * blk],
                            preferred_element_type=F32)
        return carry + total, acc

    def group_start(qbs, i):
        n_near = 3
        js = [i, jnp.maximum(i - 1, 0), jnp.maximum(i - 2, 0)]
        ivec = jnp.full((blk, blk), i, I32)
        masks = [strict, ivec >= 1, ivec >= 2]
        log_betas, parts = [], []
        for g in range(group):
            cols = slice(g * blk, (g + 1) * blk)
            kcat = jnp.concatenate([k_ref[key_rows(j), cols] for j in js], axis=0)
            log_surv, log_beta = logits(qbs[g], kcat)
            log_betas.append(log_beta)
            for m in range(n_near):
                parts.append(jnp.where(masks[m], log_surv[:, m * blk:(m + 1) * blk], 0.0))
        after, total = suffix_sums(jnp.concatenate(parts, axis=0))
        out = []
        for g in range(group):
            cols = slice(g * blk, (g + 1) * blk)
            carry = jnp.zeros((blk, blk), F32)
            weights = []
            for m in range(n_near):
                rows = slice((g * n_near + m) * blk, (g * n_near + m + 1) * blk)
                a = jnp.exp2(log_betas[g][:, m * blk:(m + 1) * blk] + after[rows] + carry)
                weights.append(jnp.where(masks[m], a, 0.0).astype(BF16))
                carry = carry + total[rows]
            vcat = jnp.concatenate([v_ref[key_rows(j), cols] for j in js], axis=0)
            acc = jnp.dot(jnp.concatenate(weights, axis=1), vcat, preferred_element_type=F32)
            out.append((carry, acc))
        return out

    q_blocks = q_ref.shape[0] // blk

    def q_block(local, c):
        i = pl.program_id(1) * q_blocks + local
        rows = pl.ds(pl.multiple_of(local * blk, blk), blk)
        qbs = [q_ref[rows, g * blk:(g + 1) * blk] for g in range(group)]
        first = group_start(qbs, i)
        carries = tuple(f[0] for f in first)
        accs = tuple(f[1] for f in first)

        def cond(st):
            j, crs, _ = st
            top = functools.reduce(jnp.maximum, crs)
            return jnp.logical_and(j >= 0, jnp.max(top) > SB_SKIP_BOUND)

        def body(st):
            j, crs, acs = st
            both = [logits(qbs[g], k_ref[key_rows(j), g * blk:(g + 1) * blk]) for g in range(group)]
            after, total = suffix_sums(jnp.concatenate([b[0] for b in both], axis=0))
            new_crs, new_acs = [], []
            for g in range(group):
                rws = slice(g * blk, (g + 1) * blk)
                a = jnp.exp2(both[g][1] + after[rws] + crs[g])
                new_acs.append(acs[g] + jnp.dot(a.astype(BF16), v_ref[key_rows(j), g * blk:(g + 1) * blk],
                                                preferred_element_type=F32))
                new_crs.append(crs[g] + total[rws])
            return j - 1, tuple(new_crs), tuple(new_acs)

        _, _, accs = lax.while_loop(cond, body, (i - 3, carries, accs))
        for g in range(group):
            o_ref[rows, g * blk:(g + 1) * blk] = accs[g].astype(BF16)
        return c

    lax.fori_loop(0, q_blocks, q_block, 0)


SB_ROW_TILE = 512


def _sb_attention(u3, n_heads):
    b, s, _ = u3.shape
    w = n_heads * HEAD_DIM
    tq = min(SB_ROW_TILE, s)
    return pl.pallas_call(
        functools.partial(_sb_kernel, group=n_heads),
        grid=(b, s // tq),
        in_specs=[pl.BlockSpec((None, tq, w), lambda bi, qi: (bi, qi, 0)),
                  pl.BlockSpec((None, s, w), lambda bi, qi: (bi, 0, 1)),
                  pl.BlockSpec((None, s, w), lambda bi, qi: (bi, 0, 2))],
        out_specs=pl.BlockSpec((None, tq, w), lambda bi, qi: (bi, qi, 0)),
        out_shape=jax.ShapeDtypeStruct((b, s, w), BF16),
        compiler_params=_cparams(("arbitrary", "arbitrary")),
        name="sb_attn",
    )(u3, u3, u3)


def _ret_kernel(q_ref, k_ref, v_ref, g_ref, cos_ref, sin_ref, gn_ref, o_ref, *, chunk, unroll):
    seq = q_ref.shape[0]
    dh = HEAD_DIM
    head = pl.program_id(1)

    def log_gamma(shape):
        hv = jnp.full(shape, head, I32).astype(F32)
        return jnp.log(1.0 - jnp.exp2(-5.0 - hv))

    diff = (lax.broadcasted_iota(I32, (chunk, chunk), 0)
            - lax.broadcasted_iota(I32, (chunk, chunk), 1)).astype(F32)
    lower = diff >= 0.0
    dmat = jnp.where(lower, jnp.exp(jnp.where(lower, diff, 0.0) * log_gamma((chunk, chunk))), 0.0)
    pos = lax.broadcasted_iota(I32, (chunk, dh), 0).astype(F32)
    k_decay = jnp.exp((chunk - 1.0 - pos) * log_gamma((chunk, dh)))
    q_decay = jnp.exp((pos + 1.0) * log_gamma((chunk, dh)))
    chunk_decay = jnp.exp(chunk * log_gamma((dh, dh)))
    gn = gn_ref[...]
    scale = dh ** -0.5

    def rope(t, cs, sn):
        return t * cs + pltpu.roll(t, dh // 2, 1) * sn

    def body(n, state):
        rows = pl.ds(pl.multiple_of(n * chunk, chunk), chunk)
        cs = cos_ref[rows, :]
        sn = sin_ref[rows, :]
        qc = rope(q_ref[rows, :].astype(F32), cs, sn)
        kc = rope(k_ref[rows, :].astype(F32), cs, sn) * scale
        vc = v_ref[rows, :]
        scores = lax.dot_general(qc.astype(BF16), kc.astype(BF16), (((1,), (1,)), ((), ())),
                                 preferred_element_type=F32) * dmat
        intra = jnp.dot(scores.astype(BF16), vc, preferred_element_type=F32)
        cross = jnp.dot((qc * q_decay).astype(BF16), state.astype(BF16), preferred_element_type=F32)
        out = intra + cross
        kd_t = jnp.transpose(kc * k_decay).astype(BF16)
        state = state * chunk_decay + jnp.dot(kd_t, vc, preferred_element_type=F32)
        ms = jnp.mean(out * out, axis=-1, keepdims=True)
        y = out * lax.rsqrt(ms + EPS) * gn * _silu(g_ref[rows, :].astype(F32))
        o_ref[rows, :] = y.astype(BF16)
        return state

    lax.fori_loop(0, seq // chunk, body, jnp.zeros((dh, dh), F32), unroll=unroll)


RET_CHUNK = 512
RET_UNROLL = 2


def _retention(u3, cos2, sin2, gn, n_heads, col0):
    b, s, _ = u3.shape
    spec = lambda off: pl.BlockSpec((None, s, HEAD_DIM), lambda bi, h: (bi, 0, col0 + off + h))
    return pl.pallas_call(
        functools.partial(_ret_kernel, chunk=RET_CHUNK, unroll=RET_UNROLL),
        grid=(b, n_heads),
        in_specs=[spec(0), spec(n_heads), spec(2 * n_heads), spec(3 * n_heads),
                  pl.BlockSpec((s, HEAD_DIM), lambda bi, h: (0, 0)),
                  pl.BlockSpec((s, HEAD_DIM), lambda bi, h: (0, 0)),
                  pl.BlockSpec((1, HEAD_DIM), lambda bi, h: (0, h))],
        out_specs=pl.BlockSpec((None, s, HEAD_DIM), lambda bi, h: (bi, 0, h)),
        out_shape=jax.ShapeDtypeStruct((b, s, n_heads * HEAD_DIM), BF16),
        compiler_params=_cparams(("arbitrary", "arbitrary")),
        name="retention",
    )(u3, u3, u3, u3, cos2, sin2, gn.reshape(1, -1))


def _outproj_kernel(sb_ref, rt_ref, wa_ref, wb_ref, x_ref, g1_ref, n2_ref, sc_ref, sh_ref, wr_ref,
                    x1_ref, h2_ref, hp_ref, lt_ref, *, chunk):
    d = x_ref.shape[1]
    half = d // 2
    g1 = g1_ref[0]
    n2 = n2_ref[...]
    sc = sc_ref[0]
    sh = sh_ref[0]
    for r in range(x_ref.shape[0] // chunk):
        rows = pl.ds(r * chunk, chunk)
        o = jnp.dot(sb_ref[rows, :], wa_ref[...], preferred_element_type=F32)
        o = o + jnp.dot(rt_ref[rows, :], wb_ref[...], preferred_element_type=F32)
        x1 = x_ref[rows, :] + g1 * o
        x1_ref[rows, :] = x1
        h2 = _norm_mod(x1, n2, sc, sh).astype(BF16)
        h2_ref[rows, :] = h2
        lt_ref[:, rows] = lax.dot_general(wr_ref[...], h2, (((1,), (1,)), ((), ())),
                                          preferred_element_type=F32)
        words = _pack_pair(h2[:, :half].astype(F32), h2[:, half:].astype(F32))
        for j in range(ROW_WORDS):
            hp_ref[pl.ds(r * chunk * ROW_WORDS + j, chunk, stride=ROW_WORDS), :] = (
                words[:, j * LANES:(j + 1) * LANES])


def _outproj(sb2, rt2, wa, wb, x2, g1, n2, sc2, sh2, wr_t, seq):
    t, d = x2.shape
    tm = 512
    per_batch = seq // tm
    mod = pl.BlockSpec((1, 1, d), lambda i: (i // per_batch, 0, 0))
    return pl.pallas_call(
        functools.partial(_outproj_kernel, chunk=512),
        grid=(t // tm,),
        in_specs=[
            pl.BlockSpec((tm, sb2.shape[1]), lambda i: (i, 0)),
            pl.BlockSpec((tm, rt2.shape[1]), lambda i: (i, 0)),
            pl.BlockSpec(wa.shape, lambda i: (0, 0)),
            pl.BlockSpec(wb.shape, lambda i: (0, 0)),
            pl.BlockSpec((tm, d), lambda i: (i, 0)),
            mod,
            pl.BlockSpec((1, d), lambda i: (0, 0)),
            mod, mod,
            pl.BlockSpec(wr_t.shape, lambda i: (0, 0)),
        ],
        out_specs=[
            pl.BlockSpec((tm, d), lambda i: (i, 0)),
            pl.BlockSpec((tm, d), lambda i: (i, 0)),
            pl.BlockSpec((tm * ROW_WORDS, LANES), lambda i: (i, 0)),
            pl.BlockSpec((N_EXPERTS, tm), lambda i: (0, i)),
        ],
        out_shape=[
            jax.ShapeDtypeStruct((t, d), F32),
            jax.ShapeDtypeStruct((t, d), BF16),
            jax.ShapeDtypeStruct((t * ROW_WORDS, LANES), U32),
            jax.ShapeDtypeStruct((N_EXPERTS, t), F32),
        ],
        compiler_params=_cparams(("arbitrary",)),
        name="outproj",
    )(sb2, rt2, wa, wb, x2, g1, n2.reshape(1, d), sc2, sh2, wr_t)


def _beats(row, allv, row_idx, idx):
    return (row > allv) | ((row == allv) & (row_idx < idx))


def _route_kernel(lt_ref, bias_ref, h2_ref, sg_ref, su_ref, sd_ref,
                  w_ref, rank_ref, cnt_ref, shared_ref, carry_ref):
    tr = lt_ref.shape[1]
    gsz = N_EXPERTS // N_GROUPS
    neg_inf = jnp.float32(-jnp.inf)

    @pl.when(pl.program_id(0) == 0)
    def _():
        carry_ref[...] = jnp.zeros_like(carry_ref)

    h2 = h2_ref[...]
    sa = jnp.dot(h2, sg_ref[...], preferred_element_type=F32)
    su = jnp.dot(h2, su_ref[...], preferred_element_type=F32)
    shared_ref[...] = jnp.dot((_silu(sa) * su).astype(BF16), sd_ref[...],
                              preferred_element_type=F32).astype(BF16)

    scores = jax.nn.sigmoid(lt_ref[...])
    choice = scores + bias_ref[...]
    sub = lax.broadcasted_iota(I32, (gsz, tr), 0)
    gs_rows = []
    for g in range(N_GROUPS):
        cg = choice[g * gsz:(g + 1) * gsz, :]
        m1 = jnp.max(cg, axis=0, keepdims=True)
        first = jnp.min(jnp.where(cg == m1, sub, gsz), axis=0, keepdims=True)
        m2 = jnp.max(jnp.where(sub == first, neg_inf, cg), axis=0, keepdims=True)
        gs_rows.append(m1 + m2)
    gs = jnp.concatenate(gs_rows, axis=0)
    gi = lax.broadcasted_iota(I32, (N_GROUPS, tr), 0)
    grank = jnp.zeros((N_GROUPS, tr), I32)
    for g in range(N_GROUPS):
        grank = grank + _beats(gs[g:g + 1, :], gs, g, gi).astype(I32)
    gmask = grank < TOPK_GROUP
    emask = jnp.concatenate(
        [jnp.broadcast_to(gmask[g:g + 1, :], (gsz, tr)) for g in range(N_GROUPS)], axis=0)
    masked = jnp.where(emask, choice, neg_inf)
    ei = lax.broadcasted_iota(I32, (N_EXPERTS, tr), 0)
    erank = jnp.zeros((N_EXPERTS, tr), I32)
    for e in range(N_EXPERTS):
        erank = erank + _beats(masked[e:e + 1, :], masked, e, ei).astype(I32)
    sel = erank < TOP_K
    ssel = jnp.where(sel, scores, 0.0)
    denom = jnp.sum(ssel, axis=0, keepdims=True)
    w_ref[...] = ssel / denom * ROUTED_SCALE

    self = sel.astype(F32)
    upper = (lax.broadcasted_iota(I32, (tr, tr), 0) < lax.broadcasted_iota(I32, (tr, tr), 1))
    prefix = jnp.dot(self.astype(BF16), upper.astype(BF16), preferred_element_type=F32)
    carry = carry_ref[...]
    rank = prefix + carry[:, 0:1]
    rank_ref[...] = jnp.where(sel, rank, -1.0).astype(I32)
    carry = carry + jnp.sum(self, axis=1, keepdims=True)
    carry_ref[...] = carry
    cnt_ref[...] = carry


def _route(lt, bias, h2, sg, su, sd):
    e, t = lt.shape
    d = h2.shape[1]
    tr = min(512, t)
    full = lambda a: pl.BlockSpec(a.shape, lambda i: (0, 0))
    return pl.pallas_call(
        _route_kernel,
        grid=(t // tr,),
        in_specs=[pl.BlockSpec((e, tr), lambda i: (0, i)),
                  pl.BlockSpec((e, 1), lambda i: (0, 0)),
                  pl.BlockSpec((tr, d), lambda i: (i, 0)),
                  full(sg), full(su), full(sd)],
        out_specs=[pl.BlockSpec((e, tr), lambda i: (0, i)),
                   pl.BlockSpec((e, tr), lambda i: (0, i)),
                   pl.BlockSpec((e, LANES), lambda i: (0, 0)),
                   pl.BlockSpec((tr, d), lambda i: (i, 0))],
        out_shape=[jax.ShapeDtypeStruct((e, t), F32),
                   jax.ShapeDtypeStruct((e, t), I32),
                   jax.ShapeDtypeStruct((e, LANES), F32),
                   jax.ShapeDtypeStruct((t, d), BF16)],
        scratch_shapes=[pltpu.VMEM((e, LANES), F32)],
        compiler_params=_cparams(("arbitrary",)),
        name="route",
    )(lt, bias.reshape(e, 1), h2, sg, su, sd)


def _plan_kernel(w_ref, rank_ref, cnt_ref, pos_ref, wc_ref, seg_ref):
    cnt = cnt_ref[...]
    nblk = jnp.floor((cnt + (MOE_BLOCK - 1.0)) * (1.0 / MOE_BLOCK))
    lower = (lax.broadcasted_iota(I32, (N_EXPERTS, N_EXPERTS), 1)
             < lax.broadcasted_iota(I32, (N_EXPERTS, N_EXPERTS), 0)).astype(BF16)
    bstart = jnp.dot(lower, nblk.astype(BF16), preferred_element_type=F32)
    bend = bstart + nblk

    @pl.when(pl.program_id(0) == 0)
    def _():
        seg_ref[...] = jnp.concatenate(
            [bstart[:, 0:1] * MOE_BLOCK, cnt[:, 0:1], nblk[:, 0:1] * MOE_BLOCK, bend[:, 0:1]],
            axis=1).astype(I32)

    rank = rank_ref[...]
    sel = rank >= 0
    pos = bstart[:, 0:1] * MOE_BLOCK + rank.astype(F32)
    slot = jnp.dot(lower, sel.astype(BF16), preferred_element_type=F32)
    w = w_ref[...]
    pos_rows, w_rows = [], []
    for k in range(TOP_K):
        m = sel & (slot == float(k))
        pos_rows.append(jnp.sum(jnp.where(m, pos, 0.0), axis=0, keepdims=True))
        w_rows.append(jnp.sum(jnp.where(m, w, 0.0), axis=0, keepdims=True))
    pos_ref[...] = jnp.concatenate(pos_rows, axis=0).astype(I32)
    wc_ref[...] = jnp.concatenate(w_rows, axis=0)


def _plan(w, rank, cnt):
    e, t = w.shape
    tr = min(1024, t)
    return pl.pallas_call(
        _plan_kernel,
        grid=(t // tr,),
        in_specs=[pl.BlockSpec((e, tr), lambda i: (0, i)),
                  pl.BlockSpec((e, tr), lambda i: (0, i)),
                  pl.BlockSpec((e, LANES), lambda i: (0, 0))],
        out_specs=[pl.BlockSpec((TOP_K, tr), lambda i: (0, i)),
                   pl.BlockSpec((TOP_K, tr), lambda i: (0, i)),
                   pl.BlockSpec((e, 4), lambda i: (0, 0))],
        out_shape=[jax.ShapeDtypeStruct((TOP_K, t), I32),
                   jax.ShapeDtypeStruct((TOP_K, t), F32),
                   jax.ShapeDtypeStruct((e, 4), I32)],
        compiler_params=_cparams(("arbitrary",)),
        name="plan",
    )(w, rank, cnt)


def _row_copy(src, src_row, dst, dst_row, sem):
    return pltpu.make_async_copy(src.at[pl.ds(src_row * ROW_WORDS, ROW_WORDS), :],
                                 dst.at[pl.ds(dst_row * ROW_WORDS, ROW_WORDS), :], sem)


def _dispatch_kernel(pos_ref, seg_ref, hp_ref, xs_ref, zero_ref, sem, zsem):
    tq = pos_ref.shape[0] // TOP_K
    blk_words = MOE_BLOCK * ROW_WORDS
    n_blocks = xs_ref.shape[0] // blk_words

    @pl.when(pl.program_id(0) == 0)
    def _():
        zero_ref[...] = jnp.zeros_like(zero_ref)
        zrow = zero_ref.at[pl.ds(0, ROW_WORDS), :]

        def per_expert(e, n):
            start = seg_ref[e, 0] + seg_ref[e, 1]
            npad = seg_ref[e, 2] - seg_ref[e, 1]

            def fill(r, c):
                pltpu.make_async_copy(
                    zrow, xs_ref.at[pl.ds((start + r) * ROW_WORDS, ROW_WORDS), :], zsem).start()
                return c

            lax.fori_loop(0, npad, fill, 0)
            return n + npad

        n_pad_rows = lax.fori_loop(0, N_EXPERTS, per_expert, 0)

        def drain_row(r, c):
            pltpu.make_async_copy(zrow, xs_ref.at[pl.ds(0, ROW_WORDS), :], zsem).wait()
            return c

        lax.fori_loop(0, n_pad_rows, drain_row, 0)
        n_used = seg_ref[N_EXPERTS - 1, 3]

        def fill_block(b, c):
            pltpu.make_async_copy(
                zero_ref, xs_ref.at[pl.ds(b * blk_words, blk_words), :], zsem).start()
            return c

        lax.fori_loop(n_used, n_blocks, fill_block, 0)

        def drain_block(b, c):
            pltpu.make_async_copy(zero_ref, xs_ref.at[pl.ds(0, blk_words), :], zsem).wait()
            return c

        lax.fori_loop(n_used, n_blocks, drain_block, 0)

    def issue(t, c):
        for k in range(TOP_K):
            _row_copy(hp_ref, t, xs_ref, pos_ref[t * TOP_K + k], sem).start(priority=k % 2)
        return c

    lax.fori_loop(0, tq, issue, 0)
    for k in range(TOP_K):
        pltpu.make_async_copy(hp_ref, xs_ref.at[pl.ds(0, tq * ROW_WORDS), :], sem).wait()


def _dispatch(pos_tm, seg, hp, n_rows):
    t = pos_tm.shape[0] // TOP_K
    tq = min(1024, t)
    return pl.pallas_call(
        _dispatch_kernel,
        grid=(t // tq,),
        in_specs=[pl.BlockSpec((tq * TOP_K,), lambda i: (i,), memory_space=pltpu.SMEM),
                  pl.BlockSpec(seg.shape, lambda i: (0, 0), memory_space=pltpu.SMEM),
                  pl.BlockSpec((tq * ROW_WORDS, LANES), lambda i: (i, 0))],
        out_specs=pl.BlockSpec(memory_space=pl.ANY),
        out_shape=jax.ShapeDtypeStruct((n_rows * ROW_WORDS, LANES), U32),
        scratch_shapes=[pltpu.VMEM((MOE_BLOCK * ROW_WORDS, LANES), U32),
                        pltpu.SemaphoreType.DMA(()), pltpu.SemaphoreType.DMA(())],
        compiler_params=_cparams(("arbitrary",)),
        name="dispatch",
    )(pos_tm, seg, hp)


def _expert_kernel(seg_ref, xs_ref, wg_ref, wu_ref, wd_ref, ys_ref,
                   wg32, wu32, wd32, wgb, wub, wdb, xbuf, ybuf, wsem, xsem, ysem):
    e = pl.program_id(0)
    n_exp = pl.num_programs(0)
    last_step = e == n_exp - 1

    def weight_copies(ex):
        slot = ex & 1
        return [pltpu.make_async_copy(src.at[ex], dst.at[slot], wsem.at[slot])
                for src, dst in ((wg_ref, wg32), (wu_ref, wu32), (wd_ref, wd32))]

    @pl.when(e == 0)
    def _():
        for cp in weight_copies(0):
            cp.start()

    for cp in weight_copies(e):
        cp.wait()
    rows = MOE_BLOCK
    blk_words = MOE_BLOCK * ROW_WORDS
    n_blocks = ys_ref.shape[0] // blk_words
    first = lax.div(seg_ref[e * 4], MOE_BLOCK)
    end = seg_ref[e * 4 + 3]
    n_used = seg_ref[(pl.num_programs(0) - 1) * 4 + 3]

    x_slots = xbuf.shape[0]
    x_ahead = x_slots - 2

    def x_copy(g):
        slot = g & (x_slots - 1)
        return pltpu.make_async_copy(xs_ref.at[pl.ds(g * blk_words, blk_words), :],
                                     xbuf.at[slot], xsem.at[slot])

    def y_copy(g):
        slot = g & 1
        return pltpu.make_async_copy(ybuf.at[slot],
                                     ys_ref.at[pl.ds(g * blk_words, blk_words), :], ysem.at[slot])

    for g0 in range(x_ahead):
        @pl.when(jnp.logical_and(e == 0, n_used > g0))
        def _():
            x_copy(g0).start()

    @pl.when(end > first)
    def _():
        wslot = e & 1
        wgb[...] = wg32[wslot].astype(BF16)
        wub[...] = wu32[wslot].astype(BF16)
        wdb[...] = wd32[wslot].astype(BF16)

    @pl.when(e + 1 < n_exp)
    def _():
        for cp in weight_copies(e + 1):
            cp.start(priority=1)

    @pl.when(end > first)
    def _():
        def block(g, c):
            slot = g & 1
            x_copy(g).wait()

            @pl.when(g + x_ahead < n_used)
            def _():
                x_copy(g + x_ahead).start()

            @pl.when(g >= 2)
            def _():
                y_copy(g - 2).wait()

            xin = xbuf.at[g & (x_slots - 1)]
            his, los = [], []
            for j in range(ROW_WORDS):
                hi, lo = _unpack_pair(xin[pl.ds(j, rows, stride=ROW_WORDS), :])
                his.append(hi.astype(BF16))
                los.append(lo.astype(BF16))
            xb = jnp.concatenate(his + los, axis=1)
            a = jnp.dot(xb, wgb[...], preferred_element_type=F32)
            u = jnp.dot(xb, wub[...], preferred_element_type=F32)
            hid = (_silu(a) * u).astype(BF16)
            out = jnp.dot(hid, wdb[...], preferred_element_type=F32).astype(BF16).astype(F32)
            half = out.shape[1] // 2
            words = _pack_pair(out[:, :half], out[:, half:])
            yout = ybuf.at[slot]
            for j in range(ROW_WORDS):
                yout[pl.ds(j, rows, stride=ROW_WORDS), :] = words[:, j * LANES:(j + 1) * LANES]
            y_copy(g).start()
            return c

        lax.fori_loop(first, end, block, 0)

    @pl.when(last_step)
    def _():
        @pl.when(n_used >= 2)
        def _():
            y_copy(n_used - 2).wait()

        @pl.when(n_used >= 1)
        def _():
            y_copy(n_used - 1).wait()

        ybuf[0] = jnp.zeros(ybuf.shape[1:], ybuf.dtype)

        def fill(g, c):
            pltpu.make_async_copy(ybuf.at[0], ys_ref.at[pl.ds(g * blk_words, blk_words), :],
                                  ysem.at[0]).start()
            return c

        lax.fori_loop(n_used, n_blocks, fill, 0)

        def drain(g, c):
            pltpu.make_async_copy(ybuf.at[0], ys_ref.at[pl.ds(0, blk_words), :], ysem.at[0]).wait()
            return c

        lax.fori_loop(n_used, n_blocks, drain, 0)


def _experts(seg_flat, xs, w_gate, w_up, w_down):
    n_exp, d, f = w_gate.shape
    blk_words = MOE_BLOCK * ROW_WORDS
    grid_spec = pltpu.PrefetchScalarGridSpec(
        num_scalar_prefetch=1,
        grid=(n_exp,),
        in_specs=[
            pl.BlockSpec(memory_space=pl.ANY),
            pl.BlockSpec(memory_space=pl.ANY),
            pl.BlockSpec(memory_space=pl.ANY),
            pl.BlockSpec(memory_space=pl.ANY),
        ],
        out_specs=pl.BlockSpec(memory_space=pl.ANY),
        scratch_shapes=[pltpu.VMEM((2, d, f), F32), pltpu.VMEM((2, d, f), F32), pltpu.VMEM((2, f, d), F32),
                        pltpu.VMEM((d, f), BF16), pltpu.VMEM((d, f), BF16), pltpu.VMEM((f, d), BF16),
                        pltpu.VMEM((4, blk_words, LANES), U32), pltpu.VMEM((2, blk_words, LANES), U32),
                        pltpu.SemaphoreType.DMA((2,)), pltpu.SemaphoreType.DMA((4,)),
                        pltpu.SemaphoreType.DMA((2,))],
    )
    return pl.pallas_call(
        _expert_kernel,
        grid_spec=grid_spec,
        out_shape=jax.ShapeDtypeStruct(xs.shape, U32),
        compiler_params=_cparams(("arbitrary",)),
        name="experts",
    )(seg_flat, xs, w_gate, w_up, w_down)


def _combine_kernel(pos_ref, nxt_ref, ys_ref, wc_ref, shared_ref, x1_ref,
                    g2_ref, nf_ref, scf_ref, shf_ref, o_ref, buf, sem):
    i = pl.program_id(0)
    tq = wc_ref.shape[1]
    slot = i & 1

    def gather(p_ref, to_slot):
        def issue(t, c):
            for k in range(TOP_K):
                pltpu.make_async_copy(
                    ys_ref.at[pl.ds(p_ref[t * TOP_K + k] * ROW_WORDS, ROW_WORDS), :],
                    buf.at[to_slot, k, pl.ds(t * ROW_WORDS, ROW_WORDS), :],
                    sem.at[to_slot]).start(priority=k % 2)
            return c

        lax.fori_loop(0, tq, issue, 0)

    @pl.when(i == 0)
    def _():
        gather(pos_ref, 0)

    @pl.when(i + 1 < pl.num_programs(0))
    def _():
        gather(nxt_ref, 1 - slot)

    for k in range(TOP_K):
        pltpu.make_async_copy(ys_ref.at[pl.ds(0, tq * ROW_WORDS), :], buf.at[slot, k],
                              sem.at[slot]).wait()

    wt = jnp.transpose(wc_ref[...])
    his, los = [], []
    for j in range(ROW_WORDS):
        acc_hi = jnp.zeros((tq, LANES), F32)
        acc_lo = jnp.zeros((tq, LANES), F32)
        for k in range(TOP_K):
            hi, lo = _unpack_pair(buf.at[slot, k][pl.ds(j, tq, stride=ROW_WORDS), :])
            wk = wt[:, k:k + 1]
            acc_hi = acc_hi + wk * hi
            acc_lo = acc_lo + wk * lo
        his.append(acc_hi)
        los.append(acc_lo)
    y = jnp.concatenate(his + los, axis=1) + shared_ref[...].astype(F32)
    x2 = x1_ref[...] + g2_ref[0] * y
    o_ref[...] = _norm_mod(x2, nf_ref[...], scf_ref[0], shf_ref[0])


def _combine(pos_tm, ys, w_c, shared, x1, g2, nf, scf, shf, seq):
    t, d = x1.shape
    tq = 128
    n_tiles = t // tq
    per_batch = seq // tq
    mod = pl.BlockSpec((1, 1, d), lambda i: (i // per_batch, 0, 0))
    return pl.pallas_call(
        _combine_kernel,
        grid=(n_tiles,),
        in_specs=[
            pl.BlockSpec((tq * TOP_K,), lambda i: (i,), memory_space=pltpu.SMEM),
            pl.BlockSpec((tq * TOP_K,), lambda i: (jnp.minimum(i + 1, n_tiles - 1),),
                         memory_space=pltpu.SMEM),
            pl.BlockSpec(memory_space=pl.ANY),
            pl.BlockSpec((TOP_K, tq), lambda i: (0, i)),
            pl.BlockSpec((tq, d), lambda i: (i, 0)),
            pl.BlockSpec((tq, d), lambda i: (i, 0)),
            mod,
            pl.BlockSpec((1, d), lambda i: (0, 0)),
            mod, mod,
        ],
        out_specs=pl.BlockSpec((tq, d), lambda i: (i, 0)),
        out_shape=jax.ShapeDtypeStruct((t, d), F32),
        scratch_shapes=[pltpu.VMEM((2, TOP_K, tq * ROW_WORDS, LANES), U32),
                        pltpu.SemaphoreType.DMA((2,))],
        compiler_params=_cparams(("arbitrary",)),
        name="combine",
    )(pos_tm, pos_tm, ys, w_c, shared, x1, g2, nf.reshape(1, d), scf, shf)


def _rope_tables(seq):
    pos = jnp.arange(seq, dtype=F32)
    inv_freq = ROPE_BASE ** (-jnp.arange(0, HEAD_DIM, 2, dtype=F32) / HEAD_DIM)
    ang = pos[:, None] * inv_freq[None, :]
    cos, sin = jnp.cos(ang), jnp.sin(ang)
    return jnp.concatenate([cos, cos], axis=1), jnp.concatenate([-sin, sin], axis=1)


def kernel(x, c, w_ada, b_ada, norm1_g, w_in, ret_gn_g, w_out, norm2_g, w_router, router_bias,
           w_gate, w_up, w_down, ws_gate, ws_up, ws_down, w_ada_final, b_ada_final, norm_f_g):
    bsz, seq, d = x.shape
    t = bsz * seq
    depth = w_ada.shape[0]
    n_heads = d // (2 * HEAD_DIM)
    sb_w = n_heads * HEAD_DIM
    cos2, sin2 = _rope_tables(seq)
    c_pad = jnp.pad(c, ((0, 8 - bsz % 8), (0, 0)))
    n_rows = t * TOP_K + N_EXPERTS * MOE_BLOCK
    as_mod = lambda v: v.reshape(bsz, 1, d)

    x2 = x.reshape(t, d)
    for l in range(depth):
        mod = _ada(c_pad, w_ada[l], b_ada[l])[:bsz]
        sh1, sc1, g1, sh2, sc2, g2 = [as_mod(m) for m in jnp.split(mod, 6, axis=-1)]
        u = _inproj(x2, norm1_g[l], sc1, sh1, w_in[l].astype(BF16), seq)
        u3 = u.reshape(bsz, seq, -1)
        sb = _sb_attention(u3, n_heads)
        rt = _retention(u3, cos2, sin2, ret_gn_g[l], n_heads, 3 * n_heads)
        wo = w_out[l].astype(BF16)
        x1, h2, hp, lt = _outproj(sb.reshape(t, sb_w), rt.reshape(t, -1), wo[:sb_w], wo[sb_w:], x2,
                                  g1, norm2_g[l], sc2, sh2, jnp.transpose(w_router[l]).astype(BF16), seq)
        w_e, rank_e, cnt, shared = _route(lt, router_bias[l], h2, ws_gate[l].astype(BF16),
                                          ws_up[l].astype(BF16), ws_down[l].astype(BF16))
        pos_c, w_c, seg = _plan(w_e, rank_e, cnt)
        pos_tm = jnp.transpose(pos_c).reshape(-1)
        xs = _dispatch(pos_tm, seg, hp, n_rows)
        ys = _experts(seg.reshape(-1), xs, w_gate[l], w_up[l], w_down[l])
        if l + 1 < depth:
            raise NotImplementedError("only the final layer fuses the output norm")
        modf = _ada(c_pad, w_ada_final, b_ada_final)[:bsz]
        shf, scf = [as_mod(m) for m in jnp.split(modf, 2, axis=-1)]
        x2 = _combine(pos_tm, ys, w_c, shared, x1, g2, norm_f_g, scf, shf, seq)
    return x2.reshape(bsz, seq, d)
```
